```python
import numpy as np
import jax, jax.numpy as jnp
from jax import lax

D_MODEL = 1024
BATCH = 8
SEQ = 4096
DEPTH = 1

NSA_HEADS = 8
NSA_GROUPS = 2
NSA_HPG = NSA_HEADS // NSA_GROUPS
NSA_DH = 64
CMP_STRIDE = 16
CMP_LEN = 2 * CMP_STRIDE
CMP_HIDDEN = 128
SEL_BLOCK = 64
SEL_TOPK = 16
WINDOW = 512
NSA_QBLOCK = 64
FORCE_BONUS = 1e4
RET_HEADS = 8
RET_DK = 64
RET_DV = 64
RET_CHUNK = 128
ROPE_BASE = 10000.0
D_FF = 2816
EPS = 1e-6
NEG = -1e30

NSA_QW = NSA_HEADS * NSA_DH
NSA_KVW = NSA_GROUPS * NSA_DH
RET_QW = RET_HEADS * RET_DK
RET_VW = RET_HEADS * RET_DV
IN_WIDTHS = [NSA_QW, NSA_KVW, NSA_KVW, NSA_KVW, NSA_KVW, NSA_KVW, NSA_KVW, 3 * NSA_HEADS,
             RET_QW, RET_QW, RET_VW, RET_VW, D_MODEL, D_MODEL]
IN_W = sum(IN_WIDTHS)

kernel_name = "hybrid_nsa_retention_macaron_adaln"


def rmsnorm(x, w):
    xf = x.astype(jnp.float32)
    y = xf * lax.rsqrt(jnp.mean(xf * xf, axis=-1, keepdims=True) + EPS)
    return (y * w.astype(jnp.float32)).astype(x.dtype)


def modulate(h, shift, scale):
    return h * (1 + scale[:, None, :]) + shift[:, None, :]


def swiglu(h, w_in, w_out):
    a, b = jnp.split(h @ w_in, 2, axis=-1)
    return (jax.nn.silu(a) * b) @ w_out


def masked_softmax(s, mask):
    return jax.nn.softmax(jnp.where(mask, s.astype(jnp.float32), NEG), axis=-1)


def rope(x, pos):
    half = x.shape[-1] // 2
    inv = ROPE_BASE ** (-jnp.arange(half, dtype=jnp.float32) / half)
    ang = pos.astype(jnp.float32)[:, None] * inv[None, :]
    cos = jnp.cos(ang)[None, :, None, :]
    sin = jnp.sin(ang)[None, :, None, :]
    x1 = x[..., :half].astype(jnp.float32)
    x2 = x[..., half:].astype(jnp.float32)
    return jnp.concatenate([x1 * cos - x2 * sin, x1 * sin + x2 * cos], axis=-1).astype(x.dtype)


def compress(raw, pe, w1, w2):
    B, S, G, dh = raw.shape
    ch = raw.reshape(B, S // CMP_STRIDE, CMP_STRIDE, G, dh)
    blocks = jnp.concatenate([ch[:, :-1], ch[:, 1:]], axis=2)
    blocks = blocks + pe[None, None, :, None, :]
    nc = blocks.shape[1]
    flat = blocks.transpose(0, 1, 3, 2, 4).reshape(B, nc, G, CMP_LEN * dh)
    return jax.nn.silu(flat @ w1) @ w2


def nsa_attention(q, k_cmp, v_cmp, k_slc, v_slc, k_win, v_win, gates,
                  cmp_k_pe, cmp_k_w1, cmp_k_w2, cmp_v_pe, cmp_v_w1, cmp_v_w2):
    B, S = q.shape[:2]
    G, HPG, dh, QB = NSA_GROUPS, NSA_HPG, NSA_DH, NSA_QBLOCK
    kc = compress(k_cmp, cmp_k_pe, cmp_k_w1, cmp_k_w2)
    vc = compress(v_cmp, cmp_v_pe, cmp_v_w1, cmp_v_w2)
    nc = kc.shape[1]
    ns = S // SEL_BLOCK
    n_sel = min(SEL_TOPK, ns)
    cmp_end = jnp.arange(nc) * CMP_STRIDE + CMP_LEN - 1
    c_start = np.arange(nc) * CMP_STRIDE
    s_start = np.arange(ns) * SEL_BLOCK
    overlap = jnp.asarray((c_start[:, None] < s_start[None, :] + SEL_BLOCK) &
                          (c_start[:, None] + CMP_LEN > s_start[None, :]), dtype=jnp.float32)
    ks_blk = k_slc.reshape(B, ns, SEL_BLOCK, G, dh).transpose(0, 3, 1, 2, 4)
    vs_blk = v_slc.reshape(B, ns, SEL_BLOCK, G, dh).transpose(0, 3, 1, 2, 4)
    kw_pad = jnp.pad(k_win, ((0, 0), (WINDOW, 0), (0, 0), (0, 0)))
    vw_pad = jnp.pad(v_win, ((0, 0), (WINDOW, 0), (0, 0), (0, 0)))
    scale = NSA_DH ** -0.5
    b_ix = jnp.arange(B)[:, None, None, None]
    g_ix = jnp.arange(G)[None, :, None, None]
    j_blk = jnp.arange(ns)
    sel_off = jnp.arange(SEL_BLOCK)

    def block(nb):
        q0 = nb * QB
        t = q0 + jnp.arange(QB)
        qb = lax.dynamic_slice_in_dim(q, q0, QB, axis=1).reshape(B, QB, G, HPG, dh) * scale
        m_c = cmp_end[None, :] <= t[:, None]
        p_c = masked_softmax(jnp.einsum('bqghd,bngd->bghqn', qb, kc), m_c)
        p_c = p_c * jnp.any(m_c, axis=-1)[:, None].astype(jnp.float32)
        o_cmp = jnp.einsum('bghqn,bngd->bqghd', p_c, vc)
        imp = jnp.einsum('bghqn,nj->bgqj', p_c, overlap)
        cur = t // SEL_BLOCK
        forced = (j_blk[None, :] == 0) | (j_blk[None, :] == cur[:, None]) | (j_blk[None, :] == cur[:, None] - 1)
        valid = j_blk[None, :] * SEL_BLOCK <= t[:, None]
        imp = jnp.where(forced, imp + FORCE_BONUS, imp)
        imp = jnp.where(valid, imp, -FORCE_BONUS)
        _, idx = lax.top_k(imp, n_sel)
        kg = ks_blk[b_ix, g_ix, idx]
        vg = vs_blk[b_ix, g_ix, idx]
        kpos = idx[..., None] * SEL_BLOCK + sel_off
        m_s = (kpos <= t[None, None, :, None, None])[:, :, None]
        s_s = jnp.einsum('bqghd,bgqnld->bghqnl', qb, kg)
        s_s = jnp.where(m_s, s_s.astype(jnp.float32), NEG).reshape(B, G, HPG, QB, n_sel * SEL_BLOCK)
        p_s = jax.nn.softmax(s_s, axis=-1).reshape(B, G, HPG, QB, n_sel, SEL_BLOCK)
        o_slc = jnp.einsum('bghqnl,bgqnld->bqghd', p_s, vg)
        kw = lax.dynamic_slice_in_dim(kw_pad, q0, WINDOW + QB, axis=1)
        vw = lax.dynamic_slice_in_dim(vw_pad, q0, WINDOW + QB, axis=1)
        wpos = q0 - WINDOW + jnp.arange(WINDOW + QB)
        m_w = (wpos[None, :] <= t[:, None]) & (wpos[None, :] > t[:, None] - WINDOW) & (wpos[None, :] >= 0)
        p_w = masked_softmax(jnp.einsum('bqghd,bkgd->bghqk', qb, kw), m_w)
        o_win = jnp.einsum('bghqk,bkgd->bqghd', p_w, vw)
        g = lax.dynamic_slice_in_dim(gates, q0, QB, axis=1).reshape(B, QB, G, HPG, 3).astype(jnp.float32)
        o = g[..., 0:1] * o_cmp + g[..., 1:2] * o_slc + g[..., 2:3] * o_win
        return o.reshape(B, QB, G * HPG * dh).astype(q.dtype)

    out = lax.map(block, jnp.arange(S // QB))
    return out.transpose(1, 0, 2, 3).reshape(B, S, NSA_QW)


def retention(q, k, v, pos):
    B, S, H, dk = q.shape
    dv = v.shape[-1]
    C = RET_CHUNK
    n_chunk = S // C
    q = rope(q, pos)
    k = rope(k, pos) * (dk ** -0.5)
    log_g = jnp.log(1.0 - 2.0 ** (-5.0 - jnp.arange(H, dtype=jnp.float32)))
    i = jnp.arange(C, dtype=jnp.float32)
    diff = i[:, None] - i[None, :]
    decay_mat = jnp.where(diff >= 0, jnp.exp(log_g[:, None, None] * jnp.maximum(diff, 0.0)), 0.0)
    q_decay = jnp.exp(log_g[:, None] * (i + 1.0))[..., None]
    k_decay = jnp.exp(log_g[:, None] * (C - 1.0 - i))[..., None]
    chunk_decay = jnp.exp(log_g * C)[:, None, None]
    qc = q.reshape(B, n_chunk, C, H, dk).transpose(1, 0, 3, 2, 4)
    kc = k.reshape(B, n_chunk, C, H, dk).transpose(1, 0, 3, 2, 4)
    vc = v.reshape(B, n_chunk, C, H, dv).transpose(1, 0, 3, 2, 4)

    def step(state, inp):
        qi, ki, vi = inp
        inner = jnp.einsum('bhid,bhjd->bhij', qi, ki) * decay_mat
        o = jnp.einsum('bhij,bhjv->bhiv', inner, vi) + jnp.einsum('bhid,bhdv->bhiv', qi, state) * q_decay
        state = state * chunk_decay + jnp.einsum('bhjd,bhjv->bhdv', ki * k_decay, vi)
        return state, o

    state0 = jnp.zeros((B, H, dk, dv), jnp.float32)
    _, o = lax.scan(step, state0, (qc, kc, vc))
    return o.transpose(1, 0, 3, 2, 4).reshape(B, S, H, dv)


def setup_inputs(seed: int = 0) -> dict:
    key = jax.random.key(seed)
    ks = jax.random.split(key, 26)
    f32 = jnp.float32
    L, D = DEPTH, D_MODEL

    def nrm(k, shape, s):
        return jax.random.normal(k, shape, f32) * s

    return {
        "x": nrm(ks[0], (BATCH, SEQ, D), 1.0),
        "c": nrm(ks[1], (BATCH, D), 1.0),
        "ada_w": nrm(ks[2], (L, D, 9 * D), D ** -0.5),
        "ada_b": nrm(ks[3], (L, 9 * D), 0.01),
        "norm1_w": 1.0 + nrm(ks[4], (L, D), 0.02),
        "ffn1_w_in": nrm(ks[5], (L, D, 2 * D_FF), D ** -0.5),
        "ffn1_w_out": nrm(ks[6], (L, D_FF, D), D_FF ** -0.5),
        "norm2_w": 1.0 + nrm(ks[7], (L, D), 0.02),
        "w_in": nrm(ks[8], (L, D, IN_W), D ** -0.5),
        "cmp_k_pe": nrm(ks[9], (L, CMP_LEN, NSA_DH), 0.1),
        "cmp_k_w1": nrm(ks[10], (L, CMP_LEN * NSA_DH, CMP_HIDDEN), (CMP_LEN * NSA_DH) ** -0.5),
        "cmp_k_w2": nrm(ks[11], (L, CMP_HIDDEN, NSA_DH), CMP_HIDDEN ** -0.5),
        "cmp_v_pe": nrm(ks[12], (L, CMP_LEN, NSA_DH), 0.1),
        "cmp_v_w1": nrm(ks[13], (L, CMP_LEN * NSA_DH, CMP_HIDDEN), (CMP_LEN * NSA_DH) ** -0.5),
        "cmp_v_w2": nrm(ks[14], (L, CMP_HIDDEN, NSA_DH), CMP_HIDDEN ** -0.5),
        "ret_norm_w": 1.0 + nrm(ks[15], (L, RET_VW), 0.02),
        "w_nsa_up": nrm(ks[16], (L, NSA_QW, D), NSA_QW ** -0.5),
        "w_ret_up": nrm(ks[17], (L, RET_VW, D), RET_VW ** -0.5),
        "w_out": nrm(ks[18], (L, D, D), D ** -0.5),
        "norm3_w": 1.0 + nrm(ks[19], (L, D), 0.02),
        "ffn2_w_in": nrm(ks[20], (L, D, 2 * D_FF), D ** -0.5),
        "ffn2_w_out": nrm(ks[21], (L, D_FF, D), D_FF ** -0.5),
        "final_norm_w": 1.0 + nrm(ks[22], (D,), 0.02),
    }


def reference(x, c, ada_w, ada_b, norm1_w, ffn1_w_in, ffn1_w_out, norm2_w, w_in,
              cmp_k_pe, cmp_k_w1, cmp_k_w2, cmp_v_pe, cmp_v_w1, cmp_v_w2, ret_norm_w,
              w_nsa_up, w_ret_up, w_out, norm3_w, ffn2_w_in, ffn2_w_out, final_norm_w):
    B, S, D = x.shape
    pos = jnp.arange(S)
    split_pts = np.cumsum(IN_WIDTHS)[:-1].tolist()
    h = x
    for l in range(DEPTH):
        mod = jax.nn.silu(c) @ ada_w[l] + ada_b[l]
        sh1, sc1, gt1, sh2, sc2, gt2, sh3, sc3, gt3 = jnp.split(mod, 9, axis=-1)
        u = modulate(rmsnorm(h, norm1_w[l]), sh1, sc1)
        h = h + 0.5 * gt1[:, None, :] * swiglu(u, ffn1_w_in[l], ffn1_w_out[l])
        u = modulate(rmsnorm(h, norm2_w[l]), sh2, sc2)
        proj = u @ w_in[l]
        (nq, nkc, nvc, nks, nvs, nkw, nvw, ngt, rq, rk, rv, rg, ga, gb) = jnp.split(proj, split_pts, axis=-1)
        g4 = lambda a: a.reshape(B, S, NSA_GROUPS, NSA_DH)
        o_nsa = nsa_attention(nq.reshape(B, S, NSA_HEADS, NSA_DH), g4(nkc), g4(nvc), g4(nks), g4(nvs),
                              g4(nkw), g4(nvw), jax.nn.sigmoid(ngt).reshape(B, S, NSA_HEADS, 3),
                              cmp_k_pe[l], cmp_k_w1[l], cmp_k_w2[l], cmp_v_pe[l], cmp_v_w1[l], cmp_v_w2[l])
        o_ret = retention(rq.reshape(B, S, RET_HEADS, RET_DK), rk.reshape(B, S, RET_HEADS, RET_DK),
                          rv.reshape(B, S, RET_HEADS, RET_DV), pos)
        mu = jnp.mean(o_ret, axis=-1, keepdims=True)
        var = jnp.mean(jnp.square(o_ret - mu), axis=-1, keepdims=True)
        o_ret = ((o_ret - mu) * lax.rsqrt(var + EPS)).reshape(B, S, RET_VW) * ret_norm_w[l].astype(jnp.float32)
        o_ret = (o_ret * jax.nn.silu(rg.astype(jnp.float32))).astype(x.dtype)
        y_nsa = o_nsa @ w_nsa_up[l]
        y_ret = o_ret @ w_ret_up[l]
        mixed = jax.nn.sigmoid(ga) * y_nsa + jax.nn.sigmoid(gb) * y_ret
        h = h + gt2[:, None, :] * (mixed @ w_out[l])
        u = modulate(rmsnorm(h, norm3_w[l]), sh3, sc3)
        h = h + 0.5 * gt3[:, None, :] * swiglu(u, ffn2_w_in[l], ffn2_w_out[l])
    return rmsnorm(h, final_norm_w)
```

```python
import functools

import numpy as np
import jax
import jax.numpy as jnp
from jax import lax
from jax.experimental import pallas as pl
from jax.experimental.pallas import tpu as pltpu

F32 = jnp.float32
BF16 = jnp.bfloat16

NSA_HEADS = 8
NSA_GROUPS = 2
NSA_HPG = NSA_HEADS // NSA_GROUPS
NSA_DH = 64
CMP_STRIDE = 16
CMP_LEN = 2 * CMP_STRIDE
CMP_HIDDEN = 128
SEL_BLOCK = 64
SEL_TOPK = 16
WINDOW = 512
FORCE_BONUS = 1e4
RET_HEADS = 8
RET_DK = 64
RET_DV = 64
ROPE_BASE = 10000.0
D_FF = 2816
EPS = 1e-6
NEG = -1e30

NSA_QW = NSA_HEADS * NSA_DH
NSA_KVW = NSA_GROUPS * NSA_DH
RET_QW = RET_HEADS * RET_DK
RET_VW = RET_HEADS * RET_DV

LANES = 128
HALF = LANES // 2
VMEM_LIMIT = 56 * 1024 * 1024
FFN_CHUNK = 256
RET_CHUNK = 256
ATT_Q = 256


def _dot(a, b):
    return jnp.dot(a, b, preferred_element_type=F32)


def _dot_nt(a, b):
    return lax.dot_general(a, b, (((1,), (1,)), ((), ())), preferred_element_type=F32)


def _dot_tn(a, b):
    return lax.dot_general(a, b, (((0,), (0,)), ((), ())), preferred_element_type=F32)


def _silu(a):
    return a * jax.nn.sigmoid(a)


def _norm_mod(x, nw, sc, sh):
    ms = jnp.mean(x * x, axis=-1, keepdims=True)
    y = x * lax.rsqrt(ms + EPS) * nw
    return y * (1.0 + sc) + sh


def _full_spec(shape):
    zeros = (0,) * len(shape)
    return pl.BlockSpec(shape, lambda *_: zeros)


def _params(sem):
    return pltpu.CompilerParams(dimension_semantics=sem, vmem_limit_bytes=VMEM_LIMIT)


def _mod_kernel(c_ref, w_ref, b_ref, o_ref):
    c = c_ref[...]
    o_ref[...] = _dot(_silu(c).astype(BF16), w_ref[...].astype(BF16)) + b_ref[...]


def _modulation(c, ada_w, ada_b):
    B, D = c.shape
    N = ada_w.shape[1]
    tn = N // 8
    return pl.pallas_call(
        _mod_kernel,
        grid=(N // tn,),
        in_specs=[pl.BlockSpec((B, D), lambda j: (0, 0)),
                  pl.BlockSpec((D, tn), lambda j: (0, j)),
                  pl.BlockSpec((1, tn), lambda j: (0, j))],
        out_specs=pl.BlockSpec((B, tn), lambda j: (0, j)),
        out_shape=jax.ShapeDtypeStruct((B, N), F32),
        compiler_params=_params(("arbitrary",)),
        name="modulation",
    )(c, ada_w, ada_b.reshape(1, N))


def _ffn_kernel(x_ref, mod_ref, nw_ref, fw_ref, wa_ref, wb_ref, wo_ref, o_ref, u_scr, acc_scr,
                *, mod_base, final_norm):
    x = x_ref[0]
    sh = mod_ref[0, mod_base:mod_base + 1, :]
    sc = mod_ref[0, mod_base + 1:mod_base + 2, :]
    gt = mod_ref[0, mod_base + 2:mod_base + 3, :]
    u_scr[...] = _norm_mod(x, nw_ref[...], sc, sh).astype(BF16)
    acc_scr[...] = jnp.zeros_like(acc_scr)

    def body(ci, carry):
        u = u_scr[...]
        a = _dot(u, wa_ref[ci])
        b = _dot(u, wb_ref[ci])
        acc_scr[...] += _dot((_silu(a) * b).astype(BF16), wo_ref[ci])
        return carry

    lax.fori_loop(0, wa_ref.shape[0], body, 0)
    y = x + 0.5 * gt * acc_scr[...]
    if final_norm:
        ms = jnp.mean(y * y, axis=-1, keepdims=True)
        y = y * lax.rsqrt(ms + EPS) * fw_ref[...]
    o_ref[0] = y


def _ffn(x, mod, norm_w, final_w, w_in, w_out, *, mod_base, final_norm, tm):
    B, S, D = x.shape
    dff = w_out.shape[0]
    nch = dff // FFN_CHUNK
    wa = w_in[:, :dff].astype(BF16).reshape(D, nch, FFN_CHUNK).transpose(1, 0, 2)
    wb = w_in[:, dff:].astype(BF16).reshape(D, nch, FFN_CHUNK).transpose(1, 0, 2)
    wo = w_out.astype(BF16).reshape(nch, FFN_CHUNK, D)
    kern = functools.partial(_ffn_kernel, mod_base=mod_base, final_norm=final_norm)
    return pl.pallas_call(
        kern,
        grid=(B, S // tm),
        in_specs=[pl.BlockSpec((1, tm, D), lambda b, i: (b, i, 0)),
                  pl.BlockSpec((1, 9, D), lambda b, i: (b, 0, 0)),
                  _full_spec((1, D)), _full_spec((1, D)),
                  _full_spec(wa.shape), _full_spec(wb.shape), _full_spec(wo.shape)],
        out_specs=pl.BlockSpec((1, tm, D), lambda b, i: (b, i, 0)),
        out_shape=jax.ShapeDtypeStruct((B, S, D), F32),
        scratch_shapes=[pltpu.VMEM((tm, D), BF16), pltpu.VMEM((tm, D), F32)],
        compiler_params=_params(("arbitrary", "arbitrary")),
        name="ffn_final" if final_norm else "ffn",
    )(x, mod, norm_w.reshape(1, D), final_w.reshape(1, D), wa, wb, wo)


def _proj_kernel(h_ref, mod_ref, nw_ref, wq_ref, wkv_ref, wg_ref, wr_ref, cos_ref, sin_ref,
                 q_ref, cv_ref, kaug_ref, vs_ref, kw_ref, vw_ref, g_ref, rq_ref, rk_ref, rv_ref,
                 u_scr):
    tm = h_ref.shape[1]
    x = h_ref[0]
    sh = mod_ref[0, 3:4, :]
    sc = mod_ref[0, 4:5, :]
    u_scr[...] = _norm_mod(x, nw_ref[...], sc, sh).astype(BF16)
    u = u_scr[...]

    q_ref[0] = _dot(u, wq_ref[...]).astype(BF16)

    kv = _dot(u, wkv_ref[...])
    cv_ref[0, 0] = kv[:, 0:LANES].astype(BF16)
    cv_ref[1, 0] = kv[:, LANES:2 * LANES].astype(BF16)
    ks = kv[:, 2 * LANES:3 * LANES]
    lane = lax.broadcasted_iota(jnp.int32, (tm, LANES), 1)
    row = lax.broadcasted_iota(jnp.int32, (tm, LANES), 0)
    low = lane < HALF
    blk = (pl.program_id(1) * tm + row) // SEL_BLOCK
    onehot = jnp.where(lane - HALF == blk, 1.0, 0.0)
    kaug_ref[0, 0] = jnp.where(low, ks, onehot).astype(BF16)
    kaug_ref[0, 1] = jnp.where(low, pltpu.roll(ks, HALF, 1), onehot).astype(BF16)
    vs_ref[0] = kv[:, 3 * LANES:4 * LANES].astype(BF16)
    kw_ref[0] = kv[:, 4 * LANES:5 * LANES].astype(BF16)
    vw_ref[0] = kv[:, 5 * LANES:6 * LANES].astype(BF16)

    g_ref[0] = jax.nn.sigmoid(_dot(u, wg_ref[...]))

    r = _dot(u, wr_ref[...])
    cos = cos_ref[...]
    sin = sin_ref[...]
    first = (lane & (RET_DK // 2)) == 0
    npair = RET_QW // LANES
    for j in range(2 * npair):
        xb = r[:, j * LANES:(j + 1) * LANES]
        partner = jnp.where(first, pltpu.roll(xb, LANES - RET_DK // 2, 1), pltpu.roll(xb, RET_DK // 2, 1))
        y = (xb * cos + partner * sin).astype(BF16)
        if j < npair:
            rq_ref[0, :, j * LANES:(j + 1) * LANES] = y
        else:
            rk_ref[0, :, (j - npair) * LANES:(j - npair + 1) * LANES] = y
    rv_ref[0] = r[:, 2 * RET_QW:2 * RET_QW + RET_VW].astype(BF16)


def _projection(h, mod, norm_w, w_in, *, tm):
    B, S, D = h.shape
    o = np.cumsum([0, NSA_QW] + [NSA_KVW] * 6 + [3 * NSA_HEADS, RET_QW, RET_QW, RET_VW])
    scale_q = NSA_DH ** -0.5
    scale_k = RET_DK ** -0.5
    wq = (w_in[:, o[0]:o[1]] * scale_q).astype(BF16)
    wkv = w_in[:, o[1]:o[7]].astype(BF16)
    wg = jnp.pad(w_in[:, o[7]:o[8]], ((0, 0), (0, LANES - 3 * NSA_HEADS))).astype(BF16)
    wr = jnp.concatenate([w_in[:, o[8]:o[9]], w_in[:, o[9]:o[10]] * scale_k, w_in[:, o[10]:o[11]]],
                         axis=1).astype(BF16)
    half = RET_DK // 2
    lane = jnp.arange(LANES)
    inv = ROPE_BASE ** (-jnp.arange(half, dtype=F32) / half)
    ang = jnp.arange(S).astype(F32)[:, None] * inv[lane % half][None, :]
    cos_t = jnp.cos(ang)
    sin_t = jnp.where((lane % RET_DK) < half, -jnp.sin(ang), jnp.sin(ang))
    tok = lambda w: pl.BlockSpec((1, tm, w), lambda b, i: (b, i, 0))
    outs = pl.pallas_call(
        _proj_kernel,
        grid=(B, S // tm),
        in_specs=[tok(D), pl.BlockSpec((1, 9, D), lambda b, i: (b, 0, 0)), _full_spec((1, D)),
                  _full_spec(wq.shape), _full_spec(wkv.shape), _full_spec(wg.shape), _full_spec(wr.shape),
                  pl.BlockSpec((tm, LANES), lambda b, i: (i, 0)), pl.BlockSpec((tm, LANES), lambda b, i: (i, 0))],
        out_specs=[tok(NSA_QW),
                   pl.BlockSpec((2, 1, tm, LANES), lambda b, i: (0, b, i, 0)),
                   pl.BlockSpec((1, NSA_GROUPS, tm, LANES), lambda b, i: (b, 0, i, 0)),
                   tok(LANES), tok(LANES), tok(LANES), tok(LANES),
                   tok(RET_QW), tok(RET_QW), tok(RET_VW)],
        out_shape=[jax.ShapeDtypeStruct((B, S, NSA_QW), BF16),
                   jax.ShapeDtypeStruct((2, B, S, LANES), BF16),
                   jax.ShapeDtypeStruct((B, NSA_GROUPS, S, LANES), BF16),
                   jax.ShapeDtypeStruct((B, S, LANES), BF16),
                   jax.ShapeDtypeStruct((B, S, LANES), BF16),
                   jax.ShapeDtypeStruct((B, S, LANES), BF16),
                   jax.ShapeDtypeStruct((B, S, LANES), F32),
                   jax.ShapeDtypeStruct((B, S, RET_QW), BF16),
                   jax.ShapeDtypeStruct((B, S, RET_QW), BF16),
                   jax.ShapeDtypeStruct((B, S, RET_VW), BF16)],
        scratch_shapes=[pltpu.VMEM((tm, D), BF16)],
        compiler_params=_params(("arbitrary", "arbitrary")),
        name="projection",
    )(h, mod, norm_w.reshape(1, D), wq, wkv, wg, wr, cos_t, sin_t)
    return outs


def _compress_kernel(ch_ref, w1c_ref, pe_ref, w1_ref, w2_ref, o_ref):
    ch = ch_ref[0, 0]
    nc = ch.shape[0]
    ab = _dot(ch, w1c_ref[0])
    pt = _dot(pe_ref[0], w1_ref[0])[0:1, :]
    pt2 = jnp.concatenate([pt, pt], axis=1)
    nh = NSA_GROUPS * CMP_HIDDEN
    hid = ab[:, :nh] + pltpu.roll(ab[:, nh:], nc - 1, 0) + pt2
    out = _dot(_silu(hid).astype(BF16), w2_ref[0])
    row = lax.broadcasted_iota(jnp.int32, out.shape, 0)
    o_ref[0, 0] = jnp.where(row < nc - 1, out, 0.0).astype(BF16)


def _compress(cv, pe, w1, w2):
    _, B, S, _ = cv.shape
    nc = S // CMP_STRIDE
    width = CMP_STRIDE * LANES
    chunks = cv.reshape(2, B, nc, width)
    eye = jnp.eye(NSA_GROUPS, dtype=F32)
    nh = NSA_GROUPS * CMP_HIDDEN

    def expand(w):
        w = w.reshape(CMP_STRIDE, NSA_DH, CMP_HIDDEN)
        return jnp.einsum("ldj,gh->lgdhj", w, eye).reshape(width, nh)

    half = CMP_STRIDE * NSA_DH
    w1c = jnp.stack([jnp.concatenate([expand(w[:half]), expand(w[half:])], axis=1) for w in w1]).astype(BF16)
    w2bd = jnp.stack([jnp.einsum("jd,gh->gjhd", w, eye).reshape(nh, NSA_KVW) for w in w2]).astype(BF16)
    pe8 = jnp.stack([jnp.broadcast_to(p.reshape(1, CMP_LEN * NSA_DH), (8, CMP_LEN * NSA_DH)) for p in pe]).astype(BF16)
    w1s = jnp.stack(w1).astype(BF16)
    sel = lambda shape: pl.BlockSpec((1,) + shape, lambda t, b: (t,) + (0,) * len(shape))
    return pl.pallas_call(
        _compress_kernel,
        grid=(2, B),
        in_specs=[pl.BlockSpec((1, 1, nc, width), lambda t, b: (t, b, 0, 0)),
                  sel(w1c.shape[1:]), sel(pe8.shape[1:]), sel(w1s.shape[1:]), sel(w2bd.shape[1:])],
        out_specs=pl.BlockSpec((1, 1, nc, LANES), lambda t, b: (t, b, 0, 0)),
        out_shape=jax.ShapeDtypeStruct((2, B, nc, LANES), BF16),
        compiler_params=_params(("arbitrary", "arbitrary")),
        name="compress",
    )(chunks, w1c, pe8, w1s, w2bd)


def _select_kernel(q_ref, kvc_ref, g_ref, ovl_ref, qaug_ref, ocmp_ref, *, n_sel):
    Q = q_ref.shape[1]
    NC = kvc_ref.shape[2]
    q0 = pl.program_id(1) * Q
    kcf = kvc_ref[0, 0].astype(F32)
    vc = kvc_ref[1, 0]
    kcr = pltpu.roll(kcf, HALF, 1)
    lowk = lax.broadcasted_iota(jnp.int32, (NC, LANES), 1) < HALF
    k_even = [jnp.where(lowk, kcf, 0.0).astype(BF16), jnp.where(lowk, kcr, 0.0).astype(BF16)]
    k_odd = [jnp.where(lowk, 0.0, kcr).astype(BF16), jnp.where(lowk, 0.0, kcf).astype(BF16)]
    gates = g_ref[0]
    lowq = lax.broadcasted_iota(jnp.int32, (Q, LANES), 1) < HALF
    n_idx = lax.broadcasted_iota(jnp.int32, (Q, NC), 1)
    t_idx = q0 + lax.broadcasted_iota(jnp.int32, (Q, NC), 0)
    allowed = n_idx * CMP_STRIDE + (CMP_LEN - 1) <= t_idx
    ovl = ovl_ref[...]
    nb = LANES - HALF
    j_blk = lax.broadcasted_iota(jnp.int32, (nb, Q), 0)
    t_blk = q0 + lax.broadcasted_iota(jnp.int32, (nb, Q), 1)
    cur = t_blk // SEL_BLOCK
    forced = (j_blk == 0) | (j_blk == cur) | (j_blk == cur - 1)
    valid = j_blk * SEL_BLOCK <= t_blk

    for g in range(NSA_GROUPS):
        psum = jnp.zeros((Q, NC), F32)
        for pp in range(NSA_HPG // 2):
            pb = g * (NSA_HPG // 2) + pp
            qpair = q_ref[0, :, pb * LANES:(pb + 1) * LANES]
            outs = []
            for kk in (k_even[g], k_odd[g]):
                s = jnp.where(allowed, _dot_nt(qpair, kk), NEG)
                mx = jnp.max(s, axis=1, keepdims=True)
                ex = jnp.where(allowed, jnp.exp(s - mx), 0.0)
                den = jnp.sum(ex, axis=1, keepdims=True)
                p = ex / jnp.where(den > 0.0, den, 1.0)
                psum = psum + p
                outs.append(_dot(p.astype(BF16), vc))
            if g == 0:
                a, b = outs[0], pltpu.roll(outs[1], HALF, 1)
            else:
                a, b = pltpu.roll(outs[0], HALF, 1), outs[1]
            he = 2 * pb
            gate = jnp.where(lowq, gates[:, 3 * he:3 * he + 1], gates[:, 3 * he + 3:3 * he + 4])
            ocmp_ref[0, :, pb * LANES:(pb + 1) * LANES] = jnp.where(lowq, a, b) * gate

        hi = psum.astype(BF16)
        lo = (psum - hi.astype(F32)).astype(BF16)
        imp = (_dot_nt(ovl, hi) + _dot_nt(ovl, lo))[HALF:, :]
        imp = jnp.where(forced, imp + FORCE_BONUS, imp)
        imp = jnp.where(valid, imp, -FORCE_BONUS)
        rank = jnp.zeros((nb, Q), jnp.int32)
        for jp in range(nb):
            r = imp[jp:jp + 1, :]
            rank = rank + jnp.where(j_blk > jp, jnp.where(r >= imp, 1, 0), jnp.where(r > imp, 1, 0))
        sel = (rank < n_sel) & valid
        neg_t = jnp.concatenate([jnp.zeros((HALF, Q), F32), jnp.where(sel, 0.0, NEG)], axis=0)
        negm = neg_t.T
        for pp in range(NSA_HPG // 2):
            pb = g * (NSA_HPG // 2) + pp
            qpair = q_ref[0, :, pb * LANES:(pb + 1) * LANES].astype(F32)
            qaug_ref[0, 2 * pb] = jnp.where(lowq, qpair, negm).astype(BF16)
            qaug_ref[0, 2 * pb + 1] = jnp.where(lowq, pltpu.roll(qpair, HALF, 1), negm).astype(BF16)


def _select(q, kvc, gates):
    B, S, _ = q.shape
    Q = min(ATT_Q, S)
    NC = kvc.shape[2]
    nc, ns = NC - 1, S // SEL_BLOCK
    assert ns <= LANES - HALF
    c_start = np.arange(nc) * CMP_STRIDE
    s_start = np.arange(ns) * SEL_BLOCK
    overlap = ((c_start[:, None] < s_start[None, :] + SEL_BLOCK) &
               (c_start[:, None] + CMP_LEN > s_start[None, :])).astype(np.float32)
    ovl = np.zeros((LANES, NC), np.float32)
    ovl[HALF:HALF + ns, :nc] = overlap.T
    kern = functools.partial(_select_kernel, n_sel=min(SEL_TOPK, ns))
    return pl.pallas_call(
        kern,
        grid=(B, S // Q),
        in_specs=[pl.BlockSpec((1, Q, NSA_QW), lambda b, i: (b, i, 0)),
                  pl.BlockSpec((2, 1, NC, LANES), lambda b, i: (0, b, 0, 0)),
                  pl.BlockSpec((1, Q, LANES), lambda b, i: (b, i, 0)),
                  _full_spec((LANES, NC))],
        out_specs=[pl.BlockSpec((1, NSA_HEADS, Q, LANES), lambda b, i: (b, 0, i, 0)),
                   pl.BlockSpec((1, Q, NSA_QW), lambda b, i: (b, i, 0))],
        out_shape=[jax.ShapeDtypeStruct((B, NSA_HEADS, S, LANES), BF16),
                   jax.ShapeDtypeStruct((B, S, NSA_QW), F32)],
        compiler_params=_params(("arbitrary", "arbitrary")),
        name="select",
    )(q, kvc, gates, jnp.asarray(ovl, BF16))


def _flash_kernel(qaug_ref, kaug_ref, vs_ref, kw_ref, vw_ref, g_ref, ocmp_ref, o_ref,
                  m_scr, l_scr, acc_scr, oslc_scr):
    Q = qaug_ref.shape[2]
    R = NSA_HPG * Q
    KT = Q
    qi = pl.program_id(1)
    q0 = qi * Q
    gates = g_ref[0]
    lowq = lax.broadcasted_iota(jnp.int32, (Q, LANES), 1) < HALF
    lowr = lax.broadcasted_iota(jnp.int32, (R, LANES), 1) < HALF
    t_row = q0 + (lax.broadcasted_iota(jnp.int32, (R, KT), 0) & (Q - 1))
    k_col = lax.broadcasted_iota(jnp.int32, (R, KT), 1)

    def reset():
        m_scr[...] = jnp.full_like(m_scr, NEG)
        l_scr[...] = jnp.zeros_like(l_scr)
        acc_scr[...] = jnp.zeros_like(acc_scr)

    def update(s, v):
        m_prev = m_scr[...]
        m_new = jnp.maximum(m_prev, jnp.max(s, axis=1, keepdims=True))
        p = jnp.exp(s - m_new)
        alpha = jnp.exp(m_prev - m_new)
        l_scr[...] = alpha * l_scr[...] + jnp.sum(p, axis=1, keepdims=True)
        acc_scr[...] = alpha * acc_scr[...] + _dot(p.astype(BF16), v)
        m_scr[...] = m_new

    def result():
        return acc_scr[...] / l_scr[...]

    for g in range(NSA_GROUPS):
        qa = qaug_ref[0, g * NSA_HPG:(g + 1) * NSA_HPG].reshape(R, LANES)

        reset()

        def slc_body(kj, carry):
            off = pl.multiple_of(kj * KT, KT)
            update(_dot_nt(qa, kaug_ref[0, g, pl.ds(off, KT), :]), vs_ref[0, pl.ds(off, KT), :])
            return carry

        lax.fori_loop(0, qi, slc_body, 0)
        off = pl.multiple_of(q0, KT)
        s = _dot_nt(qa, kaug_ref[0, g, pl.ds(off, KT), :])
        update(jnp.where(k_col <= (t_row - q0), s, NEG), vs_ref[0, pl.ds(off, KT), :])
        oslc_scr[...] = result()

        qf = qa.astype(F32)
        if g == 0:
            qw = jnp.where(lowr, qf, 0.0).astype(BF16)
        else:
            qw = jnp.where(lowr, 0.0, pltpu.roll(qf, HALF, 1)).astype(BF16)
        reset()

        def win_body(kj, carry):
            off = pl.multiple_of(kj * KT, KT)
            kpos = kj * KT + k_col
            keep = (kpos <= t_row) & (kpos > t_row - WINDOW)
            s = _dot_nt(qw, kw_ref[0, pl.ds(off, KT), :])
            update(jnp.where(keep, s, NEG), vw_ref[0, pl.ds(off, KT), :])
            return carry

        lax.fori_loop(jnp.maximum(qi - WINDOW // KT, 0), qi + 1, win_body, 0)
        owin = result()
        oslc = oslc_scr[...]

        for pp in range(NSA_HPG // 2):
            pb = g * (NSA_HPG // 2) + pp
            he = 2 * pb
            re_, ro_ = (2 * pp) * Q, (2 * pp + 1) * Q
            pair = []
            for o in (oslc, owin):
                oe, oo = o[re_:re_ + Q], o[ro_:ro_ + Q]
                if g == 0:
                    pair.append(jnp.where(lowq, oe, pltpu.roll(oo, HALF, 1)))
                else:
                    pair.append(jnp.where(lowq, pltpu.roll(oe, HALF, 1), oo))
            g_slc = jnp.where(lowq, gates[:, 3 * he + 1:3 * he + 2], gates[:, 3 * he + 4:3 * he + 5])
            g_win = jnp.where(lowq, gates[:, 3 * he + 2:3 * he + 3], gates[:, 3 * he + 5:3 * he + 6])
            out = ocmp_ref[0, :, pb * LANES:(pb + 1) * LANES] + g_slc * pair[0] + g_win * pair[1]
            o_ref[0, :, pb * LANES:(pb + 1) * LANES] = out.astype(BF16)


def _flash(qaug, kaug, vs, kw, vw, gates, ocmp):
    B, _, S, _ = qaug.shape
    Q = min(ATT_Q, S)
    assert WINDOW % Q == 0 and Q & (Q - 1) == 0
    R = NSA_HPG * Q
    per_b = lambda shape: pl.BlockSpec((1,) + shape, lambda b, i: (b,) + (0,) * len(shape))
    return pl.pallas_call(
        _flash_kernel,
        grid=(B, S // Q),
        in_specs=[pl.BlockSpec((1, NSA_HEADS, Q, LANES), lambda b, i: (b, 0, i, 0)),
                  per_b((NSA_GROUPS, S, LANES)), per_b((S, LANES)), per_b((S, LANES)), per_b((S, LANES)),
                  pl.BlockSpec((1, Q, LANES), lambda b, i: (b, i, 0)),
                  pl.BlockSpec((1, Q, NSA_QW), lambda b, i: (b, i, 0))],
        out_specs=pl.BlockSpec((1, Q, NSA_QW), lambda b, i: (b, i, 0)),
        out_shape=jax.ShapeDtypeStruct((B, S, NSA_QW), BF16),
        scratch_shapes=[pltpu.VMEM((R, 1), F32), pltpu.VMEM((R, 1), F32),
                        pltpu.VMEM((R, LANES), F32), pltpu.VMEM((R, LANES), F32)],
        compiler_params=_params(("arbitrary", "arbitrary")),
        name="flash",
    )(qaug, kaug, vs, kw, vw, gates, ocmp)


def _retention_kernel(rq_ref, rk_ref, rv_ref, dmat_ref, qdec_ref, kdec_ref, cdec_ref, nw_ref, o_ref, st_scr):
    C = rq_ref.shape[1]

    @pl.when(pl.program_id(1) == 0)
    def _():
        st_scr[...] = jnp.zeros_like(st_scr)

    low = lax.broadcasted_iota(jnp.int32, (C, LANES), 1) < HALF
    same_head = ((lax.broadcasted_iota(jnp.int32, (LANES, LANES), 0) < HALF) ==
                 (lax.broadcasted_iota(jnp.int32, (LANES, LANES), 1) < HALF))
    for p in range(RET_HEADS // 2):
        sl = slice(p * LANES, (p + 1) * LANES)
        q2, k2, v2 = rq_ref[0, :, sl], rk_ref[0, :, sl], rv_ref[0, :, sl]
        q2f = q2.astype(F32)
        qe = jnp.where(low, q2f, 0.0).astype(BF16)
        qo = jnp.where(low, 0.0, q2f).astype(BF16)
        ie = (_dot_nt(qe, k2) * dmat_ref[2 * p]).astype(BF16)
        io = (_dot_nt(qo, k2) * dmat_ref[2 * p + 1]).astype(BF16)
        intra = jnp.where(low, _dot(ie, v2), _dot(io, v2))
        st = st_scr[p]
        o = intra + _dot(q2, st.astype(BF16)) * qdec_ref[:, sl]
        kd = (k2.astype(F32) * kdec_ref[:, sl]).astype(BF16)
        st_scr[p] = st * cdec_ref[p:p + 1, :] + jnp.where(same_head, _dot_tn(kd, v2), 0.0)
        s_lo = jnp.sum(jnp.where(low, o, 0.0), axis=1, keepdims=True)
        s_all = jnp.sum(o, axis=1, keepdims=True)
        d = o - jnp.where(low, s_lo, s_all - s_lo) * (1.0 / RET_DV)
        d2 = d * d
        v_lo = jnp.sum(jnp.where(low, d2, 0.0), axis=1, keepdims=True)
        v_all = jnp.sum(d2, axis=1, keepdims=True)
        var = jnp.where(low, v_lo, v_all - v_lo) * (1.0 / RET_DV)
        o_ref[0, :, sl] = d * lax.rsqrt(var + EPS) * nw_ref[:, sl]


def _retention(rq, rk, rv, ret_norm_w):
    B, S, _ = rq.shape
    C = min(RET_CHUNK, S)
    H = RET_HEADS
    log_g = jnp.log(1.0 - 2.0 ** (-5.0 - jnp.arange(H, dtype=F32)))
    i = jnp.arange(C, dtype=F32)
    diff = i[:, None] - i[None, :]
    dmat = jnp.where(diff >= 0, jnp.exp(log_g[:, None, None] * jnp.maximum(diff, 0.0)), 0.0)
    per_lane = lambda a: jnp.repeat(a, RET_DK, axis=-1)
    qdec = per_lane(jnp.exp(log_g[None, :] * (i[:, None] + 1.0)))
    kdec = per_lane(jnp.exp(log_g[None, :] * (C - 1.0 - i[:, None])))
    cdec = per_lane(jnp.exp(log_g * C)[None, :]).reshape(H // 2, LANES)
    tok = lambda w: pl.BlockSpec((1, C, w), lambda b, i: (b, i, 0))
    return pl.pallas_call(
        _retention_kernel,
        grid=(B, S // C),
        in_specs=[tok(RET_QW), tok(RET_QW), tok(RET_VW),
                  _full_spec((H, C, C)), _full_spec((C, RET_QW)), _full_spec((C, RET_QW)),
                  _full_spec((H // 2, LANES)), _full_spec((1, RET_VW))],
        out_specs=tok(RET_VW),
        out_shape=jax.ShapeDtypeStruct((B, S, RET_VW), F32),
        scratch_shapes=[pltpu.VMEM((H // 2, LANES, LANES), F32)],
        compiler_params=_params(("arbitrary", "arbitrary")),
        name="retention",
    )(rq, rk, rv, dmat, qdec, kdec, cdec, ret_norm_w.reshape(1, RET_VW))


def _mix_kernel(h_ref, mod_ref, nw_ref, onsa_ref, oret_ref, wrg_ref, wga_ref, wgb_ref, wn_ref, wr_ref, wo_ref, o_ref):
    x = h_ref[0]
    sh = mod_ref[0, 3:4, :]
    sc = mod_ref[0, 4:5, :]
    gt = mod_ref[0, 5:6, :]
    u = _norm_mod(x, nw_ref[...], sc, sh).astype(BF16)
    rg = _dot(u, wrg_ref[...])
    oret = (oret_ref[0] * _silu(rg)).astype(BF16)
    y_nsa = _dot(onsa_ref[0], wn_ref[...])
    y_ret = _dot(oret, wr_ref[...])
    ga = jax.nn.sigmoid(_dot(u, wga_ref[...]))
    gb = jax.nn.sigmoid(_dot(u, wgb_ref[...]))
    mixed = (ga * y_nsa + gb * y_ret).astype(BF16)
    o_ref[0] = x + gt * _dot(mixed, wo_ref[...])


def _mix(h, mod, norm_w, o_nsa, o_ret, w_in, w_nsa_up, w_ret_up, w_out, *, tm):
    B, S, D = h.shape
    o = np.cumsum([0, NSA_QW] + [NSA_KVW] * 6 + [3 * NSA_HEADS, RET_QW, RET_QW, RET_VW, RET_VW, D, D])
    wrg = w_in[:, o[11]:o[12]].astype(BF16)
    wga = w_in[:, o[12]:o[13]].astype(BF16)
    wgb = w_in[:, o[13]:o[14]].astype(BF16)
    wn, wr, wo = w_nsa_up.astype(BF16), w_ret_up.astype(BF16), w_out.astype(BF16)
    tok = lambda w: pl.BlockSpec((1, tm, w), lambda b, i: (b, i, 0))
    return pl.pallas_call(
        _mix_kernel,
        grid=(B, S // tm),
        in_specs=[tok(D), pl.BlockSpec((1, 9, D), lambda b, i: (b, 0, 0)), _full_spec((1, D)),
                  tok(NSA_QW), tok(RET_VW),
                  _full_spec(wrg.shape), _full_spec(wga.shape), _full_spec(wgb.shape),
                  _full_spec(wn.shape), _full_spec(wr.shape), _full_spec(wo.shape)],
        out_specs=tok(D),
        out_shape=jax.ShapeDtypeStruct((B, S, D), F32),
        compiler_params=_params(("arbitrary", "arbitrary")),
        name="mix",
    )(h, mod, norm_w.reshape(1, D), o_nsa, o_ret, wrg, wga, wgb, wn, wr, wo)


def kernel(x, c, ada_w, ada_b, norm1_w, ffn1_w_in, ffn1_w_out, norm2_w, w_in, cmp_k_pe, cmp_k_w1, cmp_k_w2,
           cmp_v_pe, cmp_v_w1, cmp_v_w2, ret_norm_w, w_nsa_up, w_ret_up, w_out, norm3_w, ffn2_w_in, ffn2_w_out,
           final_norm_w):
    B, S, D = x.shape
    depth = ada_w.shape[0]
    assert depth >= 1
    tm = min(512, S)
    h = x
    for l in range(depth):
        last = l == depth - 1
        mod = _modulation(c, ada_w[l], ada_b[l]).reshape(B, 9, D)
        h = _ffn(h, mod, norm1_w[l], final_norm_w, ffn1_w_in[l], ffn1_w_out[l], mod_base=0, final_norm=False, tm=tm)
        q, cv, kaug, vs, kw, vw, gates, rq, rk, rv = _projection(h, mod, norm2_w[l], w_in[l], tm=tm)
        kvc = _compress(cv, (cmp_k_pe[l], cmp_v_pe[l]), (cmp_k_w1[l], cmp_v_w1[l]), (cmp_k_w2[l], cmp_v_w2[l]))
        qaug, ocmp = _select(q, kvc, gates)
        o_nsa = _flash(qaug, kaug, vs, kw, vw, gates, ocmp)
        o_ret = _retention(rq, rk, rv, ret_norm_w[l])
        h = _mix(h, mod, norm2_w[l], o_nsa, o_ret, w_in[l], w_nsa_up[l], w_ret_up[l], w_out[l], tm=tm)
        h = _ffn(h, mod, norm3_w[l], final_norm_w, ffn2_w_in[l], ffn2_w_out[l], mod_base=6, final_norm=last, tm=tm)
    return h
```

```python
import functools

import numpy as np
import jax
import jax.numpy as jnp
from jax import lax
from jax.experimental import pallas as pl
from jax.experimental.pallas import tpu as pltpu

F32 = jnp.float32
BF16 = jnp.bfloat16

NSA_HEADS = 8
NSA_GROUPS = 2
NSA_HPG = NSA_HEADS // NSA_GROUPS
NSA_DH = 64
CMP_STRIDE = 16
CMP_LEN = 2 * CMP_STRIDE
CMP_HIDDEN = 128
SEL_BLOCK = 64
SEL_TOPK = 16
WINDOW = 512
FORCE_BONUS = 1e4
RET_HEADS = 8
RET_DK = 64
RET_DV = 64
ROPE_BASE = 10000.0
D_FF = 2816
EPS = 1e-6
NEG = -1e30

NSA_QW = NSA_HEADS * NSA_DH
NSA_KVW = NSA_GROUPS * NSA_DH
RET_QW = RET_HEADS * RET_DK
RET_VW = RET_HEADS * RET_DV

LANES = 128
HALF = LANES // 2
VMEM_LIMIT = 56 * 1024 * 1024
FFN_CHUNK = 256
RET_CHUNK = 256
ATT_Q = 256


def _dot(a, b):
    return jnp.dot(a, b, preferred_element_type=F32)


def _dot_nt(a, b):
    return lax.dot_general(a, b, (((1,), (1,)), ((), ())), preferred_element_type=F32)


def _dot_tn(a, b):
    return lax.dot_general(a, b, (((0,), (0,)), ((), ())), preferred_element_type=F32)


def _silu(a):
    return a * jax.nn.sigmoid(a)


def _norm_mod(x, nw, sc, sh):
    ms = jnp.mean(x * x, axis=-1, keepdims=True)
    y = x * lax.rsqrt(ms + EPS) * nw
    return y * (1.0 + sc) + sh


def _full_spec(shape):
    zeros = (0,) * len(shape)
    return pl.BlockSpec(shape, lambda *_: zeros)


def _params(sem):
    return pltpu.CompilerParams(dimension_semantics=sem, vmem_limit_bytes=VMEM_LIMIT)


def _mod_kernel(c_ref, w_ref, b_ref, o_ref):
    c = c_ref[...]
    o_ref[...] = _dot(_silu(c).astype(BF16), w_ref[...].astype(BF16)) + b_ref[...]


def _modulation(c, ada_w, ada_b):
    B, D = c.shape
    N = ada_w.shape[1]
    tn = N // 8
    return pl.pallas_call(
        _mod_kernel,
        grid=(N // tn,),
        in_specs=[pl.BlockSpec((B, D), lambda j: (0, 0)),
                  pl.BlockSpec((D, tn), lambda j: (0, j)),
                  pl.BlockSpec((1, tn), lambda j: (0, j))],
        out_specs=pl.BlockSpec((B, tn), lambda j: (0, j)),
        out_shape=jax.ShapeDtypeStruct((B, N), F32),
        compiler_params=_params(("arbitrary",)),
        name="modulation",
    )(c, ada_w, ada_b.reshape(1, N))


def _ffn_kernel(x_ref, mod_ref, nw_ref, fw_ref, wa_ref, wb_ref, wo_ref, o_ref, u_scr, acc_scr,
                *, mod_base, final_norm):
    x = x_ref[0]
    sh = mod_ref[0, mod_base:mod_base + 1, :]
    sc = mod_ref[0, mod_base + 1:mod_base + 2, :]
    gt = mod_ref[0, mod_base + 2:mod_base + 3, :]
    u_scr[...] = _norm_mod(x, nw_ref[...], sc, sh).astype(BF16)
    acc_scr[...] = jnp.zeros_like(acc_scr)

    def body(ci, carry):
        u = u_scr[...]
        a = _dot(u, wa_ref[ci])
        b = _dot(u, wb_ref[ci])
        acc_scr[...] += _dot((_silu(a) * b).astype(BF16), wo_ref[ci])
        return carry

    lax.fori_loop(0, wa_ref.shape[0], body, 0)
    y = x + 0.5 * gt * acc_scr[...]
    if final_norm:
        ms = jnp.mean(y * y, axis=-1, keepdims=True)
        y = y * lax.rsqrt(ms + EPS) * fw_ref[...]
    o_ref[0] = y


def _ffn(x, mod, norm_w, final_w, w_in, w_out, *, mod_base, final_norm, tm):
    B, S, D = x.shape
    dff = w_out.shape[0]
    nch = dff // FFN_CHUNK
    wa = w_in[:, :dff].astype(BF16).reshape(D, nch, FFN_CHUNK).transpose(1, 0, 2)
    wb = w_in[:, dff:].astype(BF16).reshape(D, nch, FFN_CHUNK).transpose(1, 0, 2)
    wo = w_out.astype(BF16).reshape(nch, FFN_CHUNK, D)
    kern = functools.partial(_ffn_kernel, mod_base=mod_base, final_norm=final_norm)
    return pl.pallas_call(
        kern,
        grid=(B, S // tm),
        in_specs=[pl.BlockSpec((1, tm, D), lambda b, i: (b, i, 0)),
                  pl.BlockSpec((1, 9, D), lambda b, i: (b, 0, 0)),
                  _full_spec((1, D)), _full_spec((1, D)),
                  _full_spec(wa.shape), _full_spec(wb.shape), _full_spec(wo.shape)],
        out_specs=pl.BlockSpec((1, tm, D), lambda b, i: (b, i, 0)),
        out_shape=jax.ShapeDtypeStruct((B, S, D), F32),
        scratch_shapes=[pltpu.VMEM((tm, D), BF16), pltpu.VMEM((tm, D), F32)],
        compiler_params=_params(("arbitrary", "arbitrary")),
        name="ffn_final" if final_norm else "ffn",
    )(x, mod, norm_w.reshape(1, D), final_w.reshape(1, D), wa, wb, wo)


def _proj_kernel(h_ref, mod_ref, nw_ref, wq_ref, wkv_ref, wg_ref, wr_ref, cos_ref, sin_ref,
                 q_ref, cv_ref, kaug_ref, vst_ref, kw_ref, vwt_ref, g_ref, rq_ref, rk_ref, rv_ref,
                 u_scr):
    tm = h_ref.shape[1]
    x = h_ref[0]
    sh = mod_ref[0, 3:4, :]
    sc = mod_ref[0, 4:5, :]
    u_scr[...] = _norm_mod(x, nw_ref[...], sc, sh).astype(BF16)
    u = u_scr[...]

    q_ref[0] = _dot(u, wq_ref[...]).astype(BF16)

    kv = _dot(u, wkv_ref[...])
    cv_ref[0, 0] = kv[:, 0:LANES].astype(BF16)
    cv_ref[1, 0] = kv[:, LANES:2 * LANES].astype(BF16)
    ks = kv[:, 2 * LANES:3 * LANES]
    lane = lax.broadcasted_iota(jnp.int32, (tm, LANES), 1)
    row = lax.broadcasted_iota(jnp.int32, (tm, LANES), 0)
    low = lane < HALF
    blk = (pl.program_id(1) * tm + row) // SEL_BLOCK
    onehot = jnp.where(lane - HALF == blk, 1.0, 0.0)
    kaug_ref[0, 0] = jnp.where(low, ks, onehot).astype(BF16)
    kaug_ref[0, 1] = jnp.where(low, pltpu.roll(ks, HALF, 1), onehot).astype(BF16)
    kw_ref[0] = kv[:, 4 * LANES:5 * LANES].astype(BF16)
    kt = vst_ref.shape[4]
    ones = jnp.ones((HALF, tm), F32)
    for ref, col in ((vst_ref, 3), (vwt_ref, 5)):
        vt = kv[:, col * LANES:(col + 1) * LANES].T
        for g in range(NSA_GROUPS):
            aug = jnp.concatenate([vt[g * HALF:(g + 1) * HALF], ones], axis=0).astype(BF16)
            for c in range(tm // kt):
                ref[0, g, c] = aug[:, c * kt:(c + 1) * kt]

    g_ref[0] = jax.nn.sigmoid(_dot(u, wg_ref[...]))

    r = _dot(u, wr_ref[...])
    cos = cos_ref[...]
    sin = sin_ref[...]
    first = (lane & (RET_DK // 2)) == 0
    npair = RET_QW // LANES
    for j in range(2 * npair):
        xb = r[:, j * LANES:(j + 1) * LANES]
        partner = jnp.where(first, pltpu.roll(xb, LANES - RET_DK // 2, 1), pltpu.roll(xb, RET_DK // 2, 1))
        y = (xb * cos + partner * sin).astype(BF16)
        if j < npair:
            rq_ref[0, :, j * LANES:(j + 1) * LANES] = y
        else:
            rk_ref[0, :, (j - npair) * LANES:(j - npair + 1) * LANES] = y
    rv_ref[0] = r[:, 2 * RET_QW:2 * RET_QW + RET_VW].astype(BF16)


def _projection(h, mod, norm_w, w_in, *, tm):
    B, S, D = h.shape
    o = np.cumsum([0, NSA_QW] + [NSA_KVW] * 6 + [3 * NSA_HEADS, RET_QW, RET_QW, RET_VW])
    scale_q = NSA_DH ** -0.5
    scale_k = RET_DK ** -0.5
    wq = (w_in[:, o[0]:o[1]] * scale_q).astype(BF16)
    wkv = w_in[:, o[1]:o[7]].astype(BF16)
    wg = jnp.pad(w_in[:, o[7]:o[8]], ((0, 0), (0, LANES - 3 * NSA_HEADS))).astype(BF16)
    wr = jnp.concatenate([w_in[:, o[8]:o[9]], w_in[:, o[9]:o[10]] * scale_k, w_in[:, o[10]:o[11]]],
                         axis=1).astype(BF16)
    half = RET_DK // 2
    lane = jnp.arange(LANES)
    inv = ROPE_BASE ** (-jnp.arange(half, dtype=F32) / half)
    ang = jnp.arange(S).astype(F32)[:, None] * inv[lane % half][None, :]
    cos_t = jnp.cos(ang)
    sin_t = jnp.where((lane % RET_DK) < half, -jnp.sin(ang), jnp.sin(ang))
    tok = lambda w: pl.BlockSpec((1, tm, w), lambda b, i: (b, i, 0))
    kt = min(ATT_Q, S)
    vt_spec = pl.BlockSpec((1, NSA_GROUPS, tm // kt, LANES, kt), lambda b, i: (b, 0, i, 0, 0))
    vt_shape = jax.ShapeDtypeStruct((B, NSA_GROUPS, S // kt, LANES, kt), BF16)
    outs = pl.pallas_call(
        _proj_kernel,
        grid=(B, S // tm),
        in_specs=[tok(D), pl.BlockSpec((1, 9, D), lambda b, i: (b, 0, 0)), _full_spec((1, D)),
                  _full_spec(wq.shape), _full_spec(wkv.shape), _full_spec(wg.shape), _full_spec(wr.shape),
                  pl.BlockSpec((tm, LANES), lambda b, i: (i, 0)), pl.BlockSpec((tm, LANES), lambda b, i: (i, 0))],
        out_specs=[tok(NSA_QW),
                   pl.BlockSpec((2, 1, tm, LANES), lambda b, i: (0, b, i, 0)),
                   pl.BlockSpec((1, NSA_GROUPS, tm, LANES), lambda b, i: (b, 0, i, 0)),
                   vt_spec, tok(LANES), vt_spec, tok(LANES),
                   tok(RET_QW), tok(RET_QW), tok(RET_VW)],
        out_shape=[jax.ShapeDtypeStruct((B, S, NSA_QW), BF16),
                   jax.ShapeDtypeStruct((2, B, S, LANES), BF16),
                   jax.ShapeDtypeStruct((B, NSA_GROUPS, S, LANES), BF16),
                   vt_shape,
                   jax.ShapeDtypeStruct((B, S, LANES), BF16),
                   vt_shape,
                   jax.ShapeDtypeStruct((B, S, LANES), F32),
                   jax.ShapeDtypeStruct((B, S, RET_QW), BF16),
                   jax.ShapeDtypeStruct((B, S, RET_QW), BF16),
                   jax.ShapeDtypeStruct((B, S, RET_VW), BF16)],
        scratch_shapes=[pltpu.VMEM((tm, D), BF16)],
        compiler_params=_params(("arbitrary", "arbitrary")),
        name="projection",
    )(h, mod, norm_w.reshape(1, D), wq, wkv, wg, wr, cos_t, sin_t)
    return outs


def _compress_kernel(ch_ref, w1c_ref, pe_ref, w1_ref, w2_ref, o_ref):
    ch = ch_ref[0, 0]
    nc = ch.shape[0]
    ab = _dot(ch, w1c_ref[0])
    pt = _dot(pe_ref[0], w1_ref[0])[0:1, :]
    pt2 = jnp.concatenate([pt, pt], axis=1)
    nh = NSA_GROUPS * CMP_HIDDEN
    hid = ab[:, :nh] + pltpu.roll(ab[:, nh:], nc - 1, 0) + pt2
    out = _dot(_silu(hid).astype(BF16), w2_ref[0])
    row = lax.broadcasted_iota(jnp.int32, out.shape, 0)
    o_ref[0, 0] = jnp.where(row < nc - 1, out, 0.0).astype(BF16)


def _compress(cv, pe, w1, w2):
    _, B, S, _ = cv.shape
    nc = S // CMP_STRIDE
    width = CMP_STRIDE * LANES
    chunks = cv.reshape(2, B, nc, width)
    eye = jnp.eye(NSA_GROUPS, dtype=F32)
    nh = NSA_GROUPS * CMP_HIDDEN

    def expand(w):
        w = w.reshape(CMP_STRIDE, NSA_DH, CMP_HIDDEN)
        return jnp.einsum("ldj,gh->lgdhj", w, eye).reshape(width, nh)

    half = CMP_STRIDE * NSA_DH
    w1c = jnp.stack([jnp.concatenate([expand(w[:half]), expand(w[half:])], axis=1) for w in w1]).astype(BF16)
    w2bd = jnp.stack([jnp.einsum("jd,gh->gjhd", w, eye).reshape(nh, NSA_KVW) for w in w2]).astype(BF16)
    pe8 = jnp.stack([jnp.broadcast_to(p.reshape(1, CMP_LEN * NSA_DH), (8, CMP_LEN * NSA_DH)) for p in pe]).astype(BF16)
    w1s = jnp.stack(w1).astype(BF16)
    sel = lambda shape: pl.BlockSpec((1,) + shape, lambda t, b: (t,) + (0,) * len(shape))
    return pl.pallas_call(
        _compress_kernel,
        grid=(2, B),
        in_specs=[pl.BlockSpec((1, 1, nc, width), lambda t, b: (t, b, 0, 0)),
                  sel(w1c.shape[1:]), sel(pe8.shape[1:]), sel(w1s.shape[1:]), sel(w2bd.shape[1:])],
        out_specs=pl.BlockSpec((1, 1, nc, LANES), lambda t, b: (t, b, 0, 0)),
        out_shape=jax.ShapeDtypeStruct((2, B, nc, LANES), BF16),
        compiler_params=_params(("arbitrary", "arbitrary")),
        name="compress",
    )(chunks, w1c, pe8, w1s, w2bd)


def _select_kernel(q_ref, kvc_ref, g_ref, ovl_ref, qaug_ref, ocmp_ref, *, n_sel):
    Q = q_ref.shape[1]
    NC = kvc_ref.shape[2]
    q0 = pl.program_id(1) * Q
    kcf = kvc_ref[0, 0].astype(F32)
    vc = kvc_ref[1, 0]
    kcr = pltpu.roll(kcf, HALF, 1)
    lowk = lax.broadcasted_iota(jnp.int32, (NC, LANES), 1) < HALF
    k_even = [jnp.where(lowk, kcf, 0.0).astype(BF16), jnp.where(lowk, kcr, 0.0).astype(BF16)]
    k_odd = [jnp.where(lowk, 0.0, kcr).astype(BF16), jnp.where(lowk, 0.0, kcf).astype(BF16)]
    gates = g_ref[0]
    lowq = lax.broadcasted_iota(jnp.int32, (Q, LANES), 1) < HALF
    n_idx = lax.broadcasted_iota(jnp.int32, (Q, NC), 1)
    t_idx = q0 + lax.broadcasted_iota(jnp.int32, (Q, NC), 0)
    allowed = n_idx * CMP_STRIDE + (CMP_LEN - 1) <= t_idx
    ovl = ovl_ref[...]
    nb = LANES - HALF
    j_blk = lax.broadcasted_iota(jnp.int32, (nb, Q), 0)
    t_blk = q0 + lax.broadcasted_iota(jnp.int32, (nb, Q), 1)
    cur = t_blk // SEL_BLOCK
    forced = (j_blk == 0) | (j_blk == cur) | (j_blk == cur - 1)
    valid = j_blk * SEL_BLOCK <= t_blk

    for g in range(NSA_GROUPS):
        psum = jnp.zeros((Q, NC), F32)
        for pp in range(NSA_HPG // 2):
            pb = g * (NSA_HPG // 2) + pp
            qpair = q_ref[0, :, pb * LANES:(pb + 1) * LANES]
            outs = []
            for kk in (k_even[g], k_odd[g]):
                s = jnp.where(allowed, _dot_nt(qpair, kk), NEG)
                mx = jnp.max(s, axis=1, keepdims=True)
                ex = jnp.where(allowed, jnp.exp(s - mx), 0.0)
                den = jnp.sum(ex, axis=1, keepdims=True)
                p = ex / jnp.where(den > 0.0, den, 1.0)
                psum = psum + p
                outs.append(_dot(p.astype(BF16), vc))
            if g == 0:
                a, b = outs[0], pltpu.roll(outs[1], HALF, 1)
            else:
                a, b = pltpu.roll(outs[0], HALF, 1), outs[1]
            he = 2 * pb
            gate = jnp.where(lowq, gates[:, 3 * he:3 * he + 1], gates[:, 3 * he + 3:3 * he + 4])
            ocmp_ref[0, :, pb * LANES:(pb + 1) * LANES] = jnp.where(lowq, a, b) * gate

        hi = psum.astype(BF16)
        lo = (psum - hi.astype(F32)).astype(BF16)
        imp = (_dot_nt(ovl, hi) + _dot_nt(ovl, lo))[HALF:, :]
        imp = jnp.where(forced, imp + FORCE_BONUS, imp)
        imp = jnp.where(valid, imp, -FORCE_BONUS)
        rank = jnp.zeros((nb, Q), jnp.int32)
        for jp in range(nb):
            r = imp[jp:jp + 1, :]
            rank = rank + jnp.where(j_blk > jp, jnp.where(r >= imp, 1, 0), jnp.where(r > imp, 1, 0))
        sel = (rank < n_sel) & valid
        bias = jnp.where(sel, 0.0, NEG)
        for pp in range(NSA_HPG // 2):
            pb = g * (NSA_HPG // 2) + pp
            qt = q_ref[0, :, pb * LANES:(pb + 1) * LANES].astype(F32).T
            for e in range(2):
                col = (2 * pp + e) * Q
                qaug_ref[0, g, 0, :, col:col + Q] = jnp.concatenate(
                    [qt[e * HALF:(e + 1) * HALF], bias], axis=0).astype(BF16)


def _select(q, kvc, gates):
    B, S, _ = q.shape
    Q = min(ATT_Q, S)
    NC = kvc.shape[2]
    nc, ns = NC - 1, S // SEL_BLOCK
    assert ns <= LANES - HALF
    c_start = np.arange(nc) * CMP_STRIDE
    s_start = np.arange(ns) * SEL_BLOCK
    overlap = ((c_start[:, None] < s_start[None, :] + SEL_BLOCK) &
               (c_start[:, None] + CMP_LEN > s_start[None, :])).astype(np.float32)
    ovl = np.zeros((LANES, NC), np.float32)
    ovl[HALF:HALF + ns, :nc] = overlap.T
    kern = functools.partial(_select_kernel, n_sel=min(SEL_TOPK, ns))
    return pl.pallas_call(
        kern,
        grid=(B, S // Q),
        in_specs=[pl.BlockSpec((1, Q, NSA_QW), lambda b, i: (b, i, 0)),
                  pl.BlockSpec((2, 1, NC, LANES), lambda b, i: (0, b, 0, 0)),
                  pl.BlockSpec((1, Q, LANES), lambda b, i: (b, i, 0)),
                  _full_spec((LANES, NC))],
        out_specs=[pl.BlockSpec((1, NSA_GROUPS, 1, LANES, NSA_HPG * Q), lambda b, i: (b, 0, i, 0, 0)),
                   pl.BlockSpec((1, Q, NSA_QW), lambda b, i: (b, i, 0))],
        out_shape=[jax.ShapeDtypeStruct((B, NSA_GROUPS, S // Q, LANES, NSA_HPG * Q), BF16),
                   jax.ShapeDtypeStruct((B, S, NSA_QW), F32)],
        compiler_params=_params(("arbitrary", "arbitrary")),
        name="select",
    )(q, kvc, gates, jnp.asarray(ovl, BF16))


def _flash_kernel(qaug_ref, kaug_ref, vst_ref, kw_ref, vwt_ref, g_ref, ocmp_ref, o_ref,
                  m_scr, acc_scr, oslc_scr):
    R = qaug_ref.shape[4]
    Q = R // NSA_HPG
    KT = kaug_ref.shape[2] // vst_ref.shape[2]
    qi = pl.program_id(1)
    q0 = qi * Q
    i_col = lax.broadcasted_iota(jnp.int32, (KT, R), 1) & (Q - 1)
    k_row = lax.broadcasted_iota(jnp.int32, (KT, R), 0)
    top_rows = lax.broadcasted_iota(jnp.int32, (LANES, R), 0) < HALF

    def reset():
        m_scr[...] = jnp.full_like(m_scr, NEG)
        acc_scr[...] = jnp.zeros_like(acc_scr)

    def update(s, vt):
        m_prev = m_scr[...]
        m_new = jnp.maximum(m_prev, jnp.max(s, axis=0, keepdims=True))
        p = jnp.exp(s - m_new)
        alpha = jnp.exp(m_prev - m_new)
        acc_scr[...] = alpha * acc_scr[...] + _dot(vt, p.astype(BF16))
        m_scr[...] = m_new

    def result():
        acc = acc_scr[...]
        return acc[:HALF] / acc[HALF:]

    gt = g_ref[0].T

    for g in range(NSA_GROUPS):
        qa = qaug_ref[0, g, 0]

        reset()

        def slc_body(kj, carry):
            off = pl.multiple_of(kj * KT, KT)
            update(_dot(kaug_ref[0, g, pl.ds(off, KT), :], qa), vst_ref[0, g, kj])
            return carry

        lax.fori_loop(0, qi, slc_body, 0)
        off = pl.multiple_of(q0, KT)
        s = _dot(kaug_ref[0, g, pl.ds(off, KT), :], qa)
        update(jnp.where(k_row <= i_col, s, NEG), vst_ref[0, g, qi])
        oslc_scr[...] = result()

        qf = qa.astype(F32)
        if g == 0:
            qw = jnp.where(top_rows, qf, 0.0).astype(BF16)
        else:
            qw = jnp.concatenate([jnp.zeros((HALF, R), F32), qf[:HALF]], axis=0).astype(BF16)
        reset()

        def win_body(kj, carry):
            off = pl.multiple_of(kj * KT, KT)
            rel = (kj * KT - q0) + k_row - i_col
            s = _dot(kw_ref[0, pl.ds(off, KT), :], qw)
            update(jnp.where((rel <= 0) & (rel > -WINDOW), s, NEG), vwt_ref[0, g, kj])
            return carry

        lax.fori_loop(jnp.maximum(qi - WINDOW // KT, 0), qi + 1, win_body, 0)
        owin = result()
        oslc = oslc_scr[...]

        for pp in range(NSA_HPG // 2):
            pb = g * (NSA_HPG // 2) + pp
            halves = []
            for e in range(2):
                h = 2 * pb + e
                cols = slice((2 * pp + e) * Q, (2 * pp + e + 1) * Q)
                halves.append(gt[3 * h + 1:3 * h + 2] * oslc[:, cols] + gt[3 * h + 2:3 * h + 3] * owin[:, cols])
            pair = jnp.concatenate(halves, axis=0).T
            o_ref[0, :, pb * LANES:(pb + 1) * LANES] = (ocmp_ref[0, :, pb * LANES:(pb + 1) * LANES] + pair).astype(BF16)


def _flash(qaug, kaug, vst, kw, vwt, gates, ocmp):
    B, _, NQ, _, R = qaug.shape
    Q = R // NSA_HPG
    S = NQ * Q
    assert WINDOW % Q == 0 and Q & (Q - 1) == 0 and vst.shape[4] == Q
    per_b = lambda shape: pl.BlockSpec((1,) + shape, lambda b, i: (b,) + (0,) * len(shape))
    return pl.pallas_call(
        _flash_kernel,
        grid=(B, NQ),
        in_specs=[pl.BlockSpec((1, NSA_GROUPS, 1, LANES, R), lambda b, i: (b, 0, i, 0, 0)),
                  per_b((NSA_GROUPS, S, LANES)), per_b(vst.shape[1:]), per_b((S, LANES)), per_b(vwt.shape[1:]),
                  pl.BlockSpec((1, Q, LANES), lambda b, i: (b, i, 0)),
                  pl.BlockSpec((1, Q, NSA_QW), lambda b, i: (b, i, 0))],
        out_specs=pl.BlockSpec((1, Q, NSA_QW), lambda b, i: (b, i, 0)),
        out_shape=jax.ShapeDtypeStruct((B, S, NSA_QW), BF16),
        scratch_shapes=[pltpu.VMEM((1, R), F32), pltpu.VMEM((LANES, R), F32), pltpu.VMEM((HALF, R), F32)],
        compiler_params=_params(("arbitrary", "arbitrary")),
        name="flash",
    )(qaug, kaug, vst, kw, vwt, gates, ocmp)


def _retention_kernel(rq_ref, rk_ref, rv_ref, dmat_ref, qdec_ref, kdec_ref, cdec_ref, nw_ref, o_ref, st_scr):
    C = rq_ref.shape[1]

    @pl.when(pl.program_id(1) == 0)
    def _():
        st_scr[...] = jnp.zeros_like(st_scr)

    low = lax.broadcasted_iota(jnp.int32, (C, LANES), 1) < HALF
    same_head = ((lax.broadcasted_iota(jnp.int32, (LANES, LANES), 0) < HALF) ==
                 (lax.broadcasted_iota(jnp.int32, (LANES, LANES), 1) < HALF))
    for p in range(RET_HEADS // 2):
        sl = slice(p * LANES, (p + 1) * LANES)
        q2, k2, v2 = rq_ref[0, :, sl], rk_ref[0, :, sl], rv_ref[0, :, sl]
        q2f = q2.astype(F32)
        qe = jnp.where(low, q2f, 0.0).astype(BF16)
        qo = jnp.where(low, 0.0, q2f).astype(BF16)
        ie = (_dot_nt(qe, k2) * dmat_ref[2 * p]).astype(BF16)
        io = (_dot_nt(qo, k2) * dmat_ref[2 * p + 1]).astype(BF16)
        intra = jnp.where(low, _dot(ie, v2), _dot(io, v2))
        st = st_scr[p]
        o = intra + _dot(q2, st.astype(BF16)) * qdec_ref[:, sl]
        kd = (k2.astype(F32) * kdec_ref[:, sl]).astype(BF16)
        st_scr[p] = st * cdec_ref[p:p + 1, :] + jnp.where(same_head, _dot_tn(kd, v2), 0.0)
        s_lo = jnp.sum(jnp.where(low, o, 0.0), axis=1, keepdims=True)
        s_all = jnp.sum(o, axis=1, keepdims=True)
        d = o - jnp.where(low, s_lo, s_all - s_lo) * (1.0 / RET_DV)
        d2 = d * d
        v_lo = jnp.sum(jnp.where(low, d2, 0.0), axis=1, keepdims=True)
        v_all = jnp.sum(d2, axis=1, keepdims=True)
        var = jnp.where(low, v_lo, v_all - v_lo) * (1.0 / RET_DV)
        o_ref[0, :, sl] = d * lax.rsqrt(var + EPS) * nw_ref[:, sl]


def _retention(rq, rk, rv, ret_norm_w):
    B, S, _ = rq.shape
    C = min(RET_CHUNK, S)
    H = RET_HEADS
    log_g = jnp.log(1.0 - 2.0 ** (-5.0 - jnp.arange(H, dtype=F32)))
    i = jnp.arange(C, dtype=F32)
    diff = i[:, None] - i[None, :]
    dmat = jnp.where(diff >= 0, jnp.exp(log_g[:, None, None] * jnp.maximum(diff, 0.0)), 0.0)
    per_lane = lambda a: jnp.repeat(a, RET_DK, axis=-1)
    qdec = per_lane(jnp.exp(log_g[None, :] * (i[:, None] + 1.0)))
    kdec = per_lane(jnp.exp(log_g[None, :] * (C - 1.0 - i[:, None])))
    cdec = per_lane(jnp.exp(log_g * C)[None, :]).reshape(H // 2, LANES)
    tok = lambda w: pl.BlockSpec((1, C, w), lambda b, i: (b, i, 0))
    return pl.pallas_call(
        _retention_kernel,
        grid=(B, S // C),
        in_specs=[tok(RET_QW), tok(RET_QW), tok(RET_VW),
                  _full_spec((H, C, C)), _full_spec((C, RET_QW)), _full_spec((C, RET_QW)),
                  _full_spec((H // 2, LANES)), _full_spec((1, RET_VW))],
        out_specs=tok(RET_VW),
        out_shape=jax.ShapeDtypeStruct((B, S, RET_VW), F32),
        scratch_shapes=[pltpu.VMEM((H // 2, LANES, LANES), F32)],
        compiler_params=_params(("arbitrary", "arbitrary")),
        name="retention",
    )(rq, rk, rv, dmat, qdec, kdec, cdec, ret_norm_w.reshape(1, RET_VW))


def _mix_kernel(h_ref, mod_ref, nw_ref, onsa_ref, oret_ref, wrg_ref, wga_ref, wgb_ref, wn_ref, wr_ref, wo_ref, o_ref):
    x = h_ref[0]
    sh = mod_ref[0, 3:4, :]
    sc = mod_ref[0, 4:5, :]
    gt = mod_ref[0, 5:6, :]
    u = _norm_mod(x, nw_ref[...], sc, sh).astype(BF16)
    rg = _dot(u, wrg_ref[...])
    oret = (oret_ref[0] * _silu(rg)).astype(BF16)
    y_nsa = _dot(onsa_ref[0], wn_ref[...])
    y_ret = _dot(oret, wr_ref[...])
    ga = jax.nn.sigmoid(_dot(u, wga_ref[...]))
    gb = jax.nn.sigmoid(_dot(u, wgb_ref[...]))
    mixed = (ga * y_nsa + gb * y_ret).astype(BF16)
    o_ref[0] = x + gt * _dot(mixed, wo_ref[...])


def _mix(h, mod, norm_w, o_nsa, o_ret, w_in, w_nsa_up, w_ret_up, w_out, *, tm):
    B, S, D = h.shape
    o = np.cumsum([0, NSA_QW] + [NSA_KVW] * 6 + [3 * NSA_HEADS, RET_QW, RET_QW, RET_VW, RET_VW, D, D])
    wrg = w_in[:, o[11]:o[12]].astype(BF16)
    wga = w_in[:, o[12]:o[13]].astype(BF16)
    wgb = w_in[:, o[13]:o[14]].astype(BF16)
    wn, wr, wo = w_nsa_up.astype(BF16), w_ret_up.astype(BF16), w_out.astype(BF16)
    tok = lambda w: pl.BlockSpec((1, tm, w), lambda b, i: (b, i, 0))
    return pl.pallas_call(
        _mix_kernel,
        grid=(B, S // tm),
        in_specs=[tok(D), pl.BlockSpec((1, 9, D), lambda b, i: (b, 0, 0)), _full_spec((1, D)),
                  tok(NSA_QW), tok(RET_VW),
                  _full_spec(wrg.shape), _full_spec(wga.shape), _full_spec(wgb.shape),
                  _full_spec(wn.shape), _full_spec(wr.shape), _full_spec(wo.shape)],
        out_specs=tok(D),
        out_shape=jax.ShapeDtypeStruct((B, S, D), F32),
        compiler_params=_params(("arbitrary", "arbitrary")),
        name="mix",
    )(h, mod, norm_w.reshape(1, D), o_nsa, o_ret, wrg, wga, wgb, wn, wr, wo)


def kernel(x, c, ada_w, ada_b, norm1_w, ffn1_w_in, ffn1_w_out, norm2_w, w_in, cmp_k_pe, cmp_k_w1, cmp_k_w2,
           cmp_v_pe, cmp_v_w1, cmp_v_w2, ret_norm_w, w_nsa_up, w_ret_up, w_out, norm3_w, ffn2_w_in, ffn2_w_out,
           final_norm_w):
    B, S, D = x.shape
    depth = ada_w.shape[0]
    assert depth >= 1
    tm = min(512, S)
    h = x
    for l in range(depth):
        last = l == depth - 1
        mod = _modulation(c, ada_w[l], ada_b[l]).reshape(B, 9, D)
        h = _ffn(h, mod, norm1_w[l], final_norm_w, ffn1_w_in[l], ffn1_w_out[l], mod_base=0, final_norm=False, tm=tm)
        q, cv, kaug, vst, kw, vwt, gates, rq, rk, rv = _projection(h, mod, norm2_w[l], w_in[l], tm=tm)
        kvc = _compress(cv, (cmp_k_pe[l], cmp_v_pe[l]), (cmp_k_w1[l], cmp_v_w1[l]), (cmp_k_w2[l], cmp_v_w2[l]))
        qaug, ocmp = _select(q, kvc, gates)
        o_nsa = _flash(qaug, kaug, vst, kw, vwt, gates, ocmp)
        o_ret = _retention(rq, rk, rv, ret_norm_w[l])
        h = _mix(h, mod, norm2_w[l], o_nsa, o_ret, w_in[l], w_nsa_up[l], w_ret_up[l], w_out[l], tm=tm)
        h = _ffn(h, mod, norm3_w[l], final_norm_w, ffn2_w_in[l], ffn2_w_out[l], mod_base=6, final_norm=last, tm=tm)
    return h
```

```python
import functools

import numpy as np
import jax
import jax.numpy as jnp
from jax import lax
from jax.experimental import pallas as pl
from jax.experimental.pallas import tpu as pltpu

F32 = jnp.float32
BF16 = jnp.bfloat16

NSA_HEADS = 8
NSA_GROUPS = 2
NSA_HPG = NSA_HEADS // NSA_GROUPS
NSA_DH = 64
CMP_STRIDE = 16
CMP_LEN = 2 * CMP_STRIDE
CMP_HIDDEN = 128
SEL_BLOCK = 64
SEL_TOPK = 16
WINDOW = 512
FORCE_BONUS = 1e4
RET_HEADS = 8
RET_DK = 64
RET_DV = 64
ROPE_BASE = 10000.0
D_FF = 2816
EPS = 1e-6
NEG = -1e30

NSA_QW = NSA_HEADS * NSA_DH
NSA_KVW = NSA_GROUPS * NSA_DH
RET_QW = RET_HEADS * RET_DK
RET_VW = RET_HEADS * RET_DV

LANES = 128
HALF = LANES // 2
VMEM_LIMIT = 56 * 1024 * 1024
FFN_CHUNK = 256
RET_CHUNK = 256
ATT_Q = 256
VAL_ROWS = NSA_DH + 16
LOG2E = 1.4426950408889634
FLASH_SUBTILES = 2


def _dot(a, b):
    return jnp.dot(a, b, preferred_element_type=F32)


def _dot_nt(a, b):
    return lax.dot_general(a, b, (((1,), (1,)), ((), ())), preferred_element_type=F32)


def _dot_tn(a, b):
    return lax.dot_general(a, b, (((0,), (0,)), ((), ())), preferred_element_type=F32)


def _silu(a):
    return a * jax.nn.sigmoid(a)


def _norm_mod(x, nw, sc, sh):
    ms = jnp.mean(x * x, axis=-1, keepdims=True)
    y = x * lax.rsqrt(ms + EPS) * nw
    return y * (1.0 + sc) + sh


def _full_spec(shape):
    zeros = (0,) * len(shape)
    return pl.BlockSpec(shape, lambda *_: zeros)


def _params(sem):
    return pltpu.CompilerParams(dimension_semantics=sem, vmem_limit_bytes=VMEM_LIMIT)


def _mod_kernel(c_ref, w_ref, b_ref, o_ref):
    c = c_ref[...]
    o_ref[...] = _dot(_silu(c).astype(BF16), w_ref[...].astype(BF16)) + b_ref[...]


def _modulation(c, ada_w, ada_b):
    B, D = c.shape
    N = ada_w.shape[1]
    tn = N // 8
    return pl.pallas_call(
        _mod_kernel,
        grid=(N // tn,),
        in_specs=[pl.BlockSpec((B, D), lambda j: (0, 0)),
                  pl.BlockSpec((D, tn), lambda j: (0, j)),
                  pl.BlockSpec((1, tn), lambda j: (0, j))],
        out_specs=pl.BlockSpec((B, tn), lambda j: (0, j)),
        out_shape=jax.ShapeDtypeStruct((B, N), F32),
        compiler_params=_params(("arbitrary",)),
        name="modulation",
    )(c, ada_w, ada_b.reshape(1, N))


def _ffn_kernel(x_ref, mod_ref, nw_ref, fw_ref, wa_ref, wb_ref, wo_ref, o_ref, u_scr, acc_scr,
                *, mod_base, final_norm):
    x = x_ref[0]
    sh = mod_ref[0, mod_base:mod_base + 1, :]
    sc = mod_ref[0, mod_base + 1:mod_base + 2, :]
    gt = mod_ref[0, mod_base + 2:mod_base + 3, :]
    u_scr[...] = _norm_mod(x, nw_ref[...], sc, sh).astype(BF16)
    acc_scr[...] = jnp.zeros_like(acc_scr)

    def body(ci, carry):
        u = u_scr[...]
        a = _dot(u, wa_ref[ci])
        b = _dot(u, wb_ref[ci])
        acc_scr[...] += _dot((_silu(a) * b).astype(BF16), wo_ref[ci])
        return carry

    lax.fori_loop(0, wa_ref.shape[0], body, 0, unroll=True)
    y = x + 0.5 * gt * acc_scr[...]
    if final_norm:
        ms = jnp.mean(y * y, axis=-1, keepdims=True)
        y = y * lax.rsqrt(ms + EPS) * fw_ref[...]
    o_ref[0] = y


def _ffn(x, mod, norm_w, final_w, w_in, w_out, *, mod_base, final_norm, tm):
    B, S, D = x.shape
    dff = w_out.shape[0]
    nch = dff // FFN_CHUNK
    wa = w_in[:, :dff].astype(BF16).reshape(D, nch, FFN_CHUNK).transpose(1, 0, 2)
    wb = w_in[:, dff:].astype(BF16).reshape(D, nch, FFN_CHUNK).transpose(1, 0, 2)
    wo = w_out.astype(BF16).reshape(nch, FFN_CHUNK, D)
    kern = functools.partial(_ffn_kernel, mod_base=mod_base, final_norm=final_norm)
    return pl.pallas_call(
        kern,
        grid=(B, S // tm),
        in_specs=[pl.BlockSpec((1, tm, D), lambda b, i: (b, i, 0)),
                  pl.BlockSpec((1, 9, D), lambda b, i: (b, 0, 0)),
                  _full_spec((1, D)), _full_spec((1, D)),
                  _full_spec(wa.shape), _full_spec(wb.shape), _full_spec(wo.shape)],
        out_specs=pl.BlockSpec((1, tm, D), lambda b, i: (b, i, 0)),
        out_shape=jax.ShapeDtypeStruct((B, S, D), F32),
        scratch_shapes=[pltpu.VMEM((tm, D), BF16), pltpu.VMEM((tm, D), F32)],
        compiler_params=_params(("arbitrary", "arbitrary")),
        name="ffn_final" if final_norm else "ffn",
    )(x, mod, norm_w.reshape(1, D), final_w.reshape(1, D), wa, wb, wo)


def _proj_kernel(h_ref, mod_ref, nw_ref, wq_ref, wkv_ref, wg_ref, wr_ref, cos_ref, sin_ref,
                 q_ref, cv_ref, kaug_ref, vst_ref, kw_ref, vwt_ref, g_ref, rq_ref, rk_ref, rv_ref,
                 u_scr):
    tm = h_ref.shape[1]
    x = h_ref[0]
    sh = mod_ref[0, 3:4, :]
    sc = mod_ref[0, 4:5, :]
    u_scr[...] = _norm_mod(x, nw_ref[...], sc, sh).astype(BF16)
    u = u_scr[...]

    q_ref[0] = _dot(u, wq_ref[...]).astype(BF16)

    kv = _dot(u, wkv_ref[...])
    cv_ref[0, 0] = kv[:, 0:LANES].astype(BF16)
    cv_ref[1, 0] = kv[:, LANES:2 * LANES].astype(BF16)
    ks = kv[:, 2 * LANES:3 * LANES]
    lane = lax.broadcasted_iota(jnp.int32, (tm, LANES), 1)
    row = lax.broadcasted_iota(jnp.int32, (tm, LANES), 0)
    low = lane < HALF
    blk = (pl.program_id(1) * tm + row) // SEL_BLOCK
    onehot = jnp.where(lane - HALF == blk, 1.0, 0.0)
    kaug_ref[0, 0] = jnp.where(low, ks, onehot).astype(BF16)
    kaug_ref[0, 1] = jnp.where(low, pltpu.roll(ks, HALF, 1), onehot).astype(BF16)
    kw_ref[0] = kv[:, 4 * LANES:5 * LANES].astype(BF16)
    kt = vst_ref.shape[4]
    ones = jnp.ones((VAL_ROWS - NSA_DH, tm), F32)
    for ref, col in ((vst_ref, 3), (vwt_ref, 5)):
        vt = kv[:, col * LANES:(col + 1) * LANES].T
        for g in range(NSA_GROUPS):
            aug = jnp.concatenate([vt[g * HALF:(g + 1) * HALF], ones], axis=0).astype(BF16)
            for c in range(tm // kt):
                ref[0, g, c] = aug[:, c * kt:(c + 1) * kt]

    g_ref[0] = jax.nn.sigmoid(_dot(u, wg_ref[...]))

    r = _dot(u, wr_ref[...])
    cos = cos_ref[...]
    sin = sin_ref[...]
    first = (lane & (RET_DK // 2)) == 0
    npair = RET_QW // LANES
    for j in range(2 * npair):
        xb = r[:, j * LANES:(j + 1) * LANES]
        partner = jnp.where(first, pltpu.roll(xb, LANES - RET_DK // 2, 1), pltpu.roll(xb, RET_DK // 2, 1))
        y = (xb * cos + partner * sin).astype(BF16)
        if j < npair:
            rq_ref[0, :, j * LANES:(j + 1) * LANES] = y
        else:
            rk_ref[0, :, (j - npair) * LANES:(j - npair + 1) * LANES] = y
    rv_ref[0] = r[:, 2 * RET_QW:2 * RET_QW + RET_VW].astype(BF16)


def _projection(h, mod, norm_w, w_in, *, tm):
    B, S, D = h.shape
    o = np.cumsum([0, NSA_QW] + [NSA_KVW] * 6 + [3 * NSA_HEADS, RET_QW, RET_QW, RET_VW])
    scale_q = NSA_DH ** -0.5 * LOG2E
    scale_k = RET_DK ** -0.5
    wq = (w_in[:, o[0]:o[1]] * scale_q).astype(BF16)
    wkv = w_in[:, o[1]:o[7]].astype(BF16)
    wg = jnp.pad(w_in[:, o[7]:o[8]], ((0, 0), (0, LANES - 3 * NSA_HEADS))).astype(BF16)
    wr = jnp.concatenate([w_in[:, o[8]:o[9]], w_in[:, o[9]:o[10]] * scale_k, w_in[:, o[10]:o[11]]],
                         axis=1).astype(BF16)
    half = RET_DK // 2
    lane = jnp.arange(LANES)
    inv = ROPE_BASE ** (-jnp.arange(half, dtype=F32) / half)
    ang = jnp.arange(S).astype(F32)[:, None] * inv[lane % half][None, :]
    cos_t = jnp.cos(ang)
    sin_t = jnp.where((lane % RET_DK) < half, -jnp.sin(ang), jnp.sin(ang))
    tok = lambda w: pl.BlockSpec((1, tm, w), lambda b, i: (b, i, 0))
    kt = min(ATT_Q, S)
    vt_spec = pl.BlockSpec((1, NSA_GROUPS, tm // kt, VAL_ROWS, kt), lambda b, i: (b, 0, i, 0, 0))
    vt_shape = jax.ShapeDtypeStruct((B, NSA_GROUPS, S // kt, VAL_ROWS, kt), BF16)
    outs = pl.pallas_call(
        _proj_kernel,
        grid=(B, S // tm),
        in_specs=[tok(D), pl.BlockSpec((1, 9, D), lambda b, i: (b, 0, 0)), _full_spec((1, D)),
                  _full_spec(wq.shape), _full_spec(wkv.shape), _full_spec(wg.shape), _full_spec(wr.shape),
                  pl.BlockSpec((tm, LANES), lambda b, i: (i, 0)), pl.BlockSpec((tm, LANES), lambda b, i: (i, 0))],
        out_specs=[tok(NSA_QW),
                   pl.BlockSpec((2, 1, tm, LANES), lambda b, i: (0, b, i, 0)),
                   pl.BlockSpec((1, NSA_GROUPS, tm, LANES), lambda b, i: (b, 0, i, 0)),
                   vt_spec, tok(LANES), vt_spec, tok(LANES),
                   tok(RET_QW), tok(RET_QW), tok(RET_VW)],
        out_shape=[jax.ShapeDtypeStruct((B, S, NSA_QW), BF16),
                   jax.ShapeDtypeStruct((2, B, S, LANES), BF16),
                   jax.ShapeDtypeStruct((B, NSA_GROUPS, S, LANES), BF16),
                   vt_shape,
                   jax.ShapeDtypeStruct((B, S, LANES), BF16),
                   vt_shape,
                   jax.ShapeDtypeStruct((B, S, LANES), F32),
                   jax.ShapeDtypeStruct((B, S, RET_QW), BF16),
                   jax.ShapeDtypeStruct((B, S, RET_QW), BF16),
                   jax.ShapeDtypeStruct((B, S, RET_VW), BF16)],
        scratch_shapes=[pltpu.VMEM((tm, D), BF16)],
        compiler_params=_params(("arbitrary", "arbitrary")),
        name="projection",
    )(h, mod, norm_w.reshape(1, D), wq, wkv, wg, wr, cos_t, sin_t)
    return outs


def _compress_kernel(ch_ref, w1c_ref, pe_ref, w1_ref, w2_ref, o_ref, ot_ref):
    ch = ch_ref[0, 0]
    nc = ch.shape[0]
    ab = _dot(ch, w1c_ref[0])
    pt = _dot(pe_ref[0], w1_ref[0])[0:1, :]
    pt2 = jnp.concatenate([pt, pt], axis=1)
    nh = NSA_GROUPS * CMP_HIDDEN
    hid = ab[:, :nh] + pltpu.roll(ab[:, nh:], nc - 1, 0) + pt2
    out = _dot(_silu(hid).astype(BF16), w2_ref[0])
    row = lax.broadcasted_iota(jnp.int32, out.shape, 0)
    out = jnp.where(row < nc - 1, out, 0.0)
    o_ref[0, 0] = out.astype(BF16)
    ot_ref[0, 0] = out.T.astype(BF16)


def _compress(cv, pe, w1, w2):
    _, B, S, _ = cv.shape
    nc = S // CMP_STRIDE
    width = CMP_STRIDE * LANES
    chunks = cv.reshape(2, B, nc, width)
    eye = jnp.eye(NSA_GROUPS, dtype=F32)
    nh = NSA_GROUPS * CMP_HIDDEN

    def expand(w):
        w = w.reshape(CMP_STRIDE, NSA_DH, CMP_HIDDEN)
        return jnp.einsum("ldj,gh->lgdhj", w, eye).reshape(width, nh)

    half = CMP_STRIDE * NSA_DH
    w1c = jnp.stack([jnp.concatenate([expand(w[:half]), expand(w[half:])], axis=1) for w in w1]).astype(BF16)
    w2bd = jnp.stack([jnp.einsum("jd,gh->gjhd", w, eye).reshape(nh, NSA_KVW) for w in w2]).astype(BF16)
    pe8 = jnp.stack([jnp.broadcast_to(p.reshape(1, CMP_LEN * NSA_DH), (8, CMP_LEN * NSA_DH)) for p in pe]).astype(BF16)
    w1s = jnp.stack(w1).astype(BF16)
    sel = lambda shape: pl.BlockSpec((1,) + shape, lambda t, b: (t,) + (0,) * len(shape))
    return pl.pallas_call(
        _compress_kernel,
        grid=(2, B),
        in_specs=[pl.BlockSpec((1, 1, nc, width), lambda t, b: (t, b, 0, 0)),
                  sel(w1c.shape[1:]), sel(pe8.shape[1:]), sel(w1s.shape[1:]), sel(w2bd.shape[1:])],
        out_specs=[pl.BlockSpec((1, 1, nc, LANES), lambda t, b: (t, b, 0, 0)),
                   pl.BlockSpec((1, 1, LANES, nc), lambda t, b: (t, b, 0, 0))],
        out_shape=[jax.ShapeDtypeStruct((2, B, nc, LANES), BF16),
                   jax.ShapeDtypeStruct((2, B, LANES, nc), BF16)],
        compiler_params=_params(("arbitrary", "arbitrary")),
        name="compress",
    )(chunks, w1c, pe8, w1s, w2bd)


def _select_kernel(q_ref, kc_ref, vct_ref, g_ref, ovl_ref, qaug_ref, ocmp_ref, imp_scr, rank_scr, *, n_sel):
    Q = q_ref.shape[1]
    NC = kc_ref.shape[2]
    NB = ovl_ref.shape[0]
    qi = pl.program_id(1)
    q0 = qi * Q
    kcf = kc_ref[0, 0].astype(F32)
    kcr = pltpu.roll(kcf, HALF, 1)
    lowk = lax.broadcasted_iota(jnp.int32, (NC, LANES), 1) < HALF
    k_even = [jnp.where(lowk, kcf, 0.0).astype(BF16), jnp.where(lowk, kcr, 0.0).astype(BF16)]
    k_odd = [jnp.where(lowk, 0.0, kcr).astype(BF16), jnp.where(lowk, 0.0, kcf).astype(BF16)]
    vct = vct_ref[0, 0]
    gt = g_ref[0].T
    n_row = lax.broadcasted_iota(jnp.int32, (NC, Q), 0)
    t_col = q0 + lax.broadcasted_iota(jnp.int32, (NC, Q), 1)
    allowed = n_row * CMP_STRIDE + (CMP_LEN - 1) <= t_col
    any_allowed = q0 + lax.broadcasted_iota(jnp.int32, (1, Q), 1) >= CMP_LEN - 1
    ovl = ovl_ref[...]
    j_blk = lax.broadcasted_iota(jnp.int32, (NB, Q), 0)
    t_blk = q0 + lax.broadcasted_iota(jnp.int32, (NB, Q), 1)
    cur = t_blk // SEL_BLOCK
    forced = (j_blk == 0) | (j_blk == cur) | (j_blk == cur - 1)
    valid = j_blk * SEL_BLOCK <= t_blk
    j_sub = lax.broadcasted_iota(jnp.int32, (8, Q), 0)
    per_tile = Q // SEL_BLOCK

    scores = []
    for pb in range(NSA_HEADS // 2):
        g, pp = divmod(pb, NSA_HPG // 2)
        qt = q_ref[0, :, pb * LANES:(pb + 1) * LANES].astype(F32).T.astype(BF16)
        for e, kk in enumerate((k_even[g], k_odd[g])):
            scores.append(_dot(kk, qt))
            col = (2 * pp + e) * Q
            qaug_ref[0, g, 0, 0:HALF, col:col + Q] = qt[e * HALF:(e + 1) * HALF]

    for g in range(NSA_GROUPS):
        psum = jnp.zeros((NC, Q), F32)
        for hh in range(NSA_HPG):
            h = g * NSA_HPG + hh
            s = jnp.where(allowed, scores[h], NEG)
            mx = jnp.max(s, axis=0, keepdims=True)
            ex = jnp.exp2(s - mx)
            den = jnp.sum(ex, axis=0, keepdims=True)
            p = ex * jnp.where(any_allowed, 1.0 / den, 0.0)
            psum = psum + p
            o = _dot(vct, p.astype(BF16))[g * HALF:(g + 1) * HALF]
            ocmp_ref[0, 0, h * NSA_DH:(h + 1) * NSA_DH, :] = gt[3 * h:3 * h + 1] * o

        hi = psum.astype(BF16)
        lo = (psum - hi.astype(F32)).astype(BF16)
        imp = _dot(ovl, hi) + _dot(ovl, lo)
        imp = jnp.where(forced, imp + FORCE_BONUS, imp)
        imp_scr[g] = jnp.where(valid, imp, -FORCE_BONUS)
    rank_scr[...] = jnp.zeros_like(rank_scr)

    quads = 4
    for t in range(0, NB // per_tile, quads):
        @pl.when(t <= qi)
        def _():
            for g in range(NSA_GROUPS):
                vals = [imp_scr[g, 8 * sb:8 * sb + 8, :] for sb in range(NB // 8)]
                cnts = [rank_scr[g, 8 * sb:8 * sb + 8, :] for sb in range(NB // 8)]
                for jp in range(t * per_tile, (t + quads) * per_tile):
                    r = jnp.broadcast_to(imp_scr[g, jp:jp + 1, :], (8, Q))
                    for sb, v in enumerate(vals):
                        if 8 * sb > jp:
                            ahead = jnp.where(r >= v, 1, 0)
                        elif 8 * sb + 7 <= jp:
                            ahead = jnp.where(r > v, 1, 0)
                        else:
                            ahead = jnp.where(j_sub + 8 * sb > jp, jnp.where(r >= v, 1, 0), jnp.where(r > v, 1, 0))
                        cnts[sb] = cnts[sb] + ahead
                for sb, cnt in enumerate(cnts):
                    rank_scr[g, 8 * sb:8 * sb + 8, :] = cnt

    for g in range(NSA_GROUPS):
        sel = (rank_scr[g] < n_sel) & valid
        bias = jnp.where(sel, 0.0, NEG).astype(BF16)
        for hh in range(NSA_HPG):
            qaug_ref[0, g, 0, HALF:HALF + NB, hh * Q:(hh + 1) * Q] = bias


def _select(q, kc, vct, gates):
    B, S, _ = q.shape
    Q = min(ATT_Q, S)
    NC = kc.shape[2]
    nc, ns = NC - 1, S // SEL_BLOCK
    NB = LANES - HALF
    assert ns <= NB and Q % SEL_BLOCK == 0
    c_start = np.arange(nc) * CMP_STRIDE
    s_start = np.arange(ns) * SEL_BLOCK
    overlap = ((c_start[:, None] < s_start[None, :] + SEL_BLOCK) &
               (c_start[:, None] + CMP_LEN > s_start[None, :])).astype(np.float32)
    ovl = np.zeros((NB, NC), np.float32)
    ovl[:ns, :nc] = overlap.T
    kern = functools.partial(_select_kernel, n_sel=min(SEL_TOPK, ns))
    return pl.pallas_call(
        kern,
        grid=(B, S // Q),
        in_specs=[pl.BlockSpec((1, Q, NSA_QW), lambda b, i: (b, i, 0)),
                  pl.BlockSpec((1, 1, NC, LANES), lambda b, i: (0, b, 0, 0)),
                  pl.BlockSpec((1, 1, LANES, NC), lambda b, i: (1, b, 0, 0)),
                  pl.BlockSpec((1, Q, LANES), lambda b, i: (b, i, 0)),
                  _full_spec((NB, NC))],
        out_specs=[pl.BlockSpec((1, NSA_GROUPS, 1, LANES, NSA_HPG * Q), lambda b, i: (b, 0, i, 0, 0)),
                   pl.BlockSpec((1, 1, NSA_QW, Q), lambda b, i: (b, i, 0, 0))],
        out_shape=[jax.ShapeDtypeStruct((B, NSA_GROUPS, S // Q, LANES, NSA_HPG * Q), BF16),
                   jax.ShapeDtypeStruct((B, S // Q, NSA_QW, Q), F32)],
        scratch_shapes=[pltpu.VMEM((NSA_GROUPS, NB, Q), F32), pltpu.VMEM((NSA_GROUPS, NB, Q), jnp.int32)],
        compiler_params=_params(("arbitrary", "arbitrary")),
        name="select",
    )(q, kc, vct, gates, jnp.asarray(ovl, BF16))


def _flash_kernel(qaug_ref, kaug_ref, vst_ref, kw_ref, vwt_ref, g_ref, ocmp_ref, o_ref,
                  m_scr, acc_scr, qw_scr, cap_scr):
    QS = qaug_ref.shape[2]
    R = qaug_ref.shape[4]
    Q = R // NSA_HPG
    KT = kaug_ref.shape[2] // vst_ref.shape[2]
    WT = WINDOW // KT
    SLC, WIN = 0, 1
    CAUSAL, EDGE = 0, 1
    t0 = pl.program_id(1) * QS
    k_row = lax.broadcasted_iota(jnp.int32, (KT, Q), 0)
    i_col = lax.broadcasted_iota(jnp.int32, (KT, Q), 1)
    big = -NEG
    cap_scr[CAUSAL] = jnp.where(k_row <= i_col, big, NEG)
    cap_scr[EDGE] = jnp.where(k_row > i_col, big, NEG)
    m_scr[...] = jnp.full_like(m_scr, NEG)
    acc_scr[...] = jnp.zeros_like(acc_scr)
    zeros = jnp.zeros((HALF, R), BF16)
    for qs in range(QS):
        qw_scr[qs, 0] = jnp.concatenate([qaug_ref[0, 0, qs, 0:HALF, :], zeros], axis=0)
        qw_scr[qs, 1] = jnp.concatenate([zeros, qaug_ref[0, 1, qs, 0:HALF, :]], axis=0)

    def scores(ch):
        br, qs, g, hh, kj, _ = ch
        off = pl.multiple_of(kj * KT, KT)
        if br == SLC:
            return _dot(kaug_ref[0, g, pl.ds(off, KT), :], qaug_ref[0, g, qs, :, hh * Q:(hh + 1) * Q])
        return _dot(kw_ref[0, pl.ds(off, KT), :], qw_scr[qs, g, :, hh * Q:(hh + 1) * Q])

    def absorb(ch, s):
        br, qs, g, hh, kj, cap = ch
        cols = slice(hh * Q, (hh + 1) * Q)
        if cap is not None:
            s = jnp.minimum(s, cap_scr[cap])
        m_prev = m_scr[br, qs, g, :, cols]
        m_new = jnp.maximum(m_prev, jnp.max(s, axis=0, keepdims=True))
        p = jnp.exp2(s - m_new)
        alpha = jnp.exp2(m_prev - m_new)
        vt = vst_ref[0, g, kj] if br == SLC else vwt_ref[0, g, kj]
        acc_scr[br, qs, g, :, cols] = alpha * acc_scr[br, qs, g, :, cols] + _dot(vt, p.astype(BF16))
        m_scr[br, qs, g, :, cols] = m_new

    def run(chains, group=NSA_HEADS):
        groups = [chains[n:n + group] for n in range(0, len(chains), group)]
        s_next = [scores(ch) for ch in groups[0]]
        for n, grp in enumerate(groups):
            s_cur = s_next
            if n + 1 < len(groups):
                s_next = [scores(ch) for ch in groups[n + 1]]
            for ch, s in zip(grp, s_cur):
                absorb(ch, s)

    def tile(br, qs, kj, cap):
        return [(br, qs, g, hh, kj, cap) for g in range(NSA_GROUPS) for hh in range(NSA_HPG)]

    def slc_body(kj, carry):
        run([ch for qs in range(QS) for ch in tile(SLC, qs, kj, None)])
        return carry

    lax.fori_loop(0, t0, slc_body, 0)

    always = []
    later = {}
    for qs in range(QS):
        for kk in range(qs):
            always += tile(SLC, qs, t0 + kk, None)
        always += tile(SLC, qs, t0 + qs, CAUSAL)
        always += tile(WIN, qs, t0 + qs, CAUSAL)
        for back in range(1, WT + 1):
            chains = tile(WIN, qs, t0 + qs - back, EDGE if back == WT else None)
            if back <= qs:
                always += chains
            else:
                later.setdefault(-(-(back - qs) // QS) * QS, []).extend(chains)
    run(always)
    for need, chains in sorted(later.items()):
        @pl.when(t0 >= need)
        def _():
            run(chains)

    for qs in range(QS):
        gt = g_ref[0, qs * Q:(qs + 1) * Q, :].T
        for pb in range(NSA_HEADS // 2):
            halves = []
            for e in range(2):
                h = 2 * pb + e
                g, hh = divmod(h, NSA_HPG)
                cols = slice(hh * Q, (hh + 1) * Q)
                o = ocmp_ref[0, qs, h * NSA_DH:(h + 1) * NSA_DH, :]
                for br in (SLC, WIN):
                    scale = gt[3 * h + 1 + br:3 * h + 2 + br] / acc_scr[br, qs, g, NSA_DH:NSA_DH + 1, cols]
                    o = o + scale * acc_scr[br, qs, g, 0:NSA_DH, cols]
                halves.append(o)
            o_ref[0, qs * Q:(qs + 1) * Q, pb * LANES:(pb + 1) * LANES] = jnp.concatenate(halves, axis=0).T.astype(BF16)


def _flash(qaug, kaug, vst, kw, vwt, gates, ocmp):
    B, _, NQ, _, R = qaug.shape
    Q = R // NSA_HPG
    S = NQ * Q
    QS = min(FLASH_SUBTILES, NQ)
    assert WINDOW % Q == 0 and vst.shape[4] == Q and NQ % QS == 0
    per_b = lambda shape: pl.BlockSpec((1,) + shape, lambda b, i: (b,) + (0,) * len(shape))
    return pl.pallas_call(
        _flash_kernel,
        grid=(B, NQ // QS),
        in_specs=[pl.BlockSpec((1, NSA_GROUPS, QS, LANES, R), lambda b, i: (b, 0, i, 0, 0)),
                  per_b((NSA_GROUPS, S, LANES)), per_b(vst.shape[1:]), per_b((S, LANES)), per_b(vwt.shape[1:]),
                  pl.BlockSpec((1, QS * Q, LANES), lambda b, i: (b, i, 0)),
                  pl.BlockSpec((1, QS, NSA_QW, Q), lambda b, i: (b, i, 0, 0))],
        out_specs=pl.BlockSpec((1, QS * Q, NSA_QW), lambda b, i: (b, i, 0)),
        out_shape=jax.ShapeDtypeStruct((B, S, NSA_QW), BF16),
        scratch_shapes=[pltpu.VMEM((2, QS, NSA_GROUPS, 1, R), F32),
                        pltpu.VMEM((2, QS, NSA_GROUPS, VAL_ROWS, R), F32),
                        pltpu.VMEM((QS, NSA_GROUPS, LANES, R), BF16), pltpu.VMEM((2, Q, Q), F32)],
        compiler_params=_params(("arbitrary", "arbitrary")),
        name="flash",
    )(qaug, kaug, vst, kw, vwt, gates, ocmp)


def _retention_kernel(rq_ref, rk_ref, rv_ref, dmat_ref, qdec_ref, kdec_ref, cdec_ref, nw_ref, o_ref, st_scr):
    C = rq_ref.shape[1]

    @pl.when(pl.program_id(1) == 0)
    def _():
        st_scr[...] = jnp.zeros_like(st_scr)

    low = lax.broadcasted_iota(jnp.int32, (C, LANES), 1) < HALF
    same_head = ((lax.broadcasted_iota(jnp.int32, (LANES, LANES), 0) < HALF) ==
                 (lax.broadcasted_iota(jnp.int32, (LANES, LANES), 1) < HALF))
    for p in range(RET_HEADS // 2):
        sl = slice(p * LANES, (p + 1) * LANES)
        q2, k2, v2 = rq_ref[0, :, sl], rk_ref[0, :, sl], rv_ref[0, :, sl]
        q2f = q2.astype(F32)
        qe = jnp.where(low, q2f, 0.0).astype(BF16)
        qo = jnp.where(low, 0.0, q2f).astype(BF16)
        ie = (_dot_nt(qe, k2) * dmat_ref[2 * p]).astype(BF16)
        io = (_dot_nt(qo, k2) * dmat_ref[2 * p + 1]).astype(BF16)
        intra = jnp.where(low, _dot(ie, v2), _dot(io, v2))
        st = st_scr[p]
        o = intra + _dot(q2, st.astype(BF16)) * qdec_ref[:, sl]
        kd = (k2.astype(F32) * kdec_ref[:, sl]).astype(BF16)
        st_scr[p] = st * cdec_ref[p:p + 1, :] + jnp.where(same_head, _dot_tn(kd, v2), 0.0)
        s_lo = jnp.sum(jnp.where(low, o, 0.0), axis=1, keepdims=True)
        s_all = jnp.sum(o, axis=1, keepdims=True)
        d = o - jnp.where(low, s_lo, s_all - s_lo) * (1.0 / RET_DV)
        d2 = d * d
        v_lo = jnp.sum(jnp.where(low, d2, 0.0), axis=1, keepdims=True)
        v_all = jnp.sum(d2, axis=1, keepdims=True)
        var = jnp.where(low, v_lo, v_all - v_lo) * (1.0 / RET_DV)
        o_ref[0, :, sl] = d * lax.rsqrt(var + EPS) * nw_ref[:, sl]


def _retention(rq, rk, rv, ret_norm_w):
    B, S, _ = rq.shape
    C = min(RET_CHUNK, S)
    H = RET_HEADS
    log_g = jnp.log(1.0 - 2.0 ** (-5.0 - jnp.arange(H, dtype=F32)))
    i = jnp.arange(C, dtype=F32)
    diff = i[:, None] - i[None, :]
    dmat = jnp.where(diff >= 0, jnp.exp(log_g[:, None, None] * jnp.maximum(diff, 0.0)), 0.0)
    per_lane = lambda a: jnp.repeat(a, RET_DK, axis=-1)
    qdec = per_lane(jnp.exp(log_g[None, :] * (i[:, None] + 1.0)))
    kdec = per_lane(jnp.exp(log_g[None, :] * (C - 1.0 - i[:, None])))
    cdec = per_lane(jnp.exp(log_g * C)[None, :]).reshape(H // 2, LANES)
    tok = lambda w: pl.BlockSpec((1, C, w), lambda b, i: (b, i, 0))
    return pl.pallas_call(
        _retention_kernel,
        grid=(B, S // C),
        in_specs=[tok(RET_QW), tok(RET_QW), tok(RET_VW),
                  _full_spec((H, C, C)), _full_spec((C, RET_QW)), _full_spec((C, RET_QW)),
                  _full_spec((H // 2, LANES)), _full_spec((1, RET_VW))],
        out_specs=tok(RET_VW),
        out_shape=jax.ShapeDtypeStruct((B, S, RET_VW), F32),
        scratch_shapes=[pltpu.VMEM((H // 2, LANES, LANES), F32)],
        compiler_params=_params(("arbitrary", "arbitrary")),
        name="retention",
    )(rq, rk, rv, dmat, qdec, kdec, cdec, ret_norm_w.reshape(1, RET_VW))


def _mix_kernel(h_ref, mod_ref, nw_ref, onsa_ref, oret_ref, wrg_ref, wga_ref, wgb_ref, wn_ref, wr_ref, wo_ref, o_ref):
    x = h_ref[0]
    sh = mod_ref[0, 3:4, :]
    sc = mod_ref[0, 4:5, :]
    gt = mod_ref[0, 5:6, :]
    u = _norm_mod(x, nw_ref[...], sc, sh).astype(BF16)
    rg = _dot(u, wrg_ref[...])
    oret = (oret_ref[0] * _silu(rg)).astype(BF16)
    y_nsa = _dot(onsa_ref[0], wn_ref[...])
    y_ret = _dot(oret, wr_ref[...])
    ga = jax.nn.sigmoid(_dot(u, wga_ref[...]))
    gb = jax.nn.sigmoid(_dot(u, wgb_ref[...]))
    mixed = (ga * y_nsa + gb * y_ret).astype(BF16)
    o_ref[0] = x + gt * _dot(mixed, wo_ref[...])


def _mix(h, mod, norm_w, o_nsa, o_ret, w_in, w_nsa_up, w_ret_up, w_out, *, tm):
    B, S, D = h.shape
    o = np.cumsum([0, NSA_QW] + [NSA_KVW] * 6 + [3 * NSA_HEADS, RET_QW, RET_QW, RET_VW, RET_VW, D, D])
    wrg = w_in[:, o[11]:o[12]].astype(BF16)
    wga = w_in[:, o[12]:o[13]].astype(BF16)
    wgb = w_in[:, o[13]:o[14]].astype(BF16)
    wn, wr, wo = w_nsa_up.astype(BF16), w_ret_up.astype(BF16), w_out.astype(BF16)
    tok = lambda w: pl.BlockSpec((1, tm, w), lambda b, i: (b, i, 0))
    return pl.pallas_call(
        _mix_kernel,
        grid=(B, S // tm),
        in_specs=[tok(D), pl.BlockSpec((1, 9, D), lambda b, i: (b, 0, 0)), _full_spec((1, D)),
                  tok(NSA_QW), tok(RET_VW),
                  _full_spec(wrg.shape), _full_spec(wga.shape), _full_spec(wgb.shape),
                  _full_spec(wn.shape), _full_spec(wr.shape), _full_spec(wo.shape)],
        out_specs=tok(D),
        out_shape=jax.ShapeDtypeStruct((B, S, D), F32),
        compiler_params=_params(("arbitrary", "arbitrary")),
        name="mix",
    )(h, mod, norm_w.reshape(1, D), o_nsa, o_ret, wrg, wga, wgb, wn, wr, wo)


def kernel(x, c, ada_w, ada_b, norm1_w, ffn1_w_in, ffn1_w_out, norm2_w, w_in, cmp_k_pe, cmp_k_w1, cmp_k_w2,
           cmp_v_pe, cmp_v_w1, cmp_v_w2, ret_norm_w, w_nsa_up, w_ret_up, w_out, norm3_w, ffn2_w_in, ffn2_w_out,
           final_norm_w):
    B, S, D = x.shape
    depth = ada_w.shape[0]
    assert depth >= 1
    tm = min(512, S)
    h = x
    for l in range(depth):
        last = l == depth - 1
        mod = _modulation(c, ada_w[l], ada_b[l]).reshape(B, 9, D)
        h = _ffn(h, mod, norm1_w[l], final_norm_w, ffn1_w_in[l], ffn1_w_out[l], mod_base=0, final_norm=False, tm=tm)
        q, cv, kaug, vst, kw, vwt, gates, rq, rk, rv = _projection(h, mod, norm2_w[l], w_in[l], tm=tm)
        kvc, kvct = _compress(cv, (cmp_k_pe[l], cmp_v_pe[l]), (cmp_k_w1[l], cmp_v_w1[l]), (cmp_k_w2[l], cmp_v_w2[l]))
        qaug, ocmp = _select(q, kvc, kvct, gates)
        o_nsa = _flash(qaug, kaug, vst, kw, vwt, gates, ocmp)
        o_ret = _retention(rq, rk, rv, ret_norm_w[l])
        h = _mix(h, mod, norm2_w[l], o_nsa, o_ret, w_in[l], w_nsa_up[l], w_ret_up[l], w_out[l], tm=tm)
        h = _ffn(h, mod, norm3_w[l], final_norm_w, ffn2_w_in[l], ffn2_w_out[l], mod_base=6, final_norm=last, tm=tm)
    return h
```

```python
import functools

import numpy as np
import jax
import jax.numpy as jnp
from jax import lax
from jax.experimental import pallas as pl
from jax.experimental.pallas import tpu as pltpu

F32 = jnp.float32
BF16 = jnp.bfloat16

NSA_HEADS = 8
NSA_GROUPS = 2
NSA_HPG = NSA_HEADS // NSA_GROUPS
NSA_DH = 64
CMP_STRIDE = 16
CMP_LEN = 2 * CMP_STRIDE
CMP_HIDDEN = 128
SEL_BLOCK = 64
SEL_TOPK = 16
WINDOW = 512
FORCE_BONUS = 1e4
RET_HEADS = 8
RET_DK = 64
RET_DV = 64
ROPE_BASE = 10000.0
D_FF = 2816
EPS = 1e-6
NEG = -1e30

NSA_QW = NSA_HEADS * NSA_DH
NSA_KVW = NSA_GROUPS * NSA_DH
RET_QW = RET_HEADS * RET_DK
RET_VW = RET_HEADS * RET_DV

LANES = 128
HALF = LANES // 2
VMEM_LIMIT = 56 * 1024 * 1024
FFN_CHUNK = 256
RET_CHUNK = 256
ATT_Q = 256
VAL_ROWS = NSA_DH + 16
LOG2E = 1.4426950408889634
FLASH_SUBTILES = 2
RUN_AHEAD = 3
LOOP_AHEAD = 2


def _dot(a, b):
    return jnp.dot(a, b, preferred_element_type=F32)


def _dot_nt(a, b):
    return lax.dot_general(a, b, (((1,), (1,)), ((), ())), preferred_element_type=F32)


def _dot_tn(a, b):
    return lax.dot_general(a, b, (((0,), (0,)), ((), ())), preferred_element_type=F32)


def _silu(a):
    return a * jax.nn.sigmoid(a)


def _norm_mod(x, nw, sc, sh):
    ms = jnp.mean(x * x, axis=-1, keepdims=True)
    y = x * lax.rsqrt(ms + EPS) * nw
    return y * (1.0 + sc) + sh


def _full_spec(shape):
    zeros = (0,) * len(shape)
    return pl.BlockSpec(shape, lambda *_: zeros)


def _params(sem):
    return pltpu.CompilerParams(dimension_semantics=sem, vmem_limit_bytes=VMEM_LIMIT)


def _mod_kernel(c_ref, w_ref, b_ref, o_ref):
    c = c_ref[...]
    o_ref[...] = _dot(_silu(c).astype(BF16), w_ref[...].astype(BF16)) + b_ref[...]


def _modulation(c, ada_w, ada_b):
    B, D = c.shape
    N = ada_w.shape[1]
    tn = N // 8
    return pl.pallas_call(
        _mod_kernel,
        grid=(N // tn,),
        in_specs=[pl.BlockSpec((B, D), lambda j: (0, 0)),
                  pl.BlockSpec((D, tn), lambda j: (0, j)),
                  pl.BlockSpec((1, tn), lambda j: (0, j))],
        out_specs=pl.BlockSpec((B, tn), lambda j: (0, j)),
        out_shape=jax.ShapeDtypeStruct((B, N), F32),
        compiler_params=_params(("arbitrary",)),
        name="modulation",
    )(c, ada_w, ada_b.reshape(1, N))


def _ffn_kernel(x_ref, mod_ref, nw_ref, fw_ref, wi_ref, wo_ref, o_ref, u_scr, acc_scr,
                *, mod_base, final_norm):
    x = x_ref[0]
    sh = mod_ref[0, mod_base:mod_base + 1, :]
    sc = mod_ref[0, mod_base + 1:mod_base + 2, :]
    gt = mod_ref[0, mod_base + 2:mod_base + 3, :]
    u_scr[...] = _norm_mod(x, nw_ref[...], sc, sh).astype(BF16)
    acc_scr[...] = jnp.zeros_like(acc_scr)
    dff = wo_ref.shape[0]
    for c0 in range(0, dff, FFN_CHUNK):
        u = u_scr[...]
        a = _dot(u, wi_ref[:, c0:c0 + FFN_CHUNK])
        b = _dot(u, wi_ref[:, dff + c0:dff + c0 + FFN_CHUNK])
        acc_scr[...] += _dot((_silu(a) * b).astype(BF16), wo_ref[c0:c0 + FFN_CHUNK, :])
    y = x + 0.5 * gt * acc_scr[...]
    if final_norm:
        ms = jnp.mean(y * y, axis=-1, keepdims=True)
        y = y * lax.rsqrt(ms + EPS) * fw_ref[...]
    o_ref[0] = y


def _ffn(x, mod, norm_w, final_w, w_in, w_out, *, mod_base, final_norm, tm):
    B, S, D = x.shape
    assert w_out.shape[0] % FFN_CHUNK == 0
    wi = w_in.astype(BF16)
    wo = w_out.astype(BF16)
    kern = functools.partial(_ffn_kernel, mod_base=mod_base, final_norm=final_norm)
    return pl.pallas_call(
        kern,
        grid=(B, S // tm),
        in_specs=[pl.BlockSpec((1, tm, D), lambda b, i: (b, i, 0)),
                  pl.BlockSpec((1, 9, D), lambda b, i: (b, 0, 0)),
                  _full_spec((1, D)), _full_spec((1, D)),
                  _full_spec(wi.shape), _full_spec(wo.shape)],
        out_specs=pl.BlockSpec((1, tm, D), lambda b, i: (b, i, 0)),
        out_shape=jax.ShapeDtypeStruct((B, S, D), F32),
        scratch_shapes=[pltpu.VMEM((tm, D), BF16), pltpu.VMEM((tm, D), F32)],
        compiler_params=_params(("arbitrary", "arbitrary")),
        name="ffn_final" if final_norm else "ffn",
    )(x, mod, norm_w.reshape(1, D), final_w.reshape(1, D), wi, wo)


def _proj_kernel(h_ref, mod_ref, nw_ref, wq_ref, wkv_ref, wg_ref, wr_ref, cos_ref, sin_ref,
                 q_ref, cv_ref, kaug_ref, vst_ref, kw_ref, vwt_ref, g_ref, rq_ref, rk_ref, rv_ref,
                 u_scr):
    tm = h_ref.shape[1]
    x = h_ref[0]
    sh = mod_ref[0, 3:4, :]
    sc = mod_ref[0, 4:5, :]
    u_scr[...] = _norm_mod(x, nw_ref[...], sc, sh).astype(BF16)
    u = u_scr[...]

    q_ref[0] = _dot(u, wq_ref[...]).astype(BF16)

    kv = _dot(u, wkv_ref[...])
    cv_ref[0, 0] = kv[:, 0:LANES].astype(BF16)
    cv_ref[1, 0] = kv[:, LANES:2 * LANES].astype(BF16)
    ks = kv[:, 2 * LANES:3 * LANES]
    lane = lax.broadcasted_iota(jnp.int32, (tm, LANES), 1)
    row = lax.broadcasted_iota(jnp.int32, (tm, LANES), 0)
    low = lane < HALF
    blk = (pl.program_id(1) * tm + row) // SEL_BLOCK
    onehot = jnp.where(lane - HALF == blk, 1.0, 0.0)
    kaug_ref[0, 0] = jnp.where(low, ks, onehot).astype(BF16)
    kaug_ref[0, 1] = jnp.where(low, pltpu.roll(ks, HALF, 1), onehot).astype(BF16)
    kw_ref[0] = kv[:, 4 * LANES:5 * LANES].astype(BF16)
    kt = vst_ref.shape[4]
    ones = jnp.ones((VAL_ROWS - NSA_DH, tm), F32)
    for ref, col in ((vst_ref, 3), (vwt_ref, 5)):
        vt = kv[:, col * LANES:(col + 1) * LANES].T
        for g in range(NSA_GROUPS):
            aug = jnp.concatenate([vt[g * HALF:(g + 1) * HALF], ones], axis=0).astype(BF16)
            for c in range(tm // kt):
                ref[0, g, c] = aug[:, c * kt:(c + 1) * kt]

    g_ref[0] = jax.nn.sigmoid(_dot(u, wg_ref[...]))

    r = _dot(u, wr_ref[...])
    cos = cos_ref[...]
    sin = sin_ref[...]
    first = (lane & (RET_DK // 2)) == 0
    npair = RET_QW // LANES
    for j in range(2 * npair):
        xb = r[:, j * LANES:(j + 1) * LANES]
        partner = jnp.where(first, pltpu.roll(xb, LANES - RET_DK // 2, 1), pltpu.roll(xb, RET_DK // 2, 1))
        y = (xb * cos + partner * sin).astype(BF16)
        if j < npair:
            rq_ref[0, :, j * LANES:(j + 1) * LANES] = y
        else:
            rk_ref[0, :, (j - npair) * LANES:(j - npair + 1) * LANES] = y
    rv_ref[0] = r[:, 2 * RET_QW:2 * RET_QW + RET_VW].astype(BF16)


def _projection(h, mod, norm_w, w_in, *, tm):
    B, S, D = h.shape
    o = np.cumsum([0, NSA_QW] + [NSA_KVW] * 6 + [3 * NSA_HEADS, RET_QW, RET_QW, RET_VW])
    scale_q = NSA_DH ** -0.5 * LOG2E
    scale_k = RET_DK ** -0.5
    wq = (w_in[:, o[0]:o[1]] * scale_q).astype(BF16)
    wkv = w_in[:, o[1]:o[7]].astype(BF16)
    wg = jnp.pad(w_in[:, o[7]:o[8]], ((0, 0), (0, LANES - 3 * NSA_HEADS))).astype(BF16)
    wr = jnp.concatenate([w_in[:, o[8]:o[9]], w_in[:, o[9]:o[10]] * scale_k, w_in[:, o[10]:o[11]]],
                         axis=1).astype(BF16)
    half = RET_DK // 2
    lane = np.arange(LANES)
    inv = ROPE_BASE ** (-np.arange(half, dtype=np.float64) / half)
    ang = np.arange(S, dtype=np.float64)[:, None] * inv[lane % half][None, :]
    cos_t = jnp.asarray(np.cos(ang), F32)
    sin_t = jnp.asarray(np.where((lane % RET_DK) < half, -np.sin(ang), np.sin(ang)), F32)
    tok = lambda w: pl.BlockSpec((1, tm, w), lambda b, i: (b, i, 0))
    kt = min(ATT_Q, S)
    vt_spec = pl.BlockSpec((1, NSA_GROUPS, tm // kt, VAL_ROWS, kt), lambda b, i: (b, 0, i, 0, 0))
    vt_shape = jax.ShapeDtypeStruct((B, NSA_GROUPS, S // kt, VAL_ROWS, kt), BF16)
    outs = pl.pallas_call(
        _proj_kernel,
        grid=(B, S // tm),
        in_specs=[tok(D), pl.BlockSpec((1, 9, D), lambda b, i: (b, 0, 0)), _full_spec((1, D)),
                  _full_spec(wq.shape), _full_spec(wkv.shape), _full_spec(wg.shape), _full_spec(wr.shape),
                  pl.BlockSpec((tm, LANES), lambda b, i: (i, 0)), pl.BlockSpec((tm, LANES), lambda b, i: (i, 0))],
        out_specs=[tok(NSA_QW),
                   pl.BlockSpec((2, 1, tm, LANES), lambda b, i: (0, b, i, 0)),
                   pl.BlockSpec((1, NSA_GROUPS, tm, LANES), lambda b, i: (b, 0, i, 0)),
                   vt_spec, tok(LANES), vt_spec, tok(LANES),
                   tok(RET_QW), tok(RET_QW), tok(RET_VW)],
        out_shape=[jax.ShapeDtypeStruct((B, S, NSA_QW), BF16),
                   jax.ShapeDtypeStruct((2, B, S, LANES), BF16),
                   jax.ShapeDtypeStruct((B, NSA_GROUPS, S, LANES), BF16),
                   vt_shape,
                   jax.ShapeDtypeStruct((B, S, LANES), BF16),
                   vt_shape,
                   jax.ShapeDtypeStruct((B, S, LANES), F32),
                   jax.ShapeDtypeStruct((B, S, RET_QW), BF16),
                   jax.ShapeDtypeStruct((B, S, RET_QW), BF16),
                   jax.ShapeDtypeStruct((B, S, RET_VW), BF16)],
        scratch_shapes=[pltpu.VMEM((tm, D), BF16)],
        compiler_params=_params(("arbitrary", "arbitrary")),
        name="projection",
    )(h, mod, norm_w.reshape(1, D), wq, wkv, wg, wr, cos_t, sin_t)
    return outs


def _compress_kernel(ch_ref, w1c_ref, pe_ref, w1_ref, w2_ref, o_ref, ot_ref):
    ch = ch_ref[0, 0]
    nc = ch.shape[0]
    ab = _dot(ch, w1c_ref[0])
    pt = _dot(pe_ref[0], w1_ref[0])[0:1, :]
    pt2 = jnp.concatenate([pt, pt], axis=1)
    nh = NSA_GROUPS * CMP_HIDDEN
    hid = ab[:, :nh] + pltpu.roll(ab[:, nh:], nc - 1, 0) + pt2
    out = _dot(_silu(hid).astype(BF16), w2_ref[0])
    row = lax.broadcasted_iota(jnp.int32, out.shape, 0)
    out = jnp.where(row < nc - 1, out, 0.0)
    o_ref[0, 0] = out.astype(BF16)
    ot_ref[0, 0] = out.T.astype(BF16)


def _compress(cv, pe, w1, w2):
    _, B, S, _ = cv.shape
    nc = S // CMP_STRIDE
    width = CMP_STRIDE * LANES
    chunks = cv.reshape(2, B, nc, width)
    eye = jnp.eye(NSA_GROUPS, dtype=F32)
    nh = NSA_GROUPS * CMP_HIDDEN

    def expand(w):
        w = w.reshape(CMP_STRIDE, NSA_DH, CMP_HIDDEN)
        return jnp.einsum("ldj,gh->lgdhj", w, eye).reshape(width, nh)

    half = CMP_STRIDE * NSA_DH
    w1c = jnp.stack([jnp.concatenate([expand(w[:half]), expand(w[half:])], axis=1) for w in w1]).astype(BF16)
    w2bd = jnp.stack([jnp.einsum("jd,gh->gjhd", w, eye).reshape(nh, NSA_KVW) for w in w2]).astype(BF16)
    pe8 = jnp.stack([jnp.broadcast_to(p.reshape(1, CMP_LEN * NSA_DH), (8, CMP_LEN * NSA_DH)) for p in pe]).astype(BF16)
    w1s = jnp.stack(w1).astype(BF16)
    sel = lambda shape: pl.BlockSpec((1,) + shape, lambda t, b: (t,) + (0,) * len(shape))
    return pl.pallas_call(
        _compress_kernel,
        grid=(2, B),
        in_specs=[pl.BlockSpec((1, 1, nc, width), lambda t, b: (t, b, 0, 0)),
                  sel(w1c.shape[1:]), sel(pe8.shape[1:]), sel(w1s.shape[1:]), sel(w2bd.shape[1:])],
        out_specs=[pl.BlockSpec((1, 1, nc, LANES), lambda t, b: (t, b, 0, 0)),
                   pl.BlockSpec((1, 1, LANES, nc), lambda t, b: (t, b, 0, 0))],
        out_shape=[jax.ShapeDtypeStruct((2, B, nc, LANES), BF16),
                   jax.ShapeDtypeStruct((2, B, LANES, nc), BF16)],
        compiler_params=_params(("arbitrary", "arbitrary")),
        name="compress",
    )(chunks, w1c, pe8, w1s, w2bd)


def _select_kernel(q_ref, kc_ref, vct_ref, g_ref, ovl_ref, qaug_ref, ocmp_ref, imp_scr, rank_scr, *, n_sel):
    Q = q_ref.shape[1]
    NC = kc_ref.shape[2]
    NB = ovl_ref.shape[0]
    qi = pl.program_id(1)
    q0 = qi * Q
    kcf = kc_ref[0, 0].astype(F32)
    kcr = pltpu.roll(kcf, HALF, 1)
    lowk = lax.broadcasted_iota(jnp.int32, (NC, LANES), 1) < HALF
    k_even = [jnp.where(lowk, kcf, 0.0).astype(BF16), jnp.where(lowk, kcr, 0.0).astype(BF16)]
    k_odd = [jnp.where(lowk, 0.0, kcr).astype(BF16), jnp.where(lowk, 0.0, kcf).astype(BF16)]
    vct = vct_ref[0, 0]
    gt = g_ref[0].T
    n_row = lax.broadcasted_iota(jnp.int32, (NC, Q), 0)
    t_col = q0 + lax.broadcasted_iota(jnp.int32, (NC, Q), 1)
    allowed = n_row * CMP_STRIDE + (CMP_LEN - 1) <= t_col
    any_allowed = q0 + lax.broadcasted_iota(jnp.int32, (1, Q), 1) >= CMP_LEN - 1
    ovl = ovl_ref[...]
    j_blk = lax.broadcasted_iota(jnp.int32, (NB, Q), 0)
    t_blk = q0 + lax.broadcasted_iota(jnp.int32, (NB, Q), 1)
    cur = t_blk // SEL_BLOCK
    forced = (j_blk == 0) | (j_blk == cur) | (j_blk == cur - 1)
    valid = j_blk * SEL_BLOCK <= t_blk
    j_sub = lax.broadcasted_iota(jnp.int32, (8, Q), 0)
    per_tile = Q // SEL_BLOCK

    scores = []
    for pb in range(NSA_HEADS // 2):
        g, pp = divmod(pb, NSA_HPG // 2)
        qt = q_ref[0, :, pb * LANES:(pb + 1) * LANES].astype(F32).T.astype(BF16)
        for e, kk in enumerate((k_even[g], k_odd[g])):
            scores.append(_dot(kk, qt))
            col = (2 * pp + e) * Q
            qaug_ref[0, g, 0, 0:HALF, col:col + Q] = qt[e * HALF:(e + 1) * HALF]

    for g in range(NSA_GROUPS):
        psum = jnp.zeros((NC, Q), F32)
        for hh in range(NSA_HPG):
            h = g * NSA_HPG + hh
            s = jnp.where(allowed, scores[h], NEG)
            mx = jnp.max(s, axis=0, keepdims=True)
            ex = jnp.exp2(s - mx)
            den = jnp.sum(ex, axis=0, keepdims=True)
            p = ex * jnp.where(any_allowed, 1.0 / den, 0.0)
            psum = psum + p
            o = _dot(vct, p.astype(BF16))[g * HALF:(g + 1) * HALF]
            ocmp_ref[0, 0, h * NSA_DH:(h + 1) * NSA_DH, :] = gt[3 * h:3 * h + 1] * o

        hi = psum.astype(BF16)
        lo = (psum - hi.astype(F32)).astype(BF16)
        imp = _dot(ovl, hi) + _dot(ovl, lo)
        imp = jnp.where(forced, imp + FORCE_BONUS, imp)
        imp_scr[g] = jnp.where(valid, imp, -FORCE_BONUS)
    rank_scr[...] = jnp.zeros_like(rank_scr)

    quads = 4
    for t in range(0, NB // per_tile, quads):
        @pl.when(t <= qi)
        def _():
            for g in range(NSA_GROUPS):
                vals = [imp_scr[g, 8 * sb:8 * sb + 8, :] for sb in range(NB // 8)]
                cnts = [rank_scr[g, 8 * sb:8 * sb + 8, :] for sb in range(NB // 8)]
                for jp in range(t * per_tile, (t + quads) * per_tile):
                    r = jnp.broadcast_to(imp_scr[g, jp:jp + 1, :], (8, Q))
                    for sb, v in enumerate(vals):
                        if 8 * sb > jp:
                            ahead = jnp.where(r >= v, 1, 0)
                        elif 8 * sb + 7 <= jp:
                            ahead = jnp.where(r > v, 1, 0)
                        else:
                            ahead = jnp.where(j_sub + 8 * sb > jp, jnp.where(r >= v, 1, 0), jnp.where(r > v, 1, 0))
                        cnts[sb] = cnts[sb] + ahead
                for sb, cnt in enumerate(cnts):
                    rank_scr[g, 8 * sb:8 * sb + 8, :] = cnt

    for g in range(NSA_GROUPS):
        sel = (rank_scr[g] < n_sel) & valid
        bias = jnp.where(sel, 0.0, NEG).astype(BF16)
        for hh in range(NSA_HPG):
            qaug_ref[0, g, 0, HALF:HALF + NB, hh * Q:(hh + 1) * Q] = bias


def _select(q, kc, vct, gates):
    B, S, _ = q.shape
    Q = min(ATT_Q, S)
    NC = kc.shape[2]
    nc, ns = NC - 1, S // SEL_BLOCK
    NB = LANES - HALF
    assert ns <= NB and Q % SEL_BLOCK == 0
    c_start = np.arange(nc) * CMP_STRIDE
    s_start = np.arange(ns) * SEL_BLOCK
    overlap = ((c_start[:, None] < s_start[None, :] + SEL_BLOCK) &
               (c_start[:, None] + CMP_LEN > s_start[None, :])).astype(np.float32)
    ovl = np.zeros((NB, NC), np.float32)
    ovl[:ns, :nc] = overlap.T
    kern = functools.partial(_select_kernel, n_sel=min(SEL_TOPK, ns))
    return pl.pallas_call(
        kern,
        grid=(B, S // Q),
        in_specs=[pl.BlockSpec((1, Q, NSA_QW), lambda b, i: (b, i, 0)),
                  pl.BlockSpec((1, 1, NC, LANES), lambda b, i: (0, b, 0, 0)),
                  pl.BlockSpec((1, 1, LANES, NC), lambda b, i: (1, b, 0, 0)),
                  pl.BlockSpec((1, Q, LANES), lambda b, i: (b, i, 0)),
                  _full_spec((NB, NC))],
        out_specs=[pl.BlockSpec((1, NSA_GROUPS, 1, LANES, NSA_HPG * Q), lambda b, i: (b, 0, i, 0, 0)),
                   pl.BlockSpec((1, 1, NSA_QW, Q), lambda b, i: (b, i, 0, 0))],
        out_shape=[jax.ShapeDtypeStruct((B, NSA_GROUPS, S // Q, LANES, NSA_HPG * Q), BF16),
                   jax.ShapeDtypeStruct((B, S // Q, NSA_QW, Q), F32)],
        scratch_shapes=[pltpu.VMEM((NSA_GROUPS, NB, Q), F32), pltpu.VMEM((NSA_GROUPS, NB, Q), jnp.int32)],
        compiler_params=_params(("arbitrary", "arbitrary")),
        name="select",
    )(q, kc, vct, gates, jnp.asarray(ovl, BF16))


def _flash_kernel(qaug_ref, kaug_ref, vst_ref, kw_ref, vwt_ref, g_ref, ocmp_ref, o_ref,
                  m_scr, acc_scr, qw_scr, cap_scr, s_scr):
    QS = qaug_ref.shape[2]
    R = qaug_ref.shape[4]
    Q = R // NSA_HPG
    KT = kaug_ref.shape[2] // vst_ref.shape[2]
    WT = WINDOW // KT
    SLC, WIN = 0, 1
    CAUSAL, EDGE = 0, 1
    t0 = pl.program_id(1) * QS
    k_row = lax.broadcasted_iota(jnp.int32, (KT, Q), 0)
    i_col = lax.broadcasted_iota(jnp.int32, (KT, Q), 1)
    big = -NEG
    cap_scr[CAUSAL] = jnp.where(k_row <= i_col, big, NEG)
    cap_scr[EDGE] = jnp.where(k_row > i_col, big, NEG)
    m_scr[...] = jnp.full_like(m_scr, NEG)
    acc_scr[...] = jnp.zeros_like(acc_scr)
    zeros = jnp.zeros((HALF, R), BF16)
    for qs in range(QS):
        qw_scr[qs, 0] = jnp.concatenate([qaug_ref[0, 0, qs, 0:HALF, :], zeros], axis=0)
        qw_scr[qs, 1] = jnp.concatenate([zeros, qaug_ref[0, 1, qs, 0:HALF, :]], axis=0)

    def scores(ch):
        br, qs, g, hh, kj, _ = ch
        off = pl.multiple_of(kj * KT, KT)
        if br == SLC:
            return _dot(kaug_ref[0, g, pl.ds(off, KT), :], qaug_ref[0, g, qs, :, hh * Q:(hh + 1) * Q])
        return _dot(kw_ref[0, pl.ds(off, KT), :], qw_scr[qs, g, :, hh * Q:(hh + 1) * Q])

    def absorb(ch, s):
        br, qs, g, hh, kj, cap = ch
        cols = slice(hh * Q, (hh + 1) * Q)
        if cap is not None:
            s = jnp.minimum(s, cap_scr[cap])
        m_prev = m_scr[br, qs, g, :, cols]
        m_new = jnp.maximum(m_prev, jnp.max(s, axis=0, keepdims=True))
        p = jnp.exp2(s - m_new)
        alpha = jnp.exp2(m_prev - m_new)
        vt = vst_ref[0, g, kj] if br == SLC else vwt_ref[0, g, kj]
        acc_scr[br, qs, g, :, cols] = alpha * acc_scr[br, qs, g, :, cols] + _dot(vt, p.astype(BF16))
        m_scr[br, qs, g, :, cols] = m_new

    NSH = QS * NSA_HEADS
    RING = 2 * NSH

    def run(chains, ready=()):
        order = list(ready) + list(chains)
        assert len(ready) + RUN_AHEAD < RING
        for t in range(-RUN_AHEAD, len(order)):
            n = t + RUN_AHEAD
            if len(ready) <= n < len(order):
                s_scr[(n % RING) // NSH, n % NSH] = scores(order[n])
            if t >= 0:
                absorb(order[t], s_scr[(t % RING) // NSH, t % NSH])

    def tile(br, qs, kj, cap):
        return [(br, qs, g, hh, kj, cap) for g in range(NSA_GROUPS) for hh in range(NSA_HPG)]

    def shared(kj, first_cap=None):
        return [ch for qs in range(QS) for ch in tile(SLC, qs, kj, first_cap if qs == 0 else None)]

    def issue(kj, slot, n):
        s_scr[slot, n] = scores(shared(kj)[n])

    for n in range(NSH):
        issue(0, 0, n)

    def slc_body(j, carry):
        produce = [(2 * j + 1, 1, n) for n in range(NSH)] + [(2 * j + 2, 0, n) for n in range(NSH)]
        absorbs = [(2 * j, 0, n) for n in range(NSH)] + [(2 * j + 1, 1, n) for n in range(NSH)]
        for t in range(-LOOP_AHEAD, len(absorbs)):
            if t + LOOP_AHEAD < len(produce):
                issue(*produce[t + LOOP_AHEAD])
            if t >= 0:
                kj, slot, n = absorbs[t]
                absorb(shared(kj)[n], s_scr[slot, n])
        return carry

    lax.fori_loop(0, t0 // 2, slc_body, 0)
    ready = shared(t0, CAUSAL)

    always = []
    later = {}
    for qs in range(QS):
        for kk in range(1, qs):
            always += tile(SLC, qs, t0 + kk, None)
        if qs > 0:
            always += tile(SLC, qs, t0 + qs, CAUSAL)
        always += tile(WIN, qs, t0 + qs, CAUSAL)
        for back in range(1, WT + 1):
            chains = tile(WIN, qs, t0 + qs - back, EDGE if back == WT else None)
            if back <= qs:
                always += chains
            else:
                later.setdefault(-(-(back - qs) // QS) * QS, []).extend(chains)
    run(always, ready)
    for need, chains in sorted(later.items()):
        @pl.when(t0 >= need)
        def _():
            run(chains)

    for qs in range(QS):
        gt = g_ref[0, qs * Q:(qs + 1) * Q, :].T
        for pb in range(NSA_HEADS // 2):
            halves = []
            for e in range(2):
                h = 2 * pb + e
                g, hh = divmod(h, NSA_HPG)
                cols = slice(hh * Q, (hh + 1) * Q)
                o = ocmp_ref[0, qs, h * NSA_DH:(h + 1) * NSA_DH, :]
                for br in (SLC, WIN):
                    scale = gt[3 * h + 1 + br:3 * h + 2 + br] / acc_scr[br, qs, g, NSA_DH:NSA_DH + 1, cols]
                    o = o + scale * acc_scr[br, qs, g, 0:NSA_DH, cols]
                halves.append(o)
            o_ref[0, qs * Q:(qs + 1) * Q, pb * LANES:(pb + 1) * LANES] = jnp.concatenate(halves, axis=0).T.astype(BF16)


def _flash(qaug, kaug, vst, kw, vwt, gates, ocmp):
    B, _, NQ, _, R = qaug.shape
    Q = R // NSA_HPG
    S = NQ * Q
    QS = min(FLASH_SUBTILES, NQ)
    assert WINDOW % Q == 0 and vst.shape[4] == Q and NQ % QS == 0 and QS % 2 == 0
    per_b = lambda shape: pl.BlockSpec((1,) + shape, lambda b, i: (b,) + (0,) * len(shape))
    return pl.pallas_call(
        _flash_kernel,
        grid=(B, NQ // QS),
        in_specs=[pl.BlockSpec((1, NSA_GROUPS, QS, LANES, R), lambda b, i: (b, 0, i, 0, 0)),
                  per_b((NSA_GROUPS, S, LANES)), per_b(vst.shape[1:]), per_b((S, LANES)), per_b(vwt.shape[1:]),
                  pl.BlockSpec((1, QS * Q, LANES), lambda b, i: (b, i, 0)),
                  pl.BlockSpec((1, QS, NSA_QW, Q), lambda b, i: (b, i, 0, 0))],
        out_specs=pl.BlockSpec((1, QS * Q, NSA_QW), lambda b, i: (b, i, 0)),
        out_shape=jax.ShapeDtypeStruct((B, S, NSA_QW), BF16),
        scratch_shapes=[pltpu.VMEM((2, QS, NSA_GROUPS, 1, R), F32),
                        pltpu.VMEM((2, QS, NSA_GROUPS, VAL_ROWS, R), F32),
                        pltpu.VMEM((QS, NSA_GROUPS, LANES, R), BF16), pltpu.VMEM((2, Q, Q), F32),
                        pltpu.VMEM((2, QS * NSA_HEADS, Q, Q), F32)],
        compiler_params=_params(("arbitrary", "arbitrary")),
        name="flash",
    )(qaug, kaug, vst, kw, vwt, gates, ocmp)


def _retention_kernel(rq_ref, rk_ref, rv_ref, dmat_ref, qdec_ref, kdec_ref, cdec_ref, nw_ref, o_ref, st_scr):
    C = rq_ref.shape[1]

    @pl.when(pl.program_id(1) == 0)
    def _():
        st_scr[...] = jnp.zeros_like(st_scr)

    low = lax.broadcasted_iota(jnp.int32, (C, LANES), 1) < HALF
    same_head = ((lax.broadcasted_iota(jnp.int32, (LANES, LANES), 0) < HALF) ==
                 (lax.broadcasted_iota(jnp.int32, (LANES, LANES), 1) < HALF))
    for p in range(RET_HEADS // 2):
        sl = slice(p * LANES, (p + 1) * LANES)
        q2, k2, v2 = rq_ref[0, :, sl], rk_ref[0, :, sl], rv_ref[0, :, sl]
        q2f = q2.astype(F32)
        qe = jnp.where(low, q2f, 0.0).astype(BF16)
        qo = jnp.where(low, 0.0, q2f).astype(BF16)
        ie = (_dot_nt(qe, k2) * dmat_ref[2 * p]).astype(BF16)
        io = (_dot_nt(qo, k2) * dmat_ref[2 * p + 1]).astype(BF16)
        intra = jnp.where(low, _dot(ie, v2), _dot(io, v2))
        st = st_scr[p]
        o = intra + _dot(q2, st.astype(BF16)) * qdec_ref[:, sl]
        kd = (k2.astype(F32) * kdec_ref[:, sl]).astype(BF16)
        st_scr[p] = st * cdec_ref[p:p + 1, :] + jnp.where(same_head, _dot_tn(kd, v2), 0.0)
        s_lo = jnp.sum(jnp.where(low, o, 0.0), axis=1, keepdims=True)
        s_all = jnp.sum(o, axis=1, keepdims=True)
        d = o - jnp.where(low, s_lo, s_all - s_lo) * (1.0 / RET_DV)
        d2 = d * d
        v_lo = jnp.sum(jnp.where(low, d2, 0.0), axis=1, keepdims=True)
        v_all = jnp.sum(d2, axis=1, keepdims=True)
        var = jnp.where(low, v_lo, v_all - v_lo) * (1.0 / RET_DV)
        o_ref[0, :, sl] = d * lax.rsqrt(var + EPS) * nw_ref[:, sl]


def _retention(rq, rk, rv, ret_norm_w):
    B, S, _ = rq.shape
    C = min(RET_CHUNK, S)
    H = RET_HEADS
    log_g = np.log(1.0 - 2.0 ** (-5.0 - np.arange(H, dtype=np.float64)))
    i = np.arange(C, dtype=np.float64)
    diff = i[:, None] - i[None, :]
    as_f32 = lambda a: jnp.asarray(a, F32)
    dmat = as_f32(np.where(diff >= 0, np.exp(log_g[:, None, None] * np.maximum(diff, 0.0)), 0.0))
    per_lane = lambda a: np.repeat(a, RET_DK, axis=-1)
    qdec = as_f32(per_lane(np.exp(log_g[None, :] * (i[:, None] + 1.0))))
    kdec = as_f32(per_lane(np.exp(log_g[None, :] * (C - 1.0 - i[:, None]))))
    cdec = as_f32(per_lane(np.exp(log_g * C)[None, :]).reshape(H // 2, LANES))
    tok = lambda w: pl.BlockSpec((1, C, w), lambda b, i: (b, i, 0))
    return pl.pallas_call(
        _retention_kernel,
        grid=(B, S // C),
        in_specs=[tok(RET_QW), tok(RET_QW), tok(RET_VW),
                  _full_spec((H, C, C)), _full_spec((C, RET_QW)), _full_spec((C, RET_QW)),
                  _full_spec((H // 2, LANES)), _full_spec((1, RET_VW))],
        out_specs=tok(RET_VW),
        out_shape=jax.ShapeDtypeStruct((B, S, RET_VW), F32),
        scratch_shapes=[pltpu.VMEM((H // 2, LANES, LANES), F32)],
        compiler_params=_params(("arbitrary", "arbitrary")),
        name="retention",
    )(rq, rk, rv, dmat, qdec, kdec, cdec, ret_norm_w.reshape(1, RET_VW))


def _mix_kernel(h_ref, mod_ref, nw_ref, onsa_ref, oret_ref, wrg_ref, wga_ref, wgb_ref, wn_ref, wr_ref, wo_ref, o_ref):
    x = h_ref[0]
    sh = mod_ref[0, 3:4, :]
    sc = mod_ref[0, 4:5, :]
    gt = mod_ref[0, 5:6, :]
    u = _norm_mod(x, nw_ref[...], sc, sh).astype(BF16)
    rg = _dot(u, wrg_ref[...])
    oret = (oret_ref[0] * _silu(rg)).astype(BF16)
    y_nsa = _dot(onsa_ref[0], wn_ref[...])
    y_ret = _dot(oret, wr_ref[...])
    ga = jax.nn.sigmoid(_dot(u, wga_ref[...]))
    gb = jax.nn.sigmoid(_dot(u, wgb_ref[...]))
    mixed = (ga * y_nsa + gb * y_ret).astype(BF16)
    o_ref[0] = x + gt * _dot(mixed, wo_ref[...])


def _mix(h, mod, norm_w, o_nsa, o_ret, w_in, w_nsa_up, w_ret_up, w_out, *, tm):
    B, S, D = h.shape
    o = np.cumsum([0, NSA_QW] + [NSA_KVW] * 6 + [3 * NSA_HEADS, RET_QW, RET_QW, RET_VW, RET_VW, D, D])
    wrg = w_in[:, o[11]:o[12]].astype(BF16)
    wga = w_in[:, o[12]:o[13]].astype(BF16)
    wgb = w_in[:, o[13]:o[14]].astype(BF16)
    wn, wr, wo = w_nsa_up.astype(BF16), w_ret_up.astype(BF16), w_out.astype(BF16)
    tok = lambda w: pl.BlockSpec((1, tm, w), lambda b, i: (b, i, 0))
    return pl.pallas_call(
        _mix_kernel,
        grid=(B, S // tm),
        in_specs=[tok(D), pl.BlockSpec((1, 9, D), lambda b, i: (b, 0, 0)), _full_spec((1, D)),
                  tok(NSA_QW), tok(RET_VW),
                  _full_spec(wrg.shape), _full_spec(wga.shape), _full_spec(wgb.shape),
                  _full_spec(wn.shape), _full_spec(wr.shape), _full_spec(wo.shape)],
        out_specs=tok(D),
        out_shape=jax.ShapeDtypeStruct((B, S, D), F32),
        compiler_params=_params(("arbitrary", "arbitrary")),
        name="mix",
    )(h, mod, norm_w.reshape(1, D), o_nsa, o_ret, wrg, wga, wgb, wn, wr, wo)


def kernel(x, c, ada_w, ada_b, norm1_w, ffn1_w_in, ffn1_w_out, norm2_w, w_in, cmp_k_pe, cmp_k_w1, cmp_k_w2,
           cmp_v_pe, cmp_v_w1, cmp_v_w2, ret_norm_w, w_nsa_up, w_ret_up, w_out, norm3_w, ffn2_w_in, ffn2_w_out,
           final_norm_w):
    B, S, D = x.shape
    depth = ada_w.shape[0]
    assert depth >= 1
    tm = min(512, S)
    h = x
    for l in range(depth):
        last = l == depth - 1
        mod = _modulation(c, ada_w[l], ada_b[l]).reshape(B, 9, D)
        h = _ffn(h, mod, norm1_w[l], final_norm_w, ffn1_w_in[l], ffn1_w_out[l], mod_base=0, final_norm=False, tm=tm)
        q, cv, kaug, vst, kw, vwt, gates, rq, rk, rv = _projection(h, mod, norm2_w[l], w_in[l], tm=tm)
        kvc, kvct = _compress(cv, (cmp_k_pe[l], cmp_v_pe[l]), (cmp_k_w1[l], cmp_v_w1[l]), (cmp_k_w2[l], cmp_v_w2[l]))
        qaug, ocmp = _select(q, kvc, kvct, gates)
        o_nsa = _flash(qaug, kaug, vst, kw, vwt, gates, ocmp)
        o_ret = _retention(rq, rk, rv, ret_norm_w[l])
        h = _mix(h, mod, norm2_w[l], o_nsa, o_ret, w_in[l], w_nsa_up[l], w_ret_up[l], w_out[l], tm=tm)
        h = _ffn(h, mod, norm3_w[l], final_norm_w, ffn2_w_in[l], ffn2_w_out[l], mod_base=6, final_norm=last, tm=tm)
    return h
```

```python
import functools

import numpy as np
import jax
import jax.numpy as jnp
from jax import lax
from jax.experimental import pallas as pl
from jax.experimental.pallas import tpu as pltpu

F32 = jnp.float32
BF16 = jnp.bfloat16

NSA_HEADS = 8
NSA_GROUPS = 2
NSA_HPG = NSA_HEADS // NSA_GROUPS
NSA_DH = 64
CMP_STRIDE = 16
CMP_LEN = 2 * CMP_STRIDE
CMP_HIDDEN = 128
SEL_BLOCK = 64
SEL_TOPK = 16
WINDOW = 512
FORCE_BONUS = 1e4
RET_HEADS = 8
RET_DK = 64
RET_DV = 64
ROPE_BASE = 10000.0
D_FF = 2816
EPS = 1e-6
NEG = -1e30

NSA_QW = NSA_HEADS * NSA_DH
NSA_KVW = NSA_GROUPS * NSA_DH
RET_QW = RET_HEADS * RET_DK
RET_VW = RET_HEADS * RET_DV

LANES = 128
HALF = LANES // 2
VMEM_LIMIT = 56 * 1024 * 1024
FFN_CHUNK = 256
RET_CHUNK = 256
ATT_Q = 256
VAL_ROWS = NSA_DH + 16
LOG2E = 1.4426950408889634
FLASH_SUBTILES = 2
RUN_AHEAD = 6
RANK_SPAN = 16
LOOP_AHEAD = 2


def _dot(a, b):
    return jnp.dot(a, b, preferred_element_type=F32)


def _dot_nt(a, b):
    return lax.dot_general(a, b, (((1,), (1,)), ((), ())), preferred_element_type=F32)


def _dot_tn(a, b):
    return lax.dot_general(a, b, (((0,), (0,)), ((), ())), preferred_element_type=F32)


def _silu(a):
    return a * jax.nn.sigmoid(a)


def _norm_mod(x, nw, sc, sh):
    ms = jnp.mean(x * x, axis=-1, keepdims=True)
    y = x * lax.rsqrt(ms + EPS) * nw
    return y * (1.0 + sc) + sh


def _full_spec(shape):
    zeros = (0,) * len(shape)
    return pl.BlockSpec(shape, lambda *_: zeros)


def _params(sem):
    return pltpu.CompilerParams(dimension_semantics=sem, vmem_limit_bytes=VMEM_LIMIT)


def _mod_kernel(c_ref, w_ref, b_ref, o_ref):
    c = c_ref[...]
    o_ref[...] = _dot(_silu(c).astype(BF16), w_ref[...].astype(BF16)) + b_ref[...]


def _modulation(c, ada_w, ada_b):
    B, D = c.shape
    N = ada_w.shape[1]
    tn = N // 8
    return pl.pallas_call(
        _mod_kernel,
        grid=(N // tn,),
        in_specs=[pl.BlockSpec((B, D), lambda j: (0, 0)),
                  pl.BlockSpec((D, tn), lambda j: (0, j)),
                  pl.BlockSpec((1, tn), lambda j: (0, j))],
        out_specs=pl.BlockSpec((B, tn), lambda j: (0, j)),
        out_shape=jax.ShapeDtypeStruct((B, N), F32),
        compiler_params=_params(("arbitrary",)),
        name="modulation",
    )(c, ada_w, ada_b.reshape(1, N))


def _ffn_kernel(x_ref, mod_ref, nw_ref, fw_ref, wi_ref, wo_ref, o_ref, u_scr, acc_scr,
                *, mod_base, final_norm):
    x = x_ref[0]
    sh = mod_ref[0, mod_base:mod_base + 1, :]
    sc = mod_ref[0, mod_base + 1:mod_base + 2, :]
    gt = mod_ref[0, mod_base + 2:mod_base + 3, :]
    u_scr[...] = _norm_mod(x, nw_ref[...], sc, sh).astype(BF16)
    acc_scr[...] = jnp.zeros_like(acc_scr)
    dff = wo_ref.shape[0]
    for c0 in range(0, dff, FFN_CHUNK):
        u = u_scr[...]
        a = _dot(u, wi_ref[:, c0:c0 + FFN_CHUNK])
        b = _dot(u, wi_ref[:, dff + c0:dff + c0 + FFN_CHUNK])
        acc_scr[...] += _dot((_silu(a) * b).astype(BF16), wo_ref[c0:c0 + FFN_CHUNK, :])
    y = x + 0.5 * gt * acc_scr[...]
    if final_norm:
        ms = jnp.mean(y * y, axis=-1, keepdims=True)
        y = y * lax.rsqrt(ms + EPS) * fw_ref[...]
    o_ref[0] = y


def _ffn(x, mod, norm_w, final_w, w_in, w_out, *, mod_base, final_norm, tm):
    B, S, D = x.shape
    assert w_out.shape[0] % FFN_CHUNK == 0
    wi = w_in.astype(BF16)
    wo = w_out.astype(BF16)
    kern = functools.partial(_ffn_kernel, mod_base=mod_base, final_norm=final_norm)
    return pl.pallas_call(
        kern,
        grid=(B, S // tm),
        in_specs=[pl.BlockSpec((1, tm, D), lambda b, i: (b, i, 0)),
                  pl.BlockSpec((1, 9, D), lambda b, i: (b, 0, 0)),
                  _full_spec((1, D)), _full_spec((1, D)),
                  _full_spec(wi.shape), _full_spec(wo.shape)],
        out_specs=pl.BlockSpec((1, tm, D), lambda b, i: (b, i, 0)),
        out_shape=jax.ShapeDtypeStruct((B, S, D), F32),
        scratch_shapes=[pltpu.VMEM((tm, D), BF16), pltpu.VMEM((tm, D), F32)],
        compiler_params=_params(("arbitrary", "arbitrary")),
        name="ffn_final" if final_norm else "ffn",
    )(x, mod, norm_w.reshape(1, D), final_w.reshape(1, D), wi, wo)


def _proj_kernel(h_ref, mod_ref, nw_ref, wq_ref, wkv_ref, wg_ref, wr_ref, cos_ref, sin_ref,
                 q_ref, cv_ref, kaug_ref, vst_ref, kw_ref, vwt_ref, g_ref, rq_ref, rk_ref, rv_ref,
                 u_scr, cv_scr):
    tm = h_ref.shape[1]
    x = h_ref[0]
    sh = mod_ref[0, 3:4, :]
    sc = mod_ref[0, 4:5, :]
    u_scr[...] = _norm_mod(x, nw_ref[...], sc, sh).astype(BF16)
    u = u_scr[...]

    q_ref[0] = _dot_nt(u, wq_ref[...]).astype(BF16)

    kv = _dot_nt(u, wkv_ref[...])
    for t in range(2):
        cv_scr[t] = kv[:, t * LANES:(t + 1) * LANES]
        for tok in range(CMP_STRIDE):
            rows = cv_scr[t, pl.ds(tok, tm // CMP_STRIDE, stride=CMP_STRIDE), :]
            cv_ref[t, 0, :, tok * LANES:(tok + 1) * LANES] = rows.astype(BF16)
    ks = kv[:, 2 * LANES:3 * LANES]
    lane = lax.broadcasted_iota(jnp.int32, (tm, LANES), 1)
    row = lax.broadcasted_iota(jnp.int32, (tm, LANES), 0)
    low = lane < HALF
    blk = (pl.program_id(1) * tm + row) // SEL_BLOCK
    onehot = jnp.where(lane - HALF == blk, 1.0, 0.0)
    kaug_ref[0, 0] = jnp.where(low, ks, onehot).astype(BF16)
    kaug_ref[0, 1] = jnp.where(low, pltpu.roll(ks, HALF, 1), onehot).astype(BF16)
    kw_ref[0] = kv[:, 4 * LANES:5 * LANES].astype(BF16)
    kt = vst_ref.shape[4]
    ones = jnp.ones((VAL_ROWS - NSA_DH, tm), F32)
    for ref, col in ((vst_ref, 3), (vwt_ref, 5)):
        vt = kv[:, col * LANES:(col + 1) * LANES].T
        for g in range(NSA_GROUPS):
            aug = jnp.concatenate([vt[g * HALF:(g + 1) * HALF], ones], axis=0).astype(BF16)
            for c in range(tm // kt):
                ref[0, g, c] = aug[:, c * kt:(c + 1) * kt]

    g_ref[0] = jax.nn.sigmoid(_dot_nt(u, wg_ref[...]))

    r = _dot_nt(u, wr_ref[...])
    cos = cos_ref[...]
    sin = sin_ref[...]
    first = (lane & (RET_DK // 2)) == 0
    npair = RET_QW // LANES
    for j in range(2 * npair):
        xb = r[:, j * LANES:(j + 1) * LANES]
        partner = jnp.where(first, pltpu.roll(xb, LANES - RET_DK // 2, 1), pltpu.roll(xb, RET_DK // 2, 1))
        y = (xb * cos + partner * sin).astype(BF16)
        if j < npair:
            rq_ref[0, :, j * LANES:(j + 1) * LANES] = y
        else:
            rk_ref[0, :, (j - npair) * LANES:(j - npair + 1) * LANES] = y
    rv_ref[0] = r[:, 2 * RET_QW:2 * RET_QW + RET_VW].astype(BF16)


def _projection(h, mod, norm_w, w_in, *, tm):
    B, S, D = h.shape
    o = np.cumsum([0, NSA_QW] + [NSA_KVW] * 6 + [3 * NSA_HEADS, RET_QW, RET_QW, RET_VW])
    scale_q = NSA_DH ** -0.5 * LOG2E
    scale_k = RET_DK ** -0.5
    wt = w_in.T
    wq = (wt[o[0]:o[1]] * scale_q).astype(BF16)
    wkv = wt[o[1]:o[7]].astype(BF16)
    wg = jnp.pad(wt[o[7]:o[8]], ((0, LANES - 3 * NSA_HEADS), (0, 0))).astype(BF16)
    wr = jnp.concatenate([wt[o[8]:o[9]], wt[o[9]:o[10]] * scale_k, wt[o[10]:o[11]]], axis=0).astype(BF16)
    half = RET_DK // 2
    lane = np.arange(LANES)
    inv = ROPE_BASE ** (-np.arange(half, dtype=np.float64) / half)
    ang = np.arange(S, dtype=np.float64)[:, None] * inv[lane % half][None, :]
    cos_t = jnp.asarray(np.cos(ang), F32)
    sin_t = jnp.asarray(np.where((lane % RET_DK) < half, -np.sin(ang), np.sin(ang)), F32)
    tok = lambda w: pl.BlockSpec((1, tm, w), lambda b, i: (b, i, 0))
    kt = min(ATT_Q, S)
    vt_spec = pl.BlockSpec((1, NSA_GROUPS, tm // kt, VAL_ROWS, kt), lambda b, i: (b, 0, i, 0, 0))
    vt_shape = jax.ShapeDtypeStruct((B, NSA_GROUPS, S // kt, VAL_ROWS, kt), BF16)
    outs = pl.pallas_call(
        _proj_kernel,
        grid=(B, S // tm),
        in_specs=[tok(D), pl.BlockSpec((1, 9, D), lambda b, i: (b, 0, 0)), _full_spec((1, D)),
                  _full_spec(wq.shape), _full_spec(wkv.shape), _full_spec(wg.shape), _full_spec(wr.shape),
                  pl.BlockSpec((tm, LANES), lambda b, i: (i, 0)), pl.BlockSpec((tm, LANES), lambda b, i: (i, 0))],
        out_specs=[tok(NSA_QW),
                   pl.BlockSpec((2, 1, tm // CMP_STRIDE, CMP_STRIDE * LANES), lambda b, i: (0, b, i, 0)),
                   pl.BlockSpec((1, NSA_GROUPS, tm, LANES), lambda b, i: (b, 0, i, 0)),
                   vt_spec, tok(LANES), vt_spec, tok(LANES),
                   tok(RET_QW), tok(RET_QW), tok(RET_VW)],
        out_shape=[jax.ShapeDtypeStruct((B, S, NSA_QW), BF16),
                   jax.ShapeDtypeStruct((2, B, S // CMP_STRIDE, CMP_STRIDE * LANES), BF16),
                   jax.ShapeDtypeStruct((B, NSA_GROUPS, S, LANES), BF16),
                   vt_shape,
                   jax.ShapeDtypeStruct((B, S, LANES), BF16),
                   vt_shape,
                   jax.ShapeDtypeStruct((B, S, LANES), F32),
                   jax.ShapeDtypeStruct((B, S, RET_QW), BF16),
                   jax.ShapeDtypeStruct((B, S, RET_QW), BF16),
                   jax.ShapeDtypeStruct((B, S, RET_VW), BF16)],
        scratch_shapes=[pltpu.VMEM((tm, D), BF16), pltpu.VMEM((2, tm, LANES), F32)],
        compiler_params=_params(("arbitrary", "arbitrary")),
        name="projection",
    )(h, mod, norm_w.reshape(1, D), wq, wkv, wg, wr, cos_t, sin_t)
    return outs


def _compress_kernel(ch_ref, w1c_ref, pe_ref, w1_ref, w2_ref, o_ref, ot_ref):
    ch = ch_ref[0, 0]
    nc = ch.shape[0]
    ab = _dot(ch, w1c_ref[0])
    pt = _dot(pe_ref[0], w1_ref[0])[0:1, :]
    pt2 = jnp.concatenate([pt, pt], axis=1)
    nh = NSA_GROUPS * CMP_HIDDEN
    hid = ab[:, :nh] + pltpu.roll(ab[:, nh:], nc - 1, 0) + pt2
    out = _dot(_silu(hid).astype(BF16), w2_ref[0])
    row = lax.broadcasted_iota(jnp.int32, out.shape, 0)
    out = jnp.where(row < nc - 1, out, 0.0)
    o_ref[0, 0] = out.astype(BF16)
    ot_ref[0, 0, 0:LANES, :] = out.T.astype(BF16)
    ot_ref[0, 0, LANES:, :] = jnp.ones((ot_ref.shape[2] - LANES, nc), BF16)


def _compress(cv, pe, w1, w2):
    _, B, nc, width = cv.shape
    chunks = cv
    eye = jnp.eye(NSA_GROUPS, dtype=F32)
    nh = NSA_GROUPS * CMP_HIDDEN

    def expand(w):
        w = w.reshape(CMP_STRIDE, NSA_DH, CMP_HIDDEN)
        return jnp.einsum("ldj,gh->lgdhj", w, eye).reshape(width, nh)

    half = CMP_STRIDE * NSA_DH
    w1c = jnp.stack([jnp.concatenate([expand(w[:half]), expand(w[half:])], axis=1) for w in w1]).astype(BF16)
    w2bd = jnp.stack([jnp.einsum("jd,gh->gjhd", w, eye).reshape(nh, NSA_KVW) for w in w2]).astype(BF16)
    pe8 = jnp.stack([jnp.broadcast_to(p.reshape(1, CMP_LEN * NSA_DH), (8, CMP_LEN * NSA_DH)) for p in pe]).astype(BF16)
    w1s = jnp.stack(w1).astype(BF16)
    sel = lambda shape: pl.BlockSpec((1,) + shape, lambda t, b: (t,) + (0,) * len(shape))
    return pl.pallas_call(
        _compress_kernel,
        grid=(2, B),
        in_specs=[pl.BlockSpec((1, 1, nc, width), lambda t, b: (t, b, 0, 0)),
                  sel(w1c.shape[1:]), sel(pe8.shape[1:]), sel(w1s.shape[1:]), sel(w2bd.shape[1:])],
        out_specs=[pl.BlockSpec((1, 1, nc, LANES), lambda t, b: (t, b, 0, 0)),
                   pl.BlockSpec((1, 1, LANES + 16, nc), lambda t, b: (t, b, 0, 0))],
        out_shape=[jax.ShapeDtypeStruct((2, B, nc, LANES), BF16),
                   jax.ShapeDtypeStruct((2, B, LANES + 16, nc), BF16)],
        compiler_params=_params(("arbitrary", "arbitrary")),
        name="compress",
    )(chunks, w1c, pe8, w1s, w2bd)


def _select_kernel(q_ref, kc_ref, vct_ref, g_ref, ovl_ref, qaug_ref, ocmp_ref, imp_scr, rank_scr, cap_scr, *, n_sel):
    Q = q_ref.shape[1]
    NC = kc_ref.shape[2]
    NB = ovl_ref.shape[0]
    qi = pl.program_id(1)
    q0 = qi * Q
    kcf = kc_ref[0, 0].astype(F32)
    kcr = pltpu.roll(kcf, HALF, 1)
    lowk = lax.broadcasted_iota(jnp.int32, (NC, LANES), 1) < HALF
    k_even = [jnp.where(lowk, kcf, 0.0).astype(BF16), jnp.where(lowk, kcr, 0.0).astype(BF16)]
    k_odd = [jnp.where(lowk, 0.0, kcr).astype(BF16), jnp.where(lowk, 0.0, kcf).astype(BF16)]
    vct = vct_ref[0, 0]
    gt = g_ref[0].T
    n_row = lax.broadcasted_iota(jnp.int32, (NC, Q), 0)
    t_col = q0 + lax.broadcasted_iota(jnp.int32, (NC, Q), 1)
    cap_scr[...] = jnp.where(n_row * CMP_STRIDE + (CMP_LEN - 1) <= t_col, -NEG, NEG)
    any_allowed = q0 + lax.broadcasted_iota(jnp.int32, (1, Q), 1) >= CMP_LEN - 1
    ovl = ovl_ref[...]
    j_blk = lax.broadcasted_iota(jnp.int32, (NB, Q), 0)
    t_blk = q0 + lax.broadcasted_iota(jnp.int32, (NB, Q), 1)
    cur = t_blk // SEL_BLOCK
    forced = (j_blk == 0) | (j_blk == cur) | (j_blk == cur - 1)
    valid = j_blk * SEL_BLOCK <= t_blk
    j_sub = lax.broadcasted_iota(jnp.int32, (8, Q), 0)
    per_tile = Q // SEL_BLOCK

    scores = []
    for pb in range(NSA_HEADS // 2):
        g, pp = divmod(pb, NSA_HPG // 2)
        qt = q_ref[0, :, pb * LANES:(pb + 1) * LANES].astype(F32).T.astype(BF16)
        for e, kk in enumerate((k_even[g], k_odd[g])):
            scores.append(_dot(kk, qt))
            col = (2 * pp + e) * Q
            qaug_ref[0, g, 0, 0:HALF, col:col + Q] = qt[e * HALF:(e + 1) * HALF]

    for g in range(NSA_GROUPS):
        imp = jnp.zeros((NB, Q), F32)
        for hh in range(NSA_HPG):
            h = g * NSA_HPG + hh
            s = jnp.minimum(scores[h], cap_scr[...])
            mx = jnp.max(s, axis=0, keepdims=True)
            ex = jnp.exp2(s - mx).astype(BF16)
            oa = _dot(vct, ex)
            inv = jnp.where(any_allowed, 1.0 / oa[LANES:LANES + 1], 0.0)
            ocmp_ref[0, 0, h * NSA_DH:(h + 1) * NSA_DH, :] = (gt[3 * h:3 * h + 1] * inv) * oa[g * HALF:(g + 1) * HALF]
            imp = imp + _dot(ovl, ex) * inv

        imp = jnp.where(forced, imp + FORCE_BONUS, imp)
        imp_scr[g] = jnp.where(valid, imp, -FORCE_BONUS)
    rank_scr[...] = jnp.zeros_like(rank_scr)

    for c0 in range(0, NB, RANK_SPAN):
        for b0 in range(0, NB, RANK_SPAN):
            @pl.when(max(c0, b0) // per_tile <= qi)
            def _():
                slabs = range(b0 // 8, (b0 + RANK_SPAN) // 8)
                for g in range(NSA_GROUPS):
                    vals = [imp_scr[g, 8 * sb:8 * sb + 8, :] for sb in slabs]
                    cnts = [rank_scr[g, 8 * sb:8 * sb + 8, :] for sb in slabs]
                    for jp in range(c0, c0 + RANK_SPAN):
                        r = jnp.broadcast_to(imp_scr[g, jp:jp + 1, :], (8, Q))
                        for n, sb in enumerate(slabs):
                            v = vals[n]
                            if 8 * sb > jp:
                                ahead = jnp.where(r >= v, 1, 0)
                            elif 8 * sb + 7 <= jp:
                                ahead = jnp.where(r > v, 1, 0)
                            else:
                                ahead = jnp.where(j_sub + 8 * sb > jp, jnp.where(r >= v, 1, 0), jnp.where(r > v, 1, 0))
                            cnts[n] = cnts[n] + ahead
                    for n, sb in enumerate(slabs):
                        rank_scr[g, 8 * sb:8 * sb + 8, :] = cnts[n]

    for g in range(NSA_GROUPS):
        sel = (rank_scr[g] < n_sel) & valid
        bias = jnp.where(sel, 0.0, NEG).astype(BF16)
        for hh in range(NSA_HPG):
            qaug_ref[0, g, 0, HALF:HALF + NB, hh * Q:(hh + 1) * Q] = bias


def _select(q, kc, vct, gates):
    B, S, _ = q.shape
    Q = min(ATT_Q, S)
    NC = kc.shape[2]
    nc, ns = NC - 1, S // SEL_BLOCK
    NB = LANES - HALF
    assert ns <= NB and Q % SEL_BLOCK == 0
    c_start = np.arange(nc) * CMP_STRIDE
    s_start = np.arange(ns) * SEL_BLOCK
    overlap = ((c_start[:, None] < s_start[None, :] + SEL_BLOCK) &
               (c_start[:, None] + CMP_LEN > s_start[None, :])).astype(np.float32)
    ovl = np.zeros((NB, NC), np.float32)
    ovl[:ns, :nc] = overlap.T
    kern = functools.partial(_select_kernel, n_sel=min(SEL_TOPK, ns))
    return pl.pallas_call(
        kern,
        grid=(B, S // Q),
        in_specs=[pl.BlockSpec((1, Q, NSA_QW), lambda b, i: (b, i, 0)),
                  pl.BlockSpec((1, 1, NC, LANES), lambda b, i: (0, b, 0, 0)),
                  pl.BlockSpec((1, 1) + vct.shape[2:], lambda b, i: (1, b, 0, 0)),
                  pl.BlockSpec((1, Q, LANES), lambda b, i: (b, i, 0)),
                  _full_spec((NB, NC))],
        out_specs=[pl.BlockSpec((1, NSA_GROUPS, 1, LANES, NSA_HPG * Q), lambda b, i: (b, 0, i, 0, 0)),
                   pl.BlockSpec((1, 1, NSA_QW, Q), lambda b, i: (b, i, 0, 0))],
        out_shape=[jax.ShapeDtypeStruct((B, NSA_GROUPS, S // Q, LANES, NSA_HPG * Q), BF16),
                   jax.ShapeDtypeStruct((B, S // Q, NSA_QW, Q), F32)],
        scratch_shapes=[pltpu.VMEM((NSA_GROUPS, NB, Q), F32), pltpu.VMEM((NSA_GROUPS, NB, Q), jnp.int32),
                        pltpu.VMEM((NC, Q), F32)],
        compiler_params=_params(("arbitrary", "arbitrary")),
        name="select",
    )(q, kc, vct, gates, jnp.asarray(ovl, BF16))


def _flash_kernel(qaug_ref, kaug_ref, vst_ref, kw_ref, vwt_ref, g_ref, ocmp_ref, o_ref,
                  m_scr, acc_scr, qw_scr, cap_scr, s_scr):
    QS = qaug_ref.shape[2]
    R = qaug_ref.shape[4]
    Q = R // NSA_HPG
    KT = kaug_ref.shape[2] // vst_ref.shape[2]
    WT = WINDOW // KT
    SLC, WIN = 0, 1
    CAUSAL, EDGE = 0, 1
    t0 = pl.program_id(1) * QS
    k_row = lax.broadcasted_iota(jnp.int32, (KT, Q), 0)
    i_col = lax.broadcasted_iota(jnp.int32, (KT, Q), 1)
    big = -NEG
    cap_scr[CAUSAL] = jnp.where(k_row <= i_col, big, NEG)
    cap_scr[EDGE] = jnp.where(k_row > i_col, big, NEG)
    m_scr[...] = jnp.full_like(m_scr, NEG)
    acc_scr[...] = jnp.zeros_like(acc_scr)
    zeros = jnp.zeros((HALF, R), BF16)
    for qs in range(QS):
        qw_scr[qs, 0] = jnp.concatenate([qaug_ref[0, 0, qs, 0:HALF, :], zeros], axis=0)
        qw_scr[qs, 1] = jnp.concatenate([zeros, qaug_ref[0, 1, qs, 0:HALF, :]], axis=0)

    def scores(ch):
        br, qs, g, hh, kj, _ = ch
        off = pl.multiple_of(kj * KT, KT)
        if br == SLC:
            return _dot(kaug_ref[0, g, pl.ds(off, KT), :], qaug_ref[0, g, qs, :, hh * Q:(hh + 1) * Q])
        return _dot(kw_ref[0, pl.ds(off, KT), :], qw_scr[qs, g, :, hh * Q:(hh + 1) * Q])

    def absorb(ch, s):
        br, qs, g, hh, kj, cap = ch
        cols = slice(hh * Q, (hh + 1) * Q)
        if cap is not None:
            s = jnp.minimum(s, cap_scr[cap])
        m_prev = m_scr[br, qs, g, :, cols]
        m_new = jnp.maximum(m_prev, jnp.max(s, axis=0, keepdims=True))
        p = jnp.exp2(s - m_new)
        alpha = jnp.exp2(m_prev - m_new)
        vt = vst_ref[0, g, kj] if br == SLC else vwt_ref[0, g, kj]
        acc_scr[br, qs, g, :, cols] = alpha * acc_scr[br, qs, g, :, cols] + _dot(vt, p.astype(BF16))
        m_scr[br, qs, g, :, cols] = m_new

    NSH = QS * NSA_HEADS
    RING = 2 * NSH

    def run(chains, ready=()):
        order = list(ready) + list(chains)
        assert len(ready) + RUN_AHEAD < RING
        for t in range(-RUN_AHEAD, len(order)):
            n = t + RUN_AHEAD
            if len(ready) <= n < len(order):
                s_scr[(n % RING) // NSH, n % NSH] = scores(order[n])
            if t >= 0:
                absorb(order[t], s_scr[(t % RING) // NSH, t % NSH])

    def tile(br, qs, kj, cap):
        return [(br, qs, g, hh, kj, cap) for g in range(NSA_GROUPS) for hh in range(NSA_HPG)]

    def shared(kj, first_cap=None):
        return [ch for qs in range(QS) for ch in tile(SLC, qs, kj, first_cap if qs == 0 else None)]

    def issue(kj, slot, n):
        s_scr[slot, n] = scores(shared(kj)[n])

    for n in range(NSH):
        issue(0, 0, n)

    def slc_body(j, carry):
        produce = [(2 * j + 1, 1, n) for n in range(NSH)] + [(2 * j + 2, 0, n) for n in range(NSH)]
        absorbs = [(2 * j, 0, n) for n in range(NSH)] + [(2 * j + 1, 1, n) for n in range(NSH)]
        for t in range(-LOOP_AHEAD, len(absorbs)):
            if t + LOOP_AHEAD < len(produce):
                issue(*produce[t + LOOP_AHEAD])
            if t >= 0:
                kj, slot, n = absorbs[t]
                absorb(shared(kj)[n], s_scr[slot, n])
        return carry

    lax.fori_loop(0, t0 // 2, slc_body, 0)
    ready = shared(t0, CAUSAL)

    always = []
    later = {}
    for qs in range(QS):
        for kk in range(1, qs):
            always += tile(SLC, qs, t0 + kk, None)
        if qs > 0:
            always += tile(SLC, qs, t0 + qs, CAUSAL)
        always += tile(WIN, qs, t0 + qs, CAUSAL)
        for back in range(1, WT + 1):
            chains = tile(WIN, qs, t0 + qs - back, EDGE if back == WT else None)
            if back <= qs:
                always += chains
            else:
                later.setdefault(-(-(back - qs) // QS) * QS, []).extend(chains)
    run(always, ready)
    for need, chains in sorted(later.items()):
        @pl.when(t0 >= need)
        def _():
            run(chains)

    for qs in range(QS):
        gt = g_ref[0, qs * Q:(qs + 1) * Q, :].T
        for pb in range(NSA_HEADS // 2):
            halves = []
            for e in range(2):
                h = 2 * pb + e
                g, hh = divmod(h, NSA_HPG)
                cols = slice(hh * Q, (hh + 1) * Q)
                o = ocmp_ref[0, qs, h * NSA_DH:(h + 1) * NSA_DH, :]
                for br in (SLC, WIN):
                    scale = gt[3 * h + 1 + br:3 * h + 2 + br] / acc_scr[br, qs, g, NSA_DH:NSA_DH + 1, cols]
                    o = o + scale * acc_scr[br, qs, g, 0:NSA_DH, cols]
                halves.append(o)
            o_ref[0, qs * Q:(qs + 1) * Q, pb * LANES:(pb + 1) * LANES] = jnp.concatenate(halves, axis=0).T.astype(BF16)


def _flash(qaug, kaug, vst, kw, vwt, gates, ocmp):
    B, _, NQ, _, R = qaug.shape
    Q = R // NSA_HPG
    S = NQ * Q
    QS = min(FLASH_SUBTILES, NQ)
    assert WINDOW % Q == 0 and vst.shape[4] == Q and NQ % QS == 0 and QS % 2 == 0
    per_b = lambda shape: pl.BlockSpec((1,) + shape, lambda b, i: (b,) + (0,) * len(shape))
    return pl.pallas_call(
        _flash_kernel,
        grid=(B, NQ // QS),
        in_specs=[pl.BlockSpec((1, NSA_GROUPS, QS, LANES, R), lambda b, i: (b, 0, i, 0, 0)),
                  per_b((NSA_GROUPS, S, LANES)), per_b(vst.shape[1:]), per_b((S, LANES)), per_b(vwt.shape[1:]),
                  pl.BlockSpec((1, QS * Q, LANES), lambda b, i: (b, i, 0)),
                  pl.BlockSpec((1, QS, NSA_QW, Q), lambda b, i: (b, i, 0, 0))],
        out_specs=pl.BlockSpec((1, QS * Q, NSA_QW), lambda b, i: (b, i, 0)),
        out_shape=jax.ShapeDtypeStruct((B, S, NSA_QW), BF16),
        scratch_shapes=[pltpu.VMEM((2, QS, NSA_GROUPS, 1, R), F32),
                        pltpu.VMEM((2, QS, NSA_GROUPS, VAL_ROWS, R), F32),
                        pltpu.VMEM((QS, NSA_GROUPS, LANES, R), BF16), pltpu.VMEM((2, Q, Q), F32),
                        pltpu.VMEM((2, QS * NSA_HEADS, Q, Q), F32)],
        compiler_params=_params(("arbitrary", "arbitrary")),
        name="flash",
    )(qaug, kaug, vst, kw, vwt, gates, ocmp)


def _retention_kernel(rq_ref, rk_ref, rv_ref, dmat_ref, qdec_ref, kdec_ref, cdec_ref, nw_ref, o_ref, st_scr):
    C = rq_ref.shape[1]

    @pl.when(pl.program_id(1) == 0)
    def _():
        st_scr[...] = jnp.zeros_like(st_scr)

    low = lax.broadcasted_iota(jnp.int32, (C, LANES), 1) < HALF
    same_head = ((lax.broadcasted_iota(jnp.int32, (LANES, LANES), 0) < HALF) ==
                 (lax.broadcasted_iota(jnp.int32, (LANES, LANES), 1) < HALF))
    for p in range(RET_HEADS // 2):
        sl = slice(p * LANES, (p + 1) * LANES)
        q2, k2, v2 = rq_ref[0, :, sl], rk_ref[0, :, sl], rv_ref[0, :, sl]
        q2f = q2.astype(F32)
        qe = jnp.where(low, q2f, 0.0).astype(BF16)
        qo = jnp.where(low, 0.0, q2f).astype(BF16)
        ie = (_dot_nt(qe, k2) * dmat_ref[2 * p]).astype(BF16)
        io = (_dot_nt(qo, k2) * dmat_ref[2 * p + 1]).astype(BF16)
        intra = jnp.where(low, _dot(ie, v2), _dot(io, v2))
        st = st_scr[p]
        o = intra + _dot(q2, st.astype(BF16)) * qdec_ref[:, sl]
        kd = (k2.astype(F32) * kdec_ref[:, sl]).astype(BF16)
        st_scr[p] = st * cdec_ref[p:p + 1, :] + jnp.where(same_head, _dot_tn(kd, v2), 0.0)
        s_lo = jnp.sum(jnp.where(low, o, 0.0), axis=1, keepdims=True)
        s_all = jnp.sum(o, axis=1, keepdims=True)
        d = o - jnp.where(low, s_lo, s_all - s_lo) * (1.0 / RET_DV)
        d2 = d * d
        v_lo = jnp.sum(jnp.where(low, d2, 0.0), axis=1, keepdims=True)
        v_all = jnp.sum(d2, axis=1, keepdims=True)
        var = jnp.where(low, v_lo, v_all - v_lo) * (1.0 / RET_DV)
        o_ref[0, :, sl] = d * lax.rsqrt(var + EPS) * nw_ref[:, sl]


def _retention(rq, rk, rv, ret_norm_w):
    B, S, _ = rq.shape
    C = min(RET_CHUNK, S)
    H = RET_HEADS
    log_g = np.log(1.0 - 2.0 ** (-5.0 - np.arange(H, dtype=np.float64)))
    i = np.arange(C, dtype=np.float64)
    diff = i[:, None] - i[None, :]
    as_f32 = lambda a: jnp.asarray(a, F32)
    dmat = as_f32(np.where(diff >= 0, np.exp(log_g[:, None, None] * np.maximum(diff, 0.0)), 0.0))
    per_lane = lambda a: np.repeat(a, RET_DK, axis=-1)
    qdec = as_f32(per_lane(np.exp(log_g[None, :] * (i[:, None] + 1.0))))
    kdec = as_f32(per_lane(np.exp(log_g[None, :] * (C - 1.0 - i[:, None]))))
    cdec = as_f32(per_lane(np.exp(log_g * C)[None, :]).reshape(H // 2, LANES))
    tok = lambda w: pl.BlockSpec((1, C, w), lambda b, i: (b, i, 0))
    return pl.pallas_call(
        _retention_kernel,
        grid=(B, S // C),
        in_specs=[tok(RET_QW), tok(RET_QW), tok(RET_VW),
                  _full_spec((H, C, C)), _full_spec((C, RET_QW)), _full_spec((C, RET_QW)),
                  _full_spec((H // 2, LANES)), _full_spec((1, RET_VW))],
        out_specs=tok(RET_VW),
        out_shape=jax.ShapeDtypeStruct((B, S, RET_VW), F32),
        scratch_shapes=[pltpu.VMEM((H // 2, LANES, LANES), F32)],
        compiler_params=_params(("arbitrary", "arbitrary")),
        name="retention",
    )(rq, rk, rv, dmat, qdec, kdec, cdec, ret_norm_w.reshape(1, RET_VW))


def _mix_kernel(h_ref, mod_ref, nw_ref, onsa_ref, oret_ref, wrg_ref, wga_ref, wgb_ref, wn_ref, wr_ref, wo_ref, o_ref):
    x = h_ref[0]
    sh = mod_ref[0, 3:4, :]
    sc = mod_ref[0, 4:5, :]
    gt = mod_ref[0, 5:6, :]
    u = _norm_mod(x, nw_ref[...], sc, sh).astype(BF16)
    rg = _dot_nt(u, wrg_ref[...])
    oret = (oret_ref[0] * _silu(rg)).astype(BF16)
    y_nsa = _dot(onsa_ref[0], wn_ref[...])
    y_ret = _dot(oret, wr_ref[...])
    ga = jax.nn.sigmoid(_dot_nt(u, wga_ref[...]))
    gb = jax.nn.sigmoid(_dot_nt(u, wgb_ref[...]))
    mixed = (ga * y_nsa + gb * y_ret).astype(BF16)
    o_ref[0] = x + gt * _dot(mixed, wo_ref[...])


def _mix(h, mod, norm_w, o_nsa, o_ret, w_in, w_nsa_up, w_ret_up, w_out, *, tm):
    B, S, D = h.shape
    o = np.cumsum([0, NSA_QW] + [NSA_KVW] * 6 + [3 * NSA_HEADS, RET_QW, RET_QW, RET_VW, RET_VW, D, D])
    wt = w_in.T
    wrg = wt[o[11]:o[12]].astype(BF16)
    wga = wt[o[12]:o[13]].astype(BF16)
    wgb = wt[o[13]:o[14]].astype(BF16)
    wn, wr, wo = w_nsa_up.astype(BF16), w_ret_up.astype(BF16), w_out.astype(BF16)
    tok = lambda w: pl.BlockSpec((1, tm, w), lambda b, i: (b, i, 0))
    return pl.pallas_call(
        _mix_kernel,
        grid=(B, S // tm),
        in_specs=[tok(D), pl.BlockSpec((1, 9, D), lambda b, i: (b, 0, 0)), _full_spec((1, D)),
                  tok(NSA_QW), tok(RET_VW),
                  _full_spec(wrg.shape), _full_spec(wga.shape), _full_spec(wgb.shape),
                  _full_spec(wn.shape), _full_spec(wr.shape), _full_spec(wo.shape)],
        out_specs=tok(D),
        out_shape=jax.ShapeDtypeStruct((B, S, D), F32),
        compiler_params=_params(("arbitrary", "arbitrary")),
        name="mix",
    )(h, mod, norm_w.reshape(1, D), o_nsa, o_ret, wrg, wga, wgb, wn, wr, wo)


def kernel(x, c, ada_w, ada_b, norm1_w, ffn1_w_in, ffn1_w_out, norm2_w, w_in, cmp_k_pe, cmp_k_w1, cmp_k_w2,
           cmp_v_pe, cmp_v_w1, cmp_v_w2, ret_norm_w, w_nsa_up, w_ret_up, w_out, norm3_w, ffn2_w_in, ffn2_w_out,
           final_norm_w):
    B, S, D = x.shape
    depth = ada_w.shape[0]
    assert depth >= 1
    tm = min(512, S)
    h = x
    for l in range(depth):
        last = l == depth - 1
        mod = _modulation(c, ada_w[l], ada_b[l]).reshape(B, 9, D)
        h = _ffn(h, mod, norm1_w[l], final_norm_w, ffn1_w_in[l], ffn1_w_out[l], mod_base=0, final_norm=False, tm=tm)
        q, cv, kaug, vst, kw, vwt, gates, rq, rk, rv = _projection(h, mod, norm2_w[l], w_in[l], tm=tm)
        kvc, kvct = _compress(cv, (cmp_k_pe[l], cmp_v_pe[l]), (cmp_k_w1[l], cmp_v_w1[l]), (cmp_k_w2[l], cmp_v_w2[l]))
        qaug, ocmp = _select(q, kvc, kvct, gates)
        o_nsa = _flash(qaug, kaug, vst, kw, vwt, gates, ocmp)
        o_ret = _retention(rq, rk, rv, ret_norm_w[l])
        h = _mix(h, mod, norm2_w[l], o_nsa, o_ret, w_in[l], w_nsa_up[l], w_ret_up[l], w_out[l], tm=tm)
        h = _ffn(h, mod, norm3_w[l], final_norm_w, ffn2_w_in[l], ffn2_w_out[l], mod_base=6, final_norm=last, tm=tm)
    return h
```

```python
import functools

import numpy as np
import jax
import jax.numpy as jnp
from jax import lax
from jax.experimental import pallas as pl
from jax.experimental.pallas import tpu as pltpu

F32 = jnp.float32
BF16 = jnp.bfloat16

NSA_HEADS = 8
NSA_GROUPS = 2
NSA_HPG = NSA_HEADS // NSA_GROUPS
NSA_DH = 64
CMP_STRIDE = 16
CMP_LEN = 2 * CMP_STRIDE
CMP_HIDDEN = 128
SEL_BLOCK = 64
SEL_TOPK = 16
WINDOW = 512
FORCE_BONUS = 1e4
RET_HEADS = 8
RET_DK = 64
RET_DV = 64
ROPE_BASE = 10000.0
D_FF = 2816
EPS = 1e-6
NEG = -1e30

NSA_QW = NSA_HEADS * NSA_DH
NSA_KVW = NSA_GROUPS * NSA_DH
RET_QW = RET_HEADS * RET_DK
RET_VW = RET_HEADS * RET_DV

LANES = 128
HALF = LANES // 2
VMEM_LIMIT = 56 * 1024 * 1024
FFN_CHUNK = 256
TOKEN_TILE = 512
FFN_TOKEN_TILE = 1024
RET_CHUNK = 256
RET_CHUNKS_PER_STEP = 2
ATT_Q = 256
VAL_ROWS = NSA_DH + 16
LOG2E = 1.4426950408889634
FLASH_SUBTILES = 2
RUN_AHEAD = 6
RANK_SPAN = 16
LOOP_AHEAD = 2


def _dot(a, b):
    return jnp.dot(a, b, preferred_element_type=F32)


def _dot_nt(a, b):
    return lax.dot_general(a, b, (((1,), (1,)), ((), ())), preferred_element_type=F32)


def _dot_tn(a, b):
    return lax.dot_general(a, b, (((0,), (0,)), ((), ())), preferred_element_type=F32)


def _silu(a):
    return a * jax.nn.sigmoid(a)


def _norm_mod(x, nw, sc, sh):
    ms = jnp.mean(x * x, axis=-1, keepdims=True)
    y = x * lax.rsqrt(ms + EPS) * nw
    return y * (1.0 + sc) + sh


def _full_spec(shape):
    zeros = (0,) * len(shape)
    return pl.BlockSpec(shape, lambda *_: zeros)


def _resident_spec(shape):
    zeros = (0,) * len(shape)
    return pl.BlockSpec(shape, lambda *_: zeros, pipeline_mode=pl.Buffered(1))


def _params(sem):
    return pltpu.CompilerParams(dimension_semantics=sem, vmem_limit_bytes=VMEM_LIMIT)


def _mod_kernel(c_ref, w_ref, b_ref, o_ref):
    c = c_ref[...]
    o_ref[...] = _dot(_silu(c).astype(BF16), w_ref[...].astype(BF16)) + b_ref[...]


def _modulation(c, ada_w, ada_b):
    B, D = c.shape
    N = ada_w.shape[1]
    tn = N // 8
    return pl.pallas_call(
        _mod_kernel,
        grid=(N // tn,),
        in_specs=[pl.BlockSpec((B, D), lambda j: (0, 0)),
                  pl.BlockSpec((D, tn), lambda j: (0, j)),
                  pl.BlockSpec((1, tn), lambda j: (0, j))],
        out_specs=pl.BlockSpec((B, tn), lambda j: (0, j)),
        out_shape=jax.ShapeDtypeStruct((B, N), F32),
        compiler_params=_params(("arbitrary",)),
        name="modulation",
    )(c, ada_w, ada_b.reshape(1, N))


def _ffn_kernel(x_ref, mod_ref, nw_ref, fw_ref, wi_ref, wo_ref, o_ref, u_scr, acc_scr,
                *, mod_base, final_norm):
    x = x_ref[0]
    sh = mod_ref[0, mod_base:mod_base + 1, :]
    sc = mod_ref[0, mod_base + 1:mod_base + 2, :]
    gt = mod_ref[0, mod_base + 2:mod_base + 3, :]
    u_scr[...] = _norm_mod(x, nw_ref[...], sc, sh).astype(BF16)
    acc_scr[...] = jnp.zeros_like(acc_scr)
    dff = wo_ref.shape[0]
    for c0 in range(0, dff, FFN_CHUNK):
        u = u_scr[...]
        a = _dot(u, wi_ref[:, c0:c0 + FFN_CHUNK])
        b = _dot(u, wi_ref[:, dff + c0:dff + c0 + FFN_CHUNK])
        acc_scr[...] += _dot((_silu(a) * b).astype(BF16), wo_ref[c0:c0 + FFN_CHUNK, :])
    y = x + 0.5 * gt * acc_scr[...]
    if final_norm:
        ms = jnp.mean(y * y, axis=-1, keepdims=True)
        y = y * lax.rsqrt(ms + EPS) * fw_ref[...]
    o_ref[0] = y


def _ffn(x, mod, norm_w, final_w, w_in, w_out, *, mod_base, final_norm, tm):
    B, S, D = x.shape
    assert w_out.shape[0] % FFN_CHUNK == 0
    wi = w_in.astype(BF16)
    wo = w_out.astype(BF16)
    kern = functools.partial(_ffn_kernel, mod_base=mod_base, final_norm=final_norm)
    return pl.pallas_call(
        kern,
        grid=(B, S // tm),
        in_specs=[pl.BlockSpec((1, tm, D), lambda b, i: (b, i, 0)),
                  pl.BlockSpec((1, 9, D), lambda b, i: (b, 0, 0)),
                  _full_spec((1, D)), _full_spec((1, D)),
                  _resident_spec(wi.shape), _resident_spec(wo.shape)],
        out_specs=pl.BlockSpec((1, tm, D), lambda b, i: (b, i, 0)),
        out_shape=jax.ShapeDtypeStruct((B, S, D), F32),
        scratch_shapes=[pltpu.VMEM((tm, D), BF16), pltpu.VMEM((tm, D), F32)],
        compiler_params=_params(("arbitrary", "arbitrary")),
        name="ffn_final" if final_norm else "ffn",
    )(x, mod, norm_w.reshape(1, D), final_w.reshape(1, D), wi, wo)


def _proj_kernel(h_ref, mod_ref, nw_ref, wq_ref, wkv_ref, wg_ref, wr_ref, cos_ref, sin_ref,
                 q_ref, cv_ref, kaug_ref, vst_ref, kw_ref, vwt_ref, g_ref, rq_ref, rk_ref, rv_ref,
                 u_scr, cv_scr):
    tm = h_ref.shape[1]
    x = h_ref[0]
    sh = mod_ref[0, 3:4, :]
    sc = mod_ref[0, 4:5, :]
    u_scr[...] = _norm_mod(x, nw_ref[...], sc, sh).astype(BF16)
    u = u_scr[...]

    q_ref[0] = _dot_nt(u, wq_ref[...]).astype(BF16)

    kv = _dot_nt(u, wkv_ref[...])
    for t in range(2):
        cv_scr[t] = kv[:, t * LANES:(t + 1) * LANES]
        for tok in range(CMP_STRIDE):
            rows = cv_scr[t, pl.ds(tok, tm // CMP_STRIDE, stride=CMP_STRIDE), :]
            cv_ref[t, 0, :, tok * LANES:(tok + 1) * LANES] = rows.astype(BF16)
    ks = kv[:, 2 * LANES:3 * LANES]
    lane = lax.broadcasted_iota(jnp.int32, (tm, LANES), 1)
    row = lax.broadcasted_iota(jnp.int32, (tm, LANES), 0)
    low = lane < HALF
    blk = (pl.program_id(1) * tm + row) // SEL_BLOCK
    onehot = jnp.where(lane - HALF == blk, 1.0, 0.0)
    kaug_ref[0, 0] = jnp.where(low, ks, onehot).astype(BF16)
    kaug_ref[0, 1] = jnp.where(low, pltpu.roll(ks, HALF, 1), onehot).astype(BF16)
    kw_ref[0] = kv[:, 4 * LANES:5 * LANES].astype(BF16)
    kt = vst_ref.shape[4]
    ones = jnp.ones((VAL_ROWS - NSA_DH, tm), F32)
    for ref, col in ((vst_ref, 3), (vwt_ref, 5)):
        vt = kv[:, col * LANES:(col + 1) * LANES].T
        for g in range(NSA_GROUPS):
            aug = jnp.concatenate([vt[g * HALF:(g + 1) * HALF], ones], axis=0).astype(BF16)
            for c in range(tm // kt):
                ref[0, g, c] = aug[:, c * kt:(c + 1) * kt]

    g_ref[0] = jax.nn.sigmoid(_dot_nt(u, wg_ref[...]))

    r = _dot_nt(u, wr_ref[...])
    cos = cos_ref[...]
    sin = sin_ref[...]
    first = (lane & (RET_DK // 2)) == 0
    npair = RET_QW // LANES
    for j in range(2 * npair):
        xb = r[:, j * LANES:(j + 1) * LANES]
        partner = jnp.where(first, pltpu.roll(xb, LANES - RET_DK // 2, 1), pltpu.roll(xb, RET_DK // 2, 1))
        y = (xb * cos + partner * sin).astype(BF16)
        if j < npair:
            rq_ref[0, :, j * LANES:(j + 1) * LANES] = y
        else:
            rk_ref[0, :, (j - npair) * LANES:(j - npair + 1) * LANES] = y
    rv_ref[0] = r[:, 2 * RET_QW:2 * RET_QW + RET_VW].astype(BF16)


def _projection(h, mod, norm_w, w_in, *, tm):
    B, S, D = h.shape
    o = np.cumsum([0, NSA_QW] + [NSA_KVW] * 6 + [3 * NSA_HEADS, RET_QW, RET_QW, RET_VW])
    scale_q = NSA_DH ** -0.5 * LOG2E
    scale_k = RET_DK ** -0.5
    wt = w_in.T
    wq = (wt[o[0]:o[1]] * scale_q).astype(BF16)
    wkv = wt[o[1]:o[7]].astype(BF16)
    wg = jnp.pad(wt[o[7]:o[8]], ((0, LANES - 3 * NSA_HEADS), (0, 0))).astype(BF16)
    wr = jnp.concatenate([wt[o[8]:o[9]], wt[o[9]:o[10]] * scale_k, wt[o[10]:o[11]]], axis=0).astype(BF16)
    half = RET_DK // 2
    lane = np.arange(LANES)
    inv = ROPE_BASE ** (-np.arange(half, dtype=np.float64) / half)
    ang = np.arange(S, dtype=np.float64)[:, None] * inv[lane % half][None, :]
    cos_t = jnp.asarray(np.cos(ang), F32)
    sin_t = jnp.asarray(np.where((lane % RET_DK) < half, -np.sin(ang), np.sin(ang)), F32)
    tok = lambda w: pl.BlockSpec((1, tm, w), lambda b, i: (b, i, 0))
    kt = min(ATT_Q, S)
    vt_spec = pl.BlockSpec((1, NSA_GROUPS, tm // kt, VAL_ROWS, kt), lambda b, i: (b, 0, i, 0, 0))
    vt_shape = jax.ShapeDtypeStruct((B, NSA_GROUPS, S // kt, VAL_ROWS, kt), BF16)
    outs = pl.pallas_call(
        _proj_kernel,
        grid=(B, S // tm),
        in_specs=[tok(D), pl.BlockSpec((1, 9, D), lambda b, i: (b, 0, 0)), _full_spec((1, D)),
                  _full_spec(wq.shape), _full_spec(wkv.shape), _full_spec(wg.shape), _full_spec(wr.shape),
                  pl.BlockSpec((tm, LANES), lambda b, i: (i, 0)), pl.BlockSpec((tm, LANES), lambda b, i: (i, 0))],
        out_specs=[tok(NSA_QW),
                   pl.BlockSpec((2, 1, tm // CMP_STRIDE, CMP_STRIDE * LANES), lambda b, i: (0, b, i, 0)),
                   pl.BlockSpec((1, NSA_GROUPS, tm, LANES), lambda b, i: (b, 0, i, 0)),
                   vt_spec, tok(LANES), vt_spec, tok(LANES),
                   tok(RET_QW), tok(RET_QW), tok(RET_VW)],
        out_shape=[jax.ShapeDtypeStruct((B, S, NSA_QW), BF16),
                   jax.ShapeDtypeStruct((2, B, S // CMP_STRIDE, CMP_STRIDE * LANES), BF16),
                   jax.ShapeDtypeStruct((B, NSA_GROUPS, S, LANES), BF16),
                   vt_shape,
                   jax.ShapeDtypeStruct((B, S, LANES), BF16),
                   vt_shape,
                   jax.ShapeDtypeStruct((B, S, LANES), F32),
                   jax.ShapeDtypeStruct((B, S, RET_QW), BF16),
                   jax.ShapeDtypeStruct((B, S, RET_QW), BF16),
                   jax.ShapeDtypeStruct((B, S, RET_VW), BF16)],
        scratch_shapes=[pltpu.VMEM((tm, D), BF16), pltpu.VMEM((2, tm, LANES), F32)],
        compiler_params=_params(("arbitrary", "arbitrary")),
        name="projection",
    )(h, mod, norm_w.reshape(1, D), wq, wkv, wg, wr, cos_t, sin_t)
    return outs


def _compress_kernel(ch_ref, w1c_ref, pe_ref, w1_ref, w2_ref, o_ref, ot_ref):
    ch = ch_ref[0, 0]
    nc = ch.shape[0]
    ab = _dot(ch, w1c_ref[0])
    pt = _dot(pe_ref[0], w1_ref[0])[0:1, :]
    pt2 = jnp.concatenate([pt, pt], axis=1)
    nh = NSA_GROUPS * CMP_HIDDEN
    hid = ab[:, :nh] + pltpu.roll(ab[:, nh:], nc - 1, 0) + pt2
    out = _dot(_silu(hid).astype(BF16), w2_ref[0])
    row = lax.broadcasted_iota(jnp.int32, out.shape, 0)
    out = jnp.where(row < nc - 1, out, 0.0)
    o_ref[0, 0] = out.astype(BF16)
    ot_ref[0, 0, 0:LANES, :] = out.T.astype(BF16)
    ot_ref[0, 0, LANES:, :] = jnp.ones((ot_ref.shape[2] - LANES, nc), BF16)


def _compress(cv, pe, w1, w2):
    _, B, nc, width = cv.shape
    chunks = cv
    eye = jnp.eye(NSA_GROUPS, dtype=F32)
    nh = NSA_GROUPS * CMP_HIDDEN

    def expand(w):
        w = w.reshape(CMP_STRIDE, NSA_DH, CMP_HIDDEN)
        return jnp.einsum("ldj,gh->lgdhj", w, eye).reshape(width, nh)

    half = CMP_STRIDE * NSA_DH
    w1c = jnp.stack([jnp.concatenate([expand(w[:half]), expand(w[half:])], axis=1) for w in w1]).astype(BF16)
    w2bd = jnp.stack([jnp.einsum("jd,gh->gjhd", w, eye).reshape(nh, NSA_KVW) for w in w2]).astype(BF16)
    pe8 = jnp.stack([jnp.broadcast_to(p.reshape(1, CMP_LEN * NSA_DH), (8, CMP_LEN * NSA_DH)) for p in pe]).astype(BF16)
    w1s = jnp.stack(w1).astype(BF16)
    sel = lambda shape: pl.BlockSpec((1,) + shape, lambda t, b: (t,) + (0,) * len(shape))
    return pl.pallas_call(
        _compress_kernel,
        grid=(2, B),
        in_specs=[pl.BlockSpec((1, 1, nc, width), lambda t, b: (t, b, 0, 0)),
                  sel(w1c.shape[1:]), sel(pe8.shape[1:]), sel(w1s.shape[1:]), sel(w2bd.shape[1:])],
        out_specs=[pl.BlockSpec((1, 1, nc, LANES), lambda t, b: (t, b, 0, 0)),
                   pl.BlockSpec((1, 1, LANES + 16, nc), lambda t, b: (t, b, 0, 0))],
        out_shape=[jax.ShapeDtypeStruct((2, B, nc, LANES), BF16),
                   jax.ShapeDtypeStruct((2, B, LANES + 16, nc), BF16)],
        compiler_params=_params(("arbitrary", "arbitrary")),
        name="compress",
    )(chunks, w1c, pe8, w1s, w2bd)


def _select_kernel(q_ref, kc_ref, vct_ref, g_ref, ovl_ref, qaug_ref, ocmp_ref, imp_scr, rank_scr, cap_scr, *, n_sel):
    Q = q_ref.shape[1]
    NC = kc_ref.shape[2]
    NB = ovl_ref.shape[0]
    qi = pl.program_id(1)
    q0 = qi * Q
    kcf = kc_ref[0, 0].astype(F32)
    kcr = pltpu.roll(kcf, HALF, 1)
    lowk = lax.broadcasted_iota(jnp.int32, (NC, LANES), 1) < HALF
    k_even = [jnp.where(lowk, kcf, 0.0).astype(BF16), jnp.where(lowk, kcr, 0.0).astype(BF16)]
    k_odd = [jnp.where(lowk, 0.0, kcr).astype(BF16), jnp.where(lowk, 0.0, kcf).astype(BF16)]
    vct = vct_ref[0, 0]
    gt = g_ref[0].T
    n_row = lax.broadcasted_iota(jnp.int32, (NC, Q), 0)
    t_col = q0 + lax.broadcasted_iota(jnp.int32, (NC, Q), 1)
    cap_scr[...] = jnp.where(n_row * CMP_STRIDE + (CMP_LEN - 1) <= t_col, -NEG, NEG)
    any_allowed = q0 + lax.broadcasted_iota(jnp.int32, (1, Q), 1) >= CMP_LEN - 1
    ovl = ovl_ref[...]
    j_blk = lax.broadcasted_iota(jnp.int32, (NB, Q), 0)
    t_blk = q0 + lax.broadcasted_iota(jnp.int32, (NB, Q), 1)
    cur = t_blk // SEL_BLOCK
    forced = (j_blk == 0) | (j_blk == cur) | (j_blk == cur - 1)
    valid = j_blk * SEL_BLOCK <= t_blk
    j_sub = lax.broadcasted_iota(jnp.int32, (8, Q), 0)
    per_tile = Q // SEL_BLOCK

    scores = []
    for pb in range(NSA_HEADS // 2):
        g, pp = divmod(pb, NSA_HPG // 2)
        qt = q_ref[0, :, pb * LANES:(pb + 1) * LANES].astype(F32).T.astype(BF16)
        for e, kk in enumerate((k_even[g], k_odd[g])):
            scores.append(_dot(kk, qt))
            col = (2 * pp + e) * Q
            qaug_ref[0, g, 0, 0:HALF, col:col + Q] = qt[e * HALF:(e + 1) * HALF]

    for g in range(NSA_GROUPS):
        imp = jnp.zeros((NB, Q), F32)
        for hh in range(NSA_HPG):
            h = g * NSA_HPG + hh
            s = jnp.minimum(scores[h], cap_scr[...])
            mx = jnp.max(s, axis=0, keepdims=True)
            ex = jnp.exp2(s - mx).astype(BF16)
            oa = _dot(vct, ex)
            inv = jnp.where(any_allowed, 1.0 / oa[LANES:LANES + 1], 0.0)
            ocmp_ref[0, 0, h * NSA_DH:(h + 1) * NSA_DH, :] = (gt[3 * h:3 * h + 1] * inv) * oa[g * HALF:(g + 1) * HALF]
            imp = imp + _dot(ovl, ex) * inv

        imp = jnp.where(forced, imp + FORCE_BONUS, imp)
        imp_scr[g] = jnp.where(valid, imp, -FORCE_BONUS)
    rank_scr[...] = jnp.zeros_like(rank_scr)

    for c0 in range(0, NB, RANK_SPAN):
        for b0 in range(0, NB, RANK_SPAN):
            @pl.when(max(c0, b0) // per_tile <= qi)
            def _():
                slabs = range(b0 // 8, (b0 + RANK_SPAN) // 8)
                for g in range(NSA_GROUPS):
                    vals = [imp_scr[g, 8 * sb:8 * sb + 8, :] for sb in slabs]
                    cnts = [rank_scr[g, 8 * sb:8 * sb + 8, :] for sb in slabs]
                    for jp in range(c0, c0 + RANK_SPAN):
                        r = jnp.broadcast_to(imp_scr[g, jp:jp + 1, :], (8, Q))
                        for n, sb in enumerate(slabs):
                            v = vals[n]
                            if 8 * sb > jp:
                                ahead = jnp.where(r >= v, 1, 0)
                            elif 8 * sb + 7 <= jp:
                                ahead = jnp.where(r > v, 1, 0)
                            else:
                                ahead = jnp.where(j_sub + 8 * sb > jp, jnp.where(r >= v, 1, 0), jnp.where(r > v, 1, 0))
                            cnts[n] = cnts[n] + ahead
                    for n, sb in enumerate(slabs):
                        rank_scr[g, 8 * sb:8 * sb + 8, :] = cnts[n]

    for g in range(NSA_GROUPS):
        sel = (rank_scr[g] < n_sel) & valid
        bias = jnp.where(sel, 0.0, NEG).astype(BF16)
        for hh in range(NSA_HPG):
            qaug_ref[0, g, 0, HALF:HALF + NB, hh * Q:(hh + 1) * Q] = bias


def _select(q, kc, vct, gates):
    B, S, _ = q.shape
    Q = min(ATT_Q, S)
    NC = kc.shape[2]
    nc, ns = NC - 1, S // SEL_BLOCK
    NB = LANES - HALF
    assert ns <= NB and Q % SEL_BLOCK == 0
    c_start = np.arange(nc) * CMP_STRIDE
    s_start = np.arange(ns) * SEL_BLOCK
    overlap = ((c_start[:, None] < s_start[None, :] + SEL_BLOCK) &
               (c_start[:, None] + CMP_LEN > s_start[None, :])).astype(np.float32)
    ovl = np.zeros((NB, NC), np.float32)
    ovl[:ns, :nc] = overlap.T
    kern = functools.partial(_select_kernel, n_sel=min(SEL_TOPK, ns))
    return pl.pallas_call(
        kern,
        grid=(B, S // Q),
        in_specs=[pl.BlockSpec((1, Q, NSA_QW), lambda b, i: (b, i, 0)),
                  pl.BlockSpec((1, 1, NC, LANES), lambda b, i: (0, b, 0, 0)),
                  pl.BlockSpec((1, 1) + vct.shape[2:], lambda b, i: (1, b, 0, 0)),
                  pl.BlockSpec((1, Q, LANES), lambda b, i: (b, i, 0)),
                  _full_spec((NB, NC))],
        out_specs=[pl.BlockSpec((1, NSA_GROUPS, 1, LANES, NSA_HPG * Q), lambda b, i: (b, 0, i, 0, 0)),
                   pl.BlockSpec((1, 1, NSA_QW, Q), lambda b, i: (b, i, 0, 0))],
        out_shape=[jax.ShapeDtypeStruct((B, NSA_GROUPS, S // Q, LANES, NSA_HPG * Q), BF16),
                   jax.ShapeDtypeStruct((B, S // Q, NSA_QW, Q), F32)],
        scratch_shapes=[pltpu.VMEM((NSA_GROUPS, NB, Q), F32), pltpu.VMEM((NSA_GROUPS, NB, Q), jnp.int32),
                        pltpu.VMEM((NC, Q), F32)],
        compiler_params=_params(("arbitrary", "arbitrary")),
        name="select",
    )(q, kc, vct, gates, jnp.asarray(ovl, BF16))


def _flash_kernel(qaug_ref, kaug_ref, vst_ref, kw_ref, vwt_ref, g_ref, ocmp_ref, o_ref,
                  m_scr, acc_scr, qw_scr, cap_scr, s_scr):
    QS = qaug_ref.shape[2]
    R = qaug_ref.shape[4]
    Q = R // NSA_HPG
    KT = kaug_ref.shape[2] // vst_ref.shape[2]
    WT = WINDOW // KT
    SLC, WIN = 0, 1
    CAUSAL, EDGE = 0, 1
    t0 = pl.program_id(1) * QS
    k_row = lax.broadcasted_iota(jnp.int32, (KT, Q), 0)
    i_col = lax.broadcasted_iota(jnp.int32, (KT, Q), 1)
    big = -NEG
    cap_scr[CAUSAL] = jnp.where(k_row <= i_col, big, NEG)
    cap_scr[EDGE] = jnp.where(k_row > i_col, big, NEG)
    m_scr[...] = jnp.full_like(m_scr, NEG)
    acc_scr[...] = jnp.zeros_like(acc_scr)
    zeros = jnp.zeros((HALF, R), BF16)
    for qs in range(QS):
        qw_scr[qs, 0] = jnp.concatenate([qaug_ref[0, 0, qs, 0:HALF, :], zeros], axis=0)
        qw_scr[qs, 1] = jnp.concatenate([zeros, qaug_ref[0, 1, qs, 0:HALF, :]], axis=0)

    def scores(ch):
        br, qs, g, hh, kj, _ = ch
        off = pl.multiple_of(kj * KT, KT)
        if br == SLC:
            return _dot(kaug_ref[0, g, pl.ds(off, KT), :], qaug_ref[0, g, qs, :, hh * Q:(hh + 1) * Q])
        return _dot(kw_ref[0, pl.ds(off, KT), :], qw_scr[qs, g, :, hh * Q:(hh + 1) * Q])

    def absorb(ch, s):
        br, qs, g, hh, kj, cap = ch
        cols = slice(hh * Q, (hh + 1) * Q)
        if cap is not None:
            s = jnp.minimum(s, cap_scr[cap])
        m_prev = m_scr[br, qs, g, :, cols]
        m_new = jnp.maximum(m_prev, jnp.max(s, axis=0, keepdims=True))
        p = jnp.exp2(s - m_new)
        alpha = jnp.exp2(m_prev - m_new)
        vt = vst_ref[0, g, kj] if br == SLC else vwt_ref[0, g, kj]
        acc_scr[br, qs, g, :, cols] = alpha * acc_scr[br, qs, g, :, cols] + _dot(vt, p.astype(BF16))
        m_scr[br, qs, g, :, cols] = m_new

    NSH = QS * NSA_HEADS
    RING = 2 * NSH

    def run(chains, ready=()):
        order = list(ready) + list(chains)
        assert len(ready) + RUN_AHEAD < RING
        for t in range(-RUN_AHEAD, len(order)):
            n = t + RUN_AHEAD
            if len(ready) <= n < len(order):
                s_scr[(n % RING) // NSH, n % NSH] = scores(order[n])
            if t >= 0:
                absorb(order[t], s_scr[(t % RING) // NSH, t % NSH])

    def tile(br, qs, kj, cap):
        return [(br, qs, g, hh, kj, cap) for g in range(NSA_GROUPS) for hh in range(NSA_HPG)]

    def shared(kj, first_cap=None):
        return [ch for qs in range(QS) for ch in tile(SLC, qs, kj, first_cap if qs == 0 else None)]

    def issue(kj, slot, n):
        s_scr[slot, n] = scores(shared(kj)[n])

    for n in range(NSH):
        issue(0, 0, n)

    def slc_body(j, carry):
        produce = [(2 * j + 1, 1, n) for n in range(NSH)] + [(2 * j + 2, 0, n) for n in range(NSH)]
        absorbs = [(2 * j, 0, n) for n in range(NSH)] + [(2 * j + 1, 1, n) for n in range(NSH)]
        for t in range(-LOOP_AHEAD, len(absorbs)):
            if t + LOOP_AHEAD < len(produce):
                issue(*produce[t + LOOP_AHEAD])
            if t >= 0:
                kj, slot, n = absorbs[t]
                absorb(shared(kj)[n], s_scr[slot, n])
        return carry

    lax.fori_loop(0, t0 // 2, slc_body, 0)
    ready = shared(t0, CAUSAL)

    always = []
    later = {}
    for qs in range(QS):
        for kk in range(1, qs):
            always += tile(SLC, qs, t0 + kk, None)
        if qs > 0:
            always += tile(SLC, qs, t0 + qs, CAUSAL)
        always += tile(WIN, qs, t0 + qs, CAUSAL)
        for back in range(1, WT + 1):
            chains = tile(WIN, qs, t0 + qs - back, EDGE if back == WT else None)
            if back <= qs:
                always += chains
            else:
                later.setdefault(-(-(back - qs) // QS) * QS, []).extend(chains)
    run(always, ready)
    for need, chains in sorted(later.items()):
        @pl.when(t0 >= need)
        def _():
            run(chains)

    for qs in range(QS):
        gt = g_ref[0, qs * Q:(qs + 1) * Q, :].T
        for pb in range(NSA_HEADS // 2):
            halves = []
            for e in range(2):
                h = 2 * pb + e
                g, hh = divmod(h, NSA_HPG)
                cols = slice(hh * Q, (hh + 1) * Q)
                o = ocmp_ref[0, qs, h * NSA_DH:(h + 1) * NSA_DH, :]
                for br in (SLC, WIN):
                    scale = gt[3 * h + 1 + br:3 * h + 2 + br] / acc_scr[br, qs, g, NSA_DH:NSA_DH + 1, cols]
                    o = o + scale * acc_scr[br, qs, g, 0:NSA_DH, cols]
                halves.append(o)
            o_ref[0, qs * Q:(qs + 1) * Q, pb * LANES:(pb + 1) * LANES] = jnp.concatenate(halves, axis=0).T.astype(BF16)


def _flash(qaug, kaug, vst, kw, vwt, gates, ocmp):
    B, _, NQ, _, R = qaug.shape
    Q = R // NSA_HPG
    S = NQ * Q
    QS = min(FLASH_SUBTILES, NQ)
    assert WINDOW % Q == 0 and vst.shape[4] == Q and NQ % QS == 0 and QS % 2 == 0
    per_b = lambda shape: pl.BlockSpec((1,) + shape, lambda b, i: (b,) + (0,) * len(shape))
    return pl.pallas_call(
        _flash_kernel,
        grid=(B, NQ // QS),
        in_specs=[pl.BlockSpec((1, NSA_GROUPS, QS, LANES, R), lambda b, i: (b, 0, i, 0, 0)),
                  per_b((NSA_GROUPS, S, LANES)), per_b(vst.shape[1:]), per_b((S, LANES)), per_b(vwt.shape[1:]),
                  pl.BlockSpec((1, QS * Q, LANES), lambda b, i: (b, i, 0)),
                  pl.BlockSpec((1, QS, NSA_QW, Q), lambda b, i: (b, i, 0, 0))],
        out_specs=pl.BlockSpec((1, QS * Q, NSA_QW), lambda b, i: (b, i, 0)),
        out_shape=jax.ShapeDtypeStruct((B, S, NSA_QW), BF16),
        scratch_shapes=[pltpu.VMEM((2, QS, NSA_GROUPS, 1, R), F32),
                        pltpu.VMEM((2, QS, NSA_GROUPS, VAL_ROWS, R), F32),
                        pltpu.VMEM((QS, NSA_GROUPS, LANES, R), BF16), pltpu.VMEM((2, Q, Q), F32),
                        pltpu.VMEM((2, QS * NSA_HEADS, Q, Q), F32)],
        compiler_params=_params(("arbitrary", "arbitrary")),
        name="flash",
    )(qaug, kaug, vst, kw, vwt, gates, ocmp)


def _retention_kernel(rq_ref, rk_ref, rv_ref, dmat_ref, qdec_ref, kdec_ref, cdec_ref, nw_ref, o_ref, st_scr):
    C = dmat_ref.shape[1]

    @pl.when(pl.program_id(1) == 0)
    def _():
        st_scr[...] = jnp.zeros_like(st_scr)

    low = lax.broadcasted_iota(jnp.int32, (C, LANES), 1) < HALF
    same_head = ((lax.broadcasted_iota(jnp.int32, (LANES, LANES), 0) < HALF) ==
                 (lax.broadcasted_iota(jnp.int32, (LANES, LANES), 1) < HALF))
    head_mean = jnp.where(same_head, 1.0 / RET_DV, 0.0).astype(BF16)
    n_chunks = rq_ref.shape[1] // C
    pairs = [(cc, p) for cc in range(n_chunks) for p in range(RET_HEADS // 2)]
    rows = lambda cc: slice(cc * C, (cc + 1) * C)
    lanes = lambda p: slice(p * LANES, (p + 1) * LANES)
    state = [st_scr[p] for p in range(RET_HEADS // 2)]
    scores, cross = {}, {}
    for cc, p in pairs:
        q2, k2, v2 = rq_ref[0, rows(cc), lanes(p)], rk_ref[0, rows(cc), lanes(p)], rv_ref[0, rows(cc), lanes(p)]
        q2f = q2.astype(F32)
        qe = jnp.where(low, q2f, 0.0).astype(BF16)
        qo = jnp.where(low, 0.0, q2f).astype(BF16)
        scores[cc, p] = (_dot_nt(qe, k2), _dot_nt(qo, k2))
        cross[cc, p] = _dot(q2, state[p].astype(BF16))
        kd = (k2.astype(F32) * kdec_ref[:, lanes(p)]).astype(BF16)
        state[p] = state[p] * cdec_ref[p:p + 1, :] + jnp.where(same_head, _dot_tn(kd, v2), 0.0)
    for p in range(RET_HEADS // 2):
        st_scr[p] = state[p]
    outs = {}
    for cc, p in pairs:
        v2 = rv_ref[0, rows(cc), lanes(p)]
        ie = (scores[cc, p][0] * dmat_ref[2 * p]).astype(BF16)
        io = (scores[cc, p][1] * dmat_ref[2 * p + 1]).astype(BF16)
        intra = jnp.where(low, _dot(ie, v2), _dot(io, v2))
        outs[cc, p] = intra + cross[cc, p] * qdec_ref[:, lanes(p)]
    means = {key: _dot(o.astype(BF16), head_mean) for key, o in outs.items()}
    devs = {key: outs[key] - means[key] for key in outs}
    variances = {key: _dot((d * d).astype(BF16), head_mean) for key, d in devs.items()}
    for cc, p in pairs:
        o_ref[0, rows(cc), lanes(p)] = devs[cc, p] * lax.rsqrt(variances[cc, p] + EPS) * nw_ref[:, lanes(p)]


def _retention(rq, rk, rv, ret_norm_w):
    B, S, _ = rq.shape
    C = min(RET_CHUNK, S)
    H = RET_HEADS
    log_g = np.log(1.0 - 2.0 ** (-5.0 - np.arange(H, dtype=np.float64)))
    i = np.arange(C, dtype=np.float64)
    diff = i[:, None] - i[None, :]
    as_f32 = lambda a: jnp.asarray(a, F32)
    dmat = as_f32(np.where(diff >= 0, np.exp(log_g[:, None, None] * np.maximum(diff, 0.0)), 0.0))
    per_lane = lambda a: np.repeat(a, RET_DK, axis=-1)
    qdec = as_f32(per_lane(np.exp(log_g[None, :] * (i[:, None] + 1.0))))
    kdec = as_f32(per_lane(np.exp(log_g[None, :] * (C - 1.0 - i[:, None]))))
    cdec = as_f32(per_lane(np.exp(log_g * C)[None, :]).reshape(H // 2, LANES))
    span = C * min(RET_CHUNKS_PER_STEP, S // C)
    tok = lambda w: pl.BlockSpec((1, span, w), lambda b, i: (b, i, 0))
    return pl.pallas_call(
        _retention_kernel,
        grid=(B, S // span),
        in_specs=[tok(RET_QW), tok(RET_QW), tok(RET_VW),
                  _full_spec((H, C, C)), _full_spec((C, RET_QW)), _full_spec((C, RET_QW)),
                  _full_spec((H // 2, LANES)), _full_spec((1, RET_VW))],
        out_specs=tok(RET_VW),
        out_shape=jax.ShapeDtypeStruct((B, S, RET_VW), F32),
        scratch_shapes=[pltpu.VMEM((H // 2, LANES, LANES), F32)],
        compiler_params=_params(("arbitrary", "arbitrary")),
        name="retention",
    )(rq, rk, rv, dmat, qdec, kdec, cdec, ret_norm_w.reshape(1, RET_VW))


def _mix_kernel(h_ref, mod_ref, nw_ref, onsa_ref, oret_ref, wrg_ref, wga_ref, wgb_ref, wn_ref, wr_ref, wo_ref, o_ref):
    x = h_ref[0]
    sh = mod_ref[0, 3:4, :]
    sc = mod_ref[0, 4:5, :]
    gt = mod_ref[0, 5:6, :]
    u = _norm_mod(x, nw_ref[...], sc, sh).astype(BF16)
    rg = _dot_nt(u, wrg_ref[...])
    y_nsa = _dot(onsa_ref[0], wn_ref[...])
    ga = jax.nn.sigmoid(_dot_nt(u, wga_ref[...]))
    oret = (oret_ref[0] * _silu(rg)).astype(BF16)
    gb = jax.nn.sigmoid(_dot_nt(u, wgb_ref[...]))
    y_ret = _dot(oret, wr_ref[...])
    mixed = (ga * y_nsa + gb * y_ret).astype(BF16)
    o_ref[0] = x + gt * _dot(mixed, wo_ref[...])


def _mix(h, mod, norm_w, o_nsa, o_ret, w_in, w_nsa_up, w_ret_up, w_out, *, tm):
    B, S, D = h.shape
    o = np.cumsum([0, NSA_QW] + [NSA_KVW] * 6 + [3 * NSA_HEADS, RET_QW, RET_QW, RET_VW, RET_VW, D, D])
    wt = w_in.T
    wrg = wt[o[11]:o[12]].astype(BF16)
    wga = wt[o[12]:o[13]].astype(BF16)
    wgb = wt[o[13]:o[14]].astype(BF16)
    wn, wr, wo = w_nsa_up.astype(BF16), w_ret_up.astype(BF16), w_out.astype(BF16)
    tok = lambda w: pl.BlockSpec((1, tm, w), lambda b, i: (b, i, 0))
    return pl.pallas_call(
        _mix_kernel,
        grid=(B, S // tm),
        in_specs=[tok(D), pl.BlockSpec((1, 9, D), lambda b, i: (b, 0, 0)), _full_spec((1, D)),
                  tok(NSA_QW), tok(RET_VW),
                  _full_spec(wrg.shape), _full_spec(wga.shape), _full_spec(wgb.shape),
                  _full_spec(wn.shape), _full_spec(wr.shape), _full_spec(wo.shape)],
        out_specs=tok(D),
        out_shape=jax.ShapeDtypeStruct((B, S, D), F32),
        compiler_params=_params(("arbitrary", "arbitrary")),
        name="mix",
    )(h, mod, norm_w.reshape(1, D), o_nsa, o_ret, wrg, wga, wgb, wn, wr, wo)


def kernel(x, c, ada_w, ada_b, norm1_w, ffn1_w_in, ffn1_w_out, norm2_w, w_in, cmp_k_pe, cmp_k_w1, cmp_k_w2,
           cmp_v_pe, cmp_v_w1, cmp_v_w2, ret_norm_w, w_nsa_up, w_ret_up, w_out, norm3_w, ffn2_w_in, ffn2_w_out,
           final_norm_w):
    B, S, D = x.shape
    depth = ada_w.shape[0]
    assert depth >= 1
    tm = min(TOKEN_TILE, S)
    tm_ffn = min(FFN_TOKEN_TILE, S)
    h = x
    for l in range(depth):
        last = l == depth - 1
        mod = _modulation(c, ada_w[l], ada_b[l]).reshape(B, 9, D)
        h = _ffn(h, mod, norm1_w[l], final_norm_w, ffn1_w_in[l], ffn1_w_out[l], mod_base=0, final_norm=False, tm=tm_ffn)
        q, cv, kaug, vst, kw, vwt, gates, rq, rk, rv = _projection(h, mod, norm2_w[l], w_in[l], tm=tm)
        kvc, kvct = _compress(cv, (cmp_k_pe[l], cmp_v_pe[l]), (cmp_k_w1[l], cmp_v_w1[l]), (cmp_k_w2[l], cmp_v_w2[l]))
        qaug, ocmp = _select(q, kvc, kvct, gates)
        o_nsa = _flash(qaug, kaug, vst, kw, vwt, gates, ocmp)
        o_ret = _retention(rq, rk, rv, ret_norm_w[l])
        h = _mix(h, mod, norm2_w[l], o_nsa, o_ret, w_in[l], w_nsa_up[l], w_ret_up[l], w_out[l], tm=tm)
        h = _ffn(h, mod, norm3_w[l], final_norm_w, ffn2_w_in[l], ffn2_w_out[l], mod_base=6, final_norm=last, tm=tm_ffn)
    return h
```

```python
import functools

import numpy as np
import jax
import jax.numpy as jnp
from jax import lax
from jax.experimental import pallas as pl
from jax.experimental.pallas import tpu as pltpu

F32 = jnp.float32
BF16 = jnp.bfloat16

NSA_HEADS = 8
NSA_GROUPS = 2
NSA_HPG = NSA_HEADS // NSA_GROUPS
NSA_DH = 64
CMP_STRIDE = 16
CMP_LEN = 2 * CMP_STRIDE
CMP_HIDDEN = 128
SEL_BLOCK = 64
SEL_TOPK = 16
WINDOW = 512
FORCE_BONUS = 1e4
RET_HEADS = 8
RET_DK = 64
RET_DV = 64
ROPE_BASE = 10000.0
D_FF = 2816
EPS = 1e-6
NEG = -1e30

NSA_QW = NSA_HEADS * NSA_DH
NSA_KVW = NSA_GROUPS * NSA_DH
RET_QW = RET_HEADS * RET_DK
RET_VW = RET_HEADS * RET_DV

LANES = 128
HALF = LANES // 2
VMEM_LIMIT = 56 * 1024 * 1024
FFN_CHUNK = 256
TOKEN_TILE = 512
FFN_TOKEN_TILE = 1024
RET_CHUNK = 256
RET_CHUNKS_PER_STEP = 2
ATT_Q = 256
VAL_ROWS = NSA_DH + 16
LOG2E = 1.4426950408889634
FLASH_SUBTILES = 2
RUN_AHEAD = 6
RANK_SPAN = 16
LOOP_AHEAD = 2


def _dot(a, b):
    return jnp.dot(a, b, preferred_element_type=F32)


def _dot_nt(a, b):
    return lax.dot_general(a, b, (((1,), (1,)), ((), ())), preferred_element_type=F32)


def _dot_tn(a, b):
    return lax.dot_general(a, b, (((0,), (0,)), ((), ())), preferred_element_type=F32)


def _silu(a):
    return a * jax.nn.sigmoid(a)


def _norm_mod(x, nw, sc, sh):
    ms = jnp.mean(x * x, axis=-1, keepdims=True)
    gain = nw * (1.0 + sc)
    return (x * lax.rsqrt(ms + EPS)) * gain + sh


def _full_spec(shape):
    zeros = (0,) * len(shape)
    return pl.BlockSpec(shape, lambda *_: zeros)


def _resident_spec(shape):
    zeros = (0,) * len(shape)
    return pl.BlockSpec(shape, lambda *_: zeros, pipeline_mode=pl.Buffered(1))


def _params(sem):
    return pltpu.CompilerParams(dimension_semantics=sem, vmem_limit_bytes=VMEM_LIMIT)


def _mod_kernel(c_ref, w_ref, b_ref, o_ref):
    c = c_ref[...]
    o_ref[...] = _dot(_silu(c).astype(BF16), w_ref[...].astype(BF16)) + b_ref[...]


def _modulation(c, ada_w, ada_b):
    B, D = c.shape
    N = ada_w.shape[1]
    tn = N // 8
    return pl.pallas_call(
        _mod_kernel,
        grid=(N // tn,),
        in_specs=[pl.BlockSpec((B, D), lambda j: (0, 0)),
                  pl.BlockSpec((D, tn), lambda j: (0, j)),
                  pl.BlockSpec((1, tn), lambda j: (0, j))],
        out_specs=pl.BlockSpec((B, tn), lambda j: (0, j)),
        out_shape=jax.ShapeDtypeStruct((B, N), F32),
        compiler_params=_params(("arbitrary",)),
        name="modulation",
    )(c, ada_w, ada_b.reshape(1, N))


def _ffn_kernel(x_ref, mod_ref, nw_ref, fw_ref, wi_ref, wo_ref, o_ref, u_scr, acc_scr,
                *, mod_base, final_norm):
    x = x_ref[0]
    sh = mod_ref[0, mod_base:mod_base + 1, :]
    sc = mod_ref[0, mod_base + 1:mod_base + 2, :]
    gt = mod_ref[0, mod_base + 2:mod_base + 3, :]
    u_scr[...] = _norm_mod(x, nw_ref[...], sc, sh).astype(BF16)
    acc_scr[...] = jnp.zeros_like(acc_scr)
    dff = wo_ref.shape[0]
    for c0 in range(0, dff, FFN_CHUNK):
        u = u_scr[...]
        a = _dot(u, wi_ref[:, c0:c0 + FFN_CHUNK])
        b = _dot(u, wi_ref[:, dff + c0:dff + c0 + FFN_CHUNK])
        acc_scr[...] += _dot((_silu(a) * b).astype(BF16), wo_ref[c0:c0 + FFN_CHUNK, :])
    y = x + 0.5 * gt * acc_scr[...]
    if final_norm:
        ms = jnp.mean(y * y, axis=-1, keepdims=True)
        y = y * lax.rsqrt(ms + EPS) * fw_ref[...]
    o_ref[0] = y


def _ffn(x, mod, norm_w, final_w, w_in, w_out, *, mod_base, final_norm, tm):
    B, S, D = x.shape
    assert w_out.shape[0] % FFN_CHUNK == 0
    wi = w_in.astype(BF16)
    wo = w_out.astype(BF16)
    kern = functools.partial(_ffn_kernel, mod_base=mod_base, final_norm=final_norm)
    return pl.pallas_call(
        kern,
        grid=(B, S // tm),
        in_specs=[pl.BlockSpec((1, tm, D), lambda b, i: (b, i, 0)),
                  pl.BlockSpec((1, 9, D), lambda b, i: (b, 0, 0)),
                  _full_spec((1, D)), _full_spec((1, D)),
                  _resident_spec(wi.shape), _resident_spec(wo.shape)],
        out_specs=pl.BlockSpec((1, tm, D), lambda b, i: (b, i, 0)),
        out_shape=jax.ShapeDtypeStruct((B, S, D), F32),
        scratch_shapes=[pltpu.VMEM((tm, D), BF16), pltpu.VMEM((tm, D), F32)],
        compiler_params=_params(("arbitrary", "arbitrary")),
        name="ffn_final" if final_norm else "ffn",
    )(x, mod, norm_w.reshape(1, D), final_w.reshape(1, D), wi, wo)


def _proj_kernel(h_ref, mod_ref, nw_ref, wq_ref, wkv_ref, wg_ref, wr_ref, cos_ref, sin_ref,
                 q_ref, cv_ref, kaug_ref, vst_ref, kw_ref, vwt_ref, g_ref, rq_ref, rk_ref, rv_ref,
                 u_scr, cv_scr):
    tm = h_ref.shape[1]
    x = h_ref[0]
    sh = mod_ref[0, 3:4, :]
    sc = mod_ref[0, 4:5, :]
    u_scr[...] = _norm_mod(x, nw_ref[...], sc, sh).astype(BF16)
    u = u_scr[...]

    q_ref[0] = _dot_nt(u, wq_ref[...]).astype(BF16)

    kv = _dot_nt(u, wkv_ref[...])
    for t in range(2):
        cv_scr[t] = kv[:, t * LANES:(t + 1) * LANES]
        for tok in range(CMP_STRIDE):
            rows = cv_scr[t, pl.ds(tok, tm // CMP_STRIDE, stride=CMP_STRIDE), :]
            cv_ref[t, 0, :, tok * LANES:(tok + 1) * LANES] = rows.astype(BF16)
    ks = kv[:, 2 * LANES:3 * LANES]
    lane = lax.broadcasted_iota(jnp.int32, (tm, LANES), 1)
    row = lax.broadcasted_iota(jnp.int32, (tm, LANES), 0)
    low = lane < HALF
    blk = (pl.program_id(1) * tm + row) // SEL_BLOCK
    onehot = jnp.where(lane - HALF == blk, 1.0, 0.0)
    kaug_ref[0, 0] = jnp.where(low, ks, onehot).astype(BF16)
    kaug_ref[0, 1] = jnp.where(low, pltpu.roll(ks, HALF, 1), onehot).astype(BF16)
    kw_ref[0] = kv[:, 4 * LANES:5 * LANES].astype(BF16)
    kt = vst_ref.shape[4]
    ones = jnp.ones((VAL_ROWS - NSA_DH, tm), F32)
    for ref, col in ((vst_ref, 3), (vwt_ref, 5)):
        vt = kv[:, col * LANES:(col + 1) * LANES].T
        for g in range(NSA_GROUPS):
            aug = jnp.concatenate([vt[g * HALF:(g + 1) * HALF], ones], axis=0).astype(BF16)
            for c in range(tm // kt):
                ref[0, g, c] = aug[:, c * kt:(c + 1) * kt]

    g_ref[0] = jax.nn.sigmoid(_dot_nt(u, wg_ref[...]))

    r = _dot_nt(u, wr_ref[...])
    cos = cos_ref[...]
    sin = sin_ref[...]
    first = (lane & (RET_DK // 2)) == 0
    npair = RET_QW // LANES
    for j in range(2 * npair):
        xb = r[:, j * LANES:(j + 1) * LANES]
        partner = jnp.where(first, pltpu.roll(xb, LANES - RET_DK // 2, 1), pltpu.roll(xb, RET_DK // 2, 1))
        y = (xb * cos + partner * sin).astype(BF16)
        if j < npair:
            rq_ref[0, :, j * LANES:(j + 1) * LANES] = y
        else:
            rk_ref[0, :, (j - npair) * LANES:(j - npair + 1) * LANES] = y
    rv_ref[0] = r[:, 2 * RET_QW:2 * RET_QW + RET_VW].astype(BF16)


def _projection(h, mod, norm_w, w_in, *, tm):
    B, S, D = h.shape
    o = np.cumsum([0, NSA_QW] + [NSA_KVW] * 6 + [3 * NSA_HEADS, RET_QW, RET_QW, RET_VW])
    scale_q = NSA_DH ** -0.5 * LOG2E
    scale_k = RET_DK ** -0.5
    wt = w_in.T
    wq = (wt[o[0]:o[1]] * scale_q).astype(BF16)
    wkv = wt[o[1]:o[7]].astype(BF16)
    wg = jnp.pad(wt[o[7]:o[8]], ((0, LANES - 3 * NSA_HEADS), (0, 0))).astype(BF16)
    wr = jnp.concatenate([wt[o[8]:o[9]], wt[o[9]:o[10]] * scale_k, wt[o[10]:o[11]]], axis=0).astype(BF16)
    half = RET_DK // 2
    lane = np.arange(LANES)
    inv = ROPE_BASE ** (-np.arange(half, dtype=np.float64) / half)
    ang = np.arange(S, dtype=np.float64)[:, None] * inv[lane % half][None, :]
    cos_t = jnp.asarray(np.cos(ang), F32)
    sin_t = jnp.asarray(np.where((lane % RET_DK) < half, -np.sin(ang), np.sin(ang)), F32)
    tok = lambda w: pl.BlockSpec((1, tm, w), lambda b, i: (b, i, 0))
    kt = min(ATT_Q, S)
    vt_spec = pl.BlockSpec((1, NSA_GROUPS, tm // kt, VAL_ROWS, kt), lambda b, i: (b, 0, i, 0, 0))
    vt_shape = jax.ShapeDtypeStruct((B, NSA_GROUPS, S // kt, VAL_ROWS, kt), BF16)
    outs = pl.pallas_call(
        _proj_kernel,
        grid=(B, S // tm),
        in_specs=[tok(D), pl.BlockSpec((1, 9, D), lambda b, i: (b, 0, 0)), _full_spec((1, D)),
                  _full_spec(wq.shape), _full_spec(wkv.shape), _full_spec(wg.shape), _full_spec(wr.shape),
                  pl.BlockSpec((tm, LANES), lambda b, i: (i, 0)), pl.BlockSpec((tm, LANES), lambda b, i: (i, 0))],
        out_specs=[tok(NSA_QW),
                   pl.BlockSpec((2, 1, tm // CMP_STRIDE, CMP_STRIDE * LANES), lambda b, i: (0, b, i, 0)),
                   pl.BlockSpec((1, NSA_GROUPS, tm, LANES), lambda b, i: (b, 0, i, 0)),
                   vt_spec, tok(LANES), vt_spec, tok(LANES),
                   tok(RET_QW), tok(RET_QW), tok(RET_VW)],
        out_shape=[jax.ShapeDtypeStruct((B, S, NSA_QW), BF16),
                   jax.ShapeDtypeStruct((2, B, S // CMP_STRIDE, CMP_STRIDE * LANES), BF16),
                   jax.ShapeDtypeStruct((B, NSA_GROUPS, S, LANES), BF16),
                   vt_shape,
                   jax.ShapeDtypeStruct((B, S, LANES), BF16),
                   vt_shape,
                   jax.ShapeDtypeStruct((B, S, LANES), F32),
                   jax.ShapeDtypeStruct((B, S, RET_QW), BF16),
                   jax.ShapeDtypeStruct((B, S, RET_QW), BF16),
                   jax.ShapeDtypeStruct((B, S, RET_VW), BF16)],
        scratch_shapes=[pltpu.VMEM((tm, D), BF16), pltpu.VMEM((2, tm, LANES), F32)],
        compiler_params=_params(("arbitrary", "arbitrary")),
        name="projection",
    )(h, mod, norm_w.reshape(1, D), wq, wkv, wg, wr, cos_t, sin_t)
    return outs


def _compress_kernel(ch_ref, w1c_ref, pe_ref, w1_ref, w2_ref, o_ref, ot_ref):
    ch = ch_ref[0, 0]
    nc = ch.shape[0]
    ab = _dot(ch, w1c_ref[0])
    pt = _dot(pe_ref[0], w1_ref[0])[0:1, :]
    pt2 = jnp.concatenate([pt, pt], axis=1)
    nh = NSA_GROUPS * CMP_HIDDEN
    hid = ab[:, :nh] + pltpu.roll(ab[:, nh:], nc - 1, 0) + pt2
    out = _dot(_silu(hid).astype(BF16), w2_ref[0])
    row = lax.broadcasted_iota(jnp.int32, out.shape, 0)
    out = jnp.where(row < nc - 1, out, 0.0)
    o_ref[0, 0] = out.astype(BF16)
    ot_ref[0, 0, 0:LANES, :] = out.T.astype(BF16)
    ot_ref[0, 0, LANES:, :] = jnp.ones((ot_ref.shape[2] - LANES, nc), BF16)


def _compress(cv, pe, w1, w2):
    _, B, nc, width = cv.shape
    chunks = cv
    eye = jnp.eye(NSA_GROUPS, dtype=F32)
    nh = NSA_GROUPS * CMP_HIDDEN

    def expand(w):
        w = w.reshape(CMP_STRIDE, NSA_DH, CMP_HIDDEN)
        return jnp.einsum("ldj,gh->lgdhj", w, eye).reshape(width, nh)

    half = CMP_STRIDE * NSA_DH
    w1c = jnp.stack([jnp.concatenate([expand(w[:half]), expand(w[half:])], axis=1) for w in w1]).astype(BF16)
    w2bd = jnp.stack([jnp.einsum("jd,gh->gjhd", w, eye).reshape(nh, NSA_KVW) for w in w2]).astype(BF16)
    pe8 = jnp.stack([jnp.broadcast_to(p.reshape(1, CMP_LEN * NSA_DH), (8, CMP_LEN * NSA_DH)) for p in pe]).astype(BF16)
    w1s = jnp.stack(w1).astype(BF16)
    sel = lambda shape: pl.BlockSpec((1,) + shape, lambda t, b: (t,) + (0,) * len(shape))
    return pl.pallas_call(
        _compress_kernel,
        grid=(2, B),
        in_specs=[pl.BlockSpec((1, 1, nc, width), lambda t, b: (t, b, 0, 0)),
                  sel(w1c.shape[1:]), sel(pe8.shape[1:]), sel(w1s.shape[1:]), sel(w2bd.shape[1:])],
        out_specs=[pl.BlockSpec((1, 1, nc, LANES), lambda t, b: (t, b, 0, 0)),
                   pl.BlockSpec((1, 1, LANES + 16, nc), lambda t, b: (t, b, 0, 0))],
        out_shape=[jax.ShapeDtypeStruct((2, B, nc, LANES), BF16),
                   jax.ShapeDtypeStruct((2, B, LANES + 16, nc), BF16)],
        compiler_params=_params(("arbitrary", "arbitrary")),
        name="compress",
    )(chunks, w1c, pe8, w1s, w2bd)


def _select_kernel(q_ref, kc_ref, vct_ref, g_ref, ovl_ref, qaug_ref, ocmp_ref, imp_scr, rank_scr, cap_scr,
                   *, n_sel, n_tiles):
    Q = q_ref.shape[1]
    NC = kc_ref.shape[2]
    NB = ovl_ref.shape[0]
    qi = pl.program_id(1)
    q0 = qi * Q
    gt = g_ref[0].T
    any_allowed = q0 + lax.broadcasted_iota(jnp.int32, (1, Q), 1) >= CMP_LEN - 1
    j_blk = lax.broadcasted_iota(jnp.int32, (NB, Q), 0)
    t_blk = q0 + lax.broadcasted_iota(jnp.int32, (NB, Q), 1)
    cur = t_blk // SEL_BLOCK
    forced = (j_blk == 0) | (j_blk == cur) | (j_blk == cur - 1)
    valid = j_blk * SEL_BLOCK <= t_blk
    j_sub = lax.broadcasted_iota(jnp.int32, (8, Q), 0)
    per_tile = Q // SEL_BLOCK

    def compressed_branch(rows):
        kcf = kc_ref[0, 0, 0:rows, :].astype(F32)
        kcr = pltpu.roll(kcf, HALF, 1)
        lowk = lax.broadcasted_iota(jnp.int32, (rows, LANES), 1) < HALF
        k_even = [jnp.where(lowk, kcf, 0.0).astype(BF16), jnp.where(lowk, kcr, 0.0).astype(BF16)]
        k_odd = [jnp.where(lowk, 0.0, kcr).astype(BF16), jnp.where(lowk, 0.0, kcf).astype(BF16)]
        vct = vct_ref[0, 0, :, 0:rows]
        ovl = ovl_ref[:, 0:rows]
        n_row = lax.broadcasted_iota(jnp.int32, (rows, Q), 0)
        t_col = q0 + lax.broadcasted_iota(jnp.int32, (rows, Q), 1)
        cap_scr[0:rows, :] = jnp.where(n_row * CMP_STRIDE + (CMP_LEN - 1) <= t_col, -NEG, NEG)

        scores = []
        for pb in range(NSA_HEADS // 2):
            g, pp = divmod(pb, NSA_HPG // 2)
            qt = q_ref[0, :, pb * LANES:(pb + 1) * LANES].astype(F32).T.astype(BF16)
            for e, kk in enumerate((k_even[g], k_odd[g])):
                scores.append(_dot(kk, qt))
                col = (2 * pp + e) * Q
                qaug_ref[0, g, 0, 0:HALF, col:col + Q] = qt[e * HALF:(e + 1) * HALF]

        for g in range(NSA_GROUPS):
            imp = jnp.zeros((NB, Q), F32)
            for hh in range(NSA_HPG):
                h = g * NSA_HPG + hh
                s = jnp.minimum(scores[h], cap_scr[0:rows, :])
                mx = jnp.max(s, axis=0, keepdims=True)
                ex = jnp.exp2(s - mx).astype(BF16)
                oa = _dot(vct, ex)
                inv = jnp.where(any_allowed, 1.0 / oa[LANES:LANES + 1], 0.0)
                ocmp_ref[0, 0, h * NSA_DH:(h + 1) * NSA_DH, :] = (gt[3 * h:3 * h + 1] * inv) * oa[g * HALF:(g + 1) * HALF]
                imp = imp + _dot(ovl, ex) * inv

            imp = jnp.where(forced, imp + FORCE_BONUS, imp)
            imp_scr[g] = jnp.where(valid, imp, -FORCE_BONUS)

    if n_tiles % 2 == 0 and NC % 32 == 0:
        @pl.when(qi < n_tiles // 2)
        def _():
            compressed_branch(NC // 2)

        @pl.when(qi >= n_tiles // 2)
        def _():
            compressed_branch(NC)
    else:
        compressed_branch(NC)
    rank_scr[...] = jnp.zeros_like(rank_scr)

    for c0 in range(0, NB, RANK_SPAN):
        for b0 in range(0, NB, RANK_SPAN):
            @pl.when(max(c0, b0) // per_tile <= qi)
            def _():
                slabs = range(b0 // 8, (b0 + RANK_SPAN) // 8)
                for g in range(NSA_GROUPS):
                    vals = [imp_scr[g, 8 * sb:8 * sb + 8, :] for sb in slabs]
                    cnts = [rank_scr[g, 8 * sb:8 * sb + 8, :] for sb in slabs]
                    for jp in range(c0, c0 + RANK_SPAN):
                        r = jnp.broadcast_to(imp_scr[g, jp:jp + 1, :], (8, Q))
                        for n, sb in enumerate(slabs):
                            v = vals[n]
                            if 8 * sb > jp:
                                ahead = jnp.where(r >= v, 1, 0)
                            elif 8 * sb + 7 <= jp:
                                ahead = jnp.where(r > v, 1, 0)
                            else:
                                ahead = jnp.where(j_sub + 8 * sb > jp, jnp.where(r >= v, 1, 0), jnp.where(r > v, 1, 0))
                            cnts[n] = cnts[n] + ahead
                    for n, sb in enumerate(slabs):
                        rank_scr[g, 8 * sb:8 * sb + 8, :] = cnts[n]

    for g in range(NSA_GROUPS):
        sel = (rank_scr[g] < n_sel) & valid
        bias = jnp.where(sel, 0.0, NEG).astype(BF16)
        for hh in range(NSA_HPG):
            qaug_ref[0, g, 0, HALF:HALF + NB, hh * Q:(hh + 1) * Q] = bias


def _select(q, kc, vct, gates):
    B, S, _ = q.shape
    Q = min(ATT_Q, S)
    NC = kc.shape[2]
    nc, ns = NC - 1, S // SEL_BLOCK
    NB = LANES - HALF
    assert ns <= NB and Q % SEL_BLOCK == 0
    c_start = np.arange(nc) * CMP_STRIDE
    s_start = np.arange(ns) * SEL_BLOCK
    overlap = ((c_start[:, None] < s_start[None, :] + SEL_BLOCK) &
               (c_start[:, None] + CMP_LEN > s_start[None, :])).astype(np.float32)
    ovl = np.zeros((NB, NC), np.float32)
    ovl[:ns, :nc] = overlap.T
    kern = functools.partial(_select_kernel, n_sel=min(SEL_TOPK, ns), n_tiles=S // Q)
    return pl.pallas_call(
        kern,
        grid=(B, S // Q),
        in_specs=[pl.BlockSpec((1, Q, NSA_QW), lambda b, i: (b, i, 0)),
                  pl.BlockSpec((1, 1, NC, LANES), lambda b, i: (0, b, 0, 0)),
                  pl.BlockSpec((1, 1) + vct.shape[2:], lambda b, i: (1, b, 0, 0)),
                  pl.BlockSpec((1, Q, LANES), lambda b, i: (b, i, 0)),
                  _full_spec((NB, NC))],
        out_specs=[pl.BlockSpec((1, NSA_GROUPS, 1, LANES, NSA_HPG * Q), lambda b, i: (b, 0, i, 0, 0)),
                   pl.BlockSpec((1, 1, NSA_QW, Q), lambda b, i: (b, i, 0, 0))],
        out_shape=[jax.ShapeDtypeStruct((B, NSA_GROUPS, S // Q, LANES, NSA_HPG * Q), BF16),
                   jax.ShapeDtypeStruct((B, S // Q, NSA_QW, Q), F32)],
        scratch_shapes=[pltpu.VMEM((NSA_GROUPS, NB, Q), F32), pltpu.VMEM((NSA_GROUPS, NB, Q), jnp.int32),
                        pltpu.VMEM((NC, Q), F32)],
        compiler_params=_params(("arbitrary", "arbitrary")),
        name="select",
    )(q, kc, vct, gates, jnp.asarray(ovl, BF16))


def _flash_kernel(qaug_ref, kaug_ref, vst_ref, kw_ref, vwt_ref, g_ref, ocmp_ref, o_ref,
                  m_scr, acc_scr, qw_scr, cap_scr, s_scr):
    QS = qaug_ref.shape[2]
    R = qaug_ref.shape[4]
    Q = R // NSA_HPG
    KT = kaug_ref.shape[2] // vst_ref.shape[2]
    WT = WINDOW // KT
    SLC, WIN = 0, 1
    CAUSAL, EDGE = 0, 1
    t0 = pl.program_id(1) * QS
    k_row = lax.broadcasted_iota(jnp.int32, (KT, Q), 0)
    i_col = lax.broadcasted_iota(jnp.int32, (KT, Q), 1)
    big = -NEG
    cap_scr[CAUSAL] = jnp.where(k_row <= i_col, big, NEG)
    cap_scr[EDGE] = jnp.where(k_row > i_col, big, NEG)
    m_scr[...] = jnp.full_like(m_scr, NEG)
    acc_scr[...] = jnp.zeros_like(acc_scr)
    zeros = jnp.zeros((HALF, R), BF16)
    for qs in range(QS):
        qw_scr[qs, 0] = jnp.concatenate([qaug_ref[0, 0, qs, 0:HALF, :], zeros], axis=0)
        qw_scr[qs, 1] = jnp.concatenate([zeros, qaug_ref[0, 1, qs, 0:HALF, :]], axis=0)

    def scores(ch):
        br, qs, g, hh, kj, _ = ch
        off = pl.multiple_of(kj * KT, KT)
        if br == SLC:
            return _dot(kaug_ref[0, g, pl.ds(off, KT), :], qaug_ref[0, g, qs, :, hh * Q:(hh + 1) * Q])
        return _dot(kw_ref[0, pl.ds(off, KT), :], qw_scr[qs, g, :, hh * Q:(hh + 1) * Q])

    def absorb(ch, s):
        br, qs, g, hh, kj, cap = ch
        cols = slice(hh * Q, (hh + 1) * Q)
        if cap is not None:
            s = jnp.minimum(s, cap_scr[cap])
        m_prev = m_scr[br, qs, g, :, cols]
        m_new = jnp.maximum(m_prev, jnp.max(s, axis=0, keepdims=True))
        p = jnp.exp2(s - m_new)
        alpha = jnp.exp2(m_prev - m_new)
        vt = vst_ref[0, g, kj] if br == SLC else vwt_ref[0, g, kj]
        acc_scr[br, qs, g, :, cols] = alpha * acc_scr[br, qs, g, :, cols] + _dot(vt, p.astype(BF16))
        m_scr[br, qs, g, :, cols] = m_new

    NSH = QS * NSA_HEADS
    RING = 2 * NSH

    def run(chains, ready=()):
        order = list(ready) + list(chains)
        assert len(ready) + RUN_AHEAD < RING
        for t in range(-RUN_AHEAD, len(order)):
            n = t + RUN_AHEAD
            if len(ready) <= n < len(order):
                s_scr[(n % RING) // NSH, n % NSH] = scores(order[n])
            if t >= 0:
                absorb(order[t], s_scr[(t % RING) // NSH, t % NSH])

    def tile(br, qs, kj, cap):
        return [(br, qs, g, hh, kj, cap) for g in range(NSA_GROUPS) for hh in range(NSA_HPG)]

    def shared(kj, first_cap=None):
        return [ch for qs in range(QS) for ch in tile(SLC, qs, kj, first_cap if qs == 0 else None)]

    def issue(kj, slot, n):
        s_scr[slot, n] = scores(shared(kj)[n])

    for n in range(NSH):
        issue(0, 0, n)

    def slc_body(j, carry):
        produce = [(2 * j + 1, 1, n) for n in range(NSH)] + [(2 * j + 2, 0, n) for n in range(NSH)]
        absorbs = [(2 * j, 0, n) for n in range(NSH)] + [(2 * j + 1, 1, n) for n in range(NSH)]
        for t in range(-LOOP_AHEAD, len(absorbs)):
            if t + LOOP_AHEAD < len(produce):
                issue(*produce[t + LOOP_AHEAD])
            if t >= 0:
                kj, slot, n = absorbs[t]
                absorb(shared(kj)[n], s_scr[slot, n])
        return carry

    lax.fori_loop(0, t0 // 2, slc_body, 0)
    ready = shared(t0, CAUSAL)

    always = []
    later = {}
    for qs in range(QS):
        for kk in range(1, qs):
            always += tile(SLC, qs, t0 + kk, None)
        if qs > 0:
            always += tile(SLC, qs, t0 + qs, CAUSAL)
        always += tile(WIN, qs, t0 + qs, CAUSAL)
        for back in range(1, WT + 1):
            chains = tile(WIN, qs, t0 + qs - back, EDGE if back == WT else None)
            if back <= qs:
                always += chains
            else:
                later.setdefault(-(-(back - qs) // QS) * QS, []).extend(chains)
    if len(later) == 1:
        (need, chains), = later.items()

        @pl.when(t0 >= need)
        def _():
            run(always + chains, ready)

        @pl.when(t0 < need)
        def _():
            run(always, ready)
    else:
        run(always, ready)
        for need, chains in sorted(later.items()):
            @pl.when(t0 >= need)
            def _():
                run(chains)

    for qs in range(QS):
        gt = g_ref[0, qs * Q:(qs + 1) * Q, :].T
        for pb in range(NSA_HEADS // 2):
            halves = []
            for e in range(2):
                h = 2 * pb + e
                g, hh = divmod(h, NSA_HPG)
                cols = slice(hh * Q, (hh + 1) * Q)
                o = ocmp_ref[0, qs, h * NSA_DH:(h + 1) * NSA_DH, :]
                for br in (SLC, WIN):
                    scale = gt[3 * h + 1 + br:3 * h + 2 + br] / acc_scr[br, qs, g, NSA_DH:NSA_DH + 1, cols]
                    o = o + scale * acc_scr[br, qs, g, 0:NSA_DH, cols]
                halves.append(o)
            o_ref[0, qs * Q:(qs + 1) * Q, pb * LANES:(pb + 1) * LANES] = jnp.concatenate(halves, axis=0).T.astype(BF16)


def _flash(qaug, kaug, vst, kw, vwt, gates, ocmp):
    B, _, NQ, _, R = qaug.shape
    Q = R // NSA_HPG
    S = NQ * Q
    QS = min(FLASH_SUBTILES, NQ)
    assert WINDOW % Q == 0 and vst.shape[4] == Q and NQ % QS == 0 and QS % 2 == 0
    per_b = lambda shape: pl.BlockSpec((1,) + shape, lambda b, i: (b,) + (0,) * len(shape))
    return pl.pallas_call(
        _flash_kernel,
        grid=(B, NQ // QS),
        in_specs=[pl.BlockSpec((1, NSA_GROUPS, QS, LANES, R), lambda b, i: (b, 0, i, 0, 0)),
                  per_b((NSA_GROUPS, S, LANES)), per_b(vst.shape[1:]), per_b((S, LANES)), per_b(vwt.shape[1:]),
                  pl.BlockSpec((1, QS * Q, LANES), lambda b, i: (b, i, 0)),
                  pl.BlockSpec((1, QS, NSA_QW, Q), lambda b, i: (b, i, 0, 0))],
        out_specs=pl.BlockSpec((1, QS * Q, NSA_QW), lambda b, i: (b, i, 0)),
        out_shape=jax.ShapeDtypeStruct((B, S, NSA_QW), BF16),
        scratch_shapes=[pltpu.VMEM((2, QS, NSA_GROUPS, 1, R), F32),
                        pltpu.VMEM((2, QS, NSA_GROUPS, VAL_ROWS, R), F32),
                        pltpu.VMEM((QS, NSA_GROUPS, LANES, R), BF16), pltpu.VMEM((2, Q, Q), F32),
                        pltpu.VMEM((2, QS * NSA_HEADS, Q, Q), F32)],
        compiler_params=_params(("arbitrary", "arbitrary")),
        name="flash",
    )(qaug, kaug, vst, kw, vwt, gates, ocmp)


def _retention_kernel(rq_ref, rk_ref, rv_ref, dmat_ref, qdec_ref, kdec_ref, cdec_ref, nw_ref, o_ref, st_scr):
    C = dmat_ref.shape[1]

    @pl.when(pl.program_id(1) == 0)
    def _():
        st_scr[...] = jnp.zeros_like(st_scr)

    low = lax.broadcasted_iota(jnp.int32, (C, LANES), 1) < HALF
    same_head = ((lax.broadcasted_iota(jnp.int32, (LANES, LANES), 0) < HALF) ==
                 (lax.broadcasted_iota(jnp.int32, (LANES, LANES), 1) < HALF))
    head_mean = jnp.where(same_head, 1.0 / RET_DV, 0.0).astype(BF16)
    n_chunks = rq_ref.shape[1] // C
    pairs = [(cc, p) for cc in range(n_chunks) for p in range(RET_HEADS // 2)]
    rows = lambda cc: slice(cc * C, (cc + 1) * C)
    lanes = lambda p: slice(p * LANES, (p + 1) * LANES)
    state = [st_scr[p] for p in range(RET_HEADS // 2)]
    scores, cross = {}, {}
    for cc, p in pairs:
        q2, k2, v2 = rq_ref[0, rows(cc), lanes(p)], rk_ref[0, rows(cc), lanes(p)], rv_ref[0, rows(cc), lanes(p)]
        q2f = q2.astype(F32)
        qe = jnp.where(low, q2f, 0.0).astype(BF16)
        qo = jnp.where(low, 0.0, q2f).astype(BF16)
        scores[cc, p] = (_dot_nt(qe, k2), _dot_nt(qo, k2))
        cross[cc, p] = _dot(q2, state[p].astype(BF16))
        kd = (k2.astype(F32) * kdec_ref[:, lanes(p)]).astype(BF16)
        state[p] = state[p] * cdec_ref[p:p + 1, :] + jnp.where(same_head, _dot_tn(kd, v2), 0.0)
    for p in range(RET_HEADS // 2):
        st_scr[p] = state[p]
    outs = {}
    for cc, p in pairs:
        v2 = rv_ref[0, rows(cc), lanes(p)]
        ie = (scores[cc, p][0] * dmat_ref[2 * p]).astype(BF16)
        io = (scores[cc, p][1] * dmat_ref[2 * p + 1]).astype(BF16)
        intra = jnp.where(low, _dot(ie, v2), _dot(io, v2))
        outs[cc, p] = intra + cross[cc, p] * qdec_ref[:, lanes(p)]
    means = {key: _dot(o.astype(BF16), head_mean) for key, o in outs.items()}
    devs = {key: outs[key] - means[key] for key in outs}
    variances = {key: _dot((d * d).astype(BF16), head_mean) for key, d in devs.items()}
    for cc, p in pairs:
        o_ref[0, rows(cc), lanes(p)] = devs[cc, p] * lax.rsqrt(variances[cc, p] + EPS) * nw_ref[:, lanes(p)]


def _retention(rq, rk, rv, ret_norm_w):
    B, S, _ = rq.shape
    C = min(RET_CHUNK, S)
    H = RET_HEADS
    log_g = np.log(1.0 - 2.0 ** (-5.0 - np.arange(H, dtype=np.float64)))
    i = np.arange(C, dtype=np.float64)
    diff = i[:, None] - i[None, :]
    as_f32 = lambda a: jnp.asarray(a, F32)
    dmat = as_f32(np.where(diff >= 0, np.exp(log_g[:, None, None] * np.maximum(diff, 0.0)), 0.0))
    per_lane = lambda a: np.repeat(a, RET_DK, axis=-1)
    qdec = as_f32(per_lane(np.exp(log_g[None, :] * (i[:, None] + 1.0))))
    kdec = as_f32(per_lane(np.exp(log_g[None, :] * (C - 1.0 - i[:, None]))))
    cdec = as_f32(per_lane(np.exp(log_g * C)[None, :]).reshape(H // 2, LANES))
    span = C * min(RET_CHUNKS_PER_STEP, S // C)
    tok = lambda w: pl.BlockSpec((1, span, w), lambda b, i: (b, i, 0))
    return pl.pallas_call(
        _retention_kernel,
        grid=(B, S // span),
        in_specs=[tok(RET_QW), tok(RET_QW), tok(RET_VW),
                  _full_spec((H, C, C)), _full_spec((C, RET_QW)), _full_spec((C, RET_QW)),
                  _full_spec((H // 2, LANES)), _full_spec((1, RET_VW))],
        out_specs=tok(RET_VW),
        out_shape=jax.ShapeDtypeStruct((B, S, RET_VW), F32),
        scratch_shapes=[pltpu.VMEM((H // 2, LANES, LANES), F32)],
        compiler_params=_params(("arbitrary", "arbitrary")),
        name="retention",
    )(rq, rk, rv, dmat, qdec, kdec, cdec, ret_norm_w.reshape(1, RET_VW))


def _mix_kernel(h_ref, mod_ref, nw_ref, onsa_ref, oret_ref, wrg_ref, wga_ref, wgb_ref, wn_ref, wr_ref, wo_ref, o_ref):
    x = h_ref[0]
    sh = mod_ref[0, 3:4, :]
    sc = mod_ref[0, 4:5, :]
    gt = mod_ref[0, 5:6, :]
    u = _norm_mod(x, nw_ref[...], sc, sh).astype(BF16)
    rg = _dot_nt(u, wrg_ref[...])
    y_nsa = _dot(onsa_ref[0], wn_ref[...])
    ga = jax.nn.sigmoid(_dot_nt(u, wga_ref[...]))
    oret = (oret_ref[0] * _silu(rg)).astype(BF16)
    gb = jax.nn.sigmoid(_dot_nt(u, wgb_ref[...]))
    y_ret = _dot(oret, wr_ref[...])
    mixed = (ga * y_nsa + gb * y_ret).astype(BF16)
    o_ref[0] = x + gt * _dot(mixed, wo_ref[...])


def _mix(h, mod, norm_w, o_nsa, o_ret, w_in, w_nsa_up, w_ret_up, w_out, *, tm):
    B, S, D = h.shape
    o = np.cumsum([0, NSA_QW] + [NSA_KVW] * 6 + [3 * NSA_HEADS, RET_QW, RET_QW, RET_VW, RET_VW, D, D])
    wt = w_in.T
    wrg = wt[o[11]:o[12]].astype(BF16)
    wga = wt[o[12]:o[13]].astype(BF16)
    wgb = wt[o[13]:o[14]].astype(BF16)
    wn, wr, wo = w_nsa_up.astype(BF16), w_ret_up.astype(BF16), w_out.astype(BF16)
    tok = lambda w: pl.BlockSpec((1, tm, w), lambda b, i: (b, i, 0))
    return pl.pallas_call(
        _mix_kernel,
        grid=(B, S // tm),
        in_specs=[tok(D), pl.BlockSpec((1, 9, D), lambda b, i: (b, 0, 0)), _full_spec((1, D)),
                  tok(NSA_QW), tok(RET_VW),
                  _full_spec(wrg.shape), _full_spec(wga.shape), _full_spec(wgb.shape),
                  _full_spec(wn.shape), _full_spec(wr.shape), _full_spec(wo.shape)],
        out_specs=tok(D),
        out_shape=jax.ShapeDtypeStruct((B, S, D), F32),
        compiler_params=_params(("arbitrary", "arbitrary")),
        name="mix",
    )(h, mod, norm_w.reshape(1, D), o_nsa, o_ret, wrg, wga, wgb, wn, wr, wo)


def kernel(x, c, ada_w, ada_b, norm1_w, ffn1_w_in, ffn1_w_out, norm2_w, w_in, cmp_k_pe, cmp_k_w1, cmp_k_w2,
           cmp_v_pe, cmp_v_w1, cmp_v_w2, ret_norm_w, w_nsa_up, w_ret_up, w_out, norm3_w, ffn2_w_in, ffn2_w_out,
           final_norm_w):
    B, S, D = x.shape
    depth = ada_w.shape[0]
    assert depth >= 1
    tm = min(TOKEN_TILE, S)
    tm_ffn = min(FFN_TOKEN_TILE, S)
    h = x
    for l in range(depth):
        last = l == depth - 1
        mod = _modulation(c, ada_w[l], ada_b[l]).reshape(B, 9, D)
        h = _ffn(h, mod, norm1_w[l], final_norm_w, ffn1_w_in[l], ffn1_w_out[l], mod_base=0, final_norm=False, tm=tm_ffn)
        q, cv, kaug, vst, kw, vwt, gates, rq, rk, rv = _projection(h, mod, norm2_w[l], w_in[l], tm=tm)
        kvc, kvct = _compress(cv, (cmp_k_pe[l], cmp_v_pe[l]), (cmp_k_w1[l], cmp_v_w1[l]), (cmp_k_w2[l], cmp_v_w2[l]))
        qaug, ocmp = _select(q, kvc, kvct, gates)
        o_nsa = _flash(qaug, kaug, vst, kw, vwt, gates, ocmp)
        o_ret = _retention(rq, rk, rv, ret_norm_w[l])
        h = _mix(h, mod, norm2_w[l], o_nsa, o_ret, w_in[l], w_nsa_up[l], w_ret_up[l], w_out[l], tm=tm)
        h = _ffn(h, mod, norm3_w[l], final_norm_w, ffn2_w_in[l], ffn2_w_out[l], mod_base=6, final_norm=last, tm=tm_ffn)
    return h
```

```python
import functools

import numpy as np
import jax
import jax.numpy as jnp
from jax import lax
from jax.experimental import pallas as pl
from jax.experimental.pallas import tpu as pltpu

F32 = jnp.float32
BF16 = jnp.bfloat16

NSA_HEADS = 8
NSA_GROUPS = 2
NSA_HPG = NSA_HEADS // NSA_GROUPS
NSA_DH = 64
CMP_STRIDE = 16
CMP_LEN = 2 * CMP_STRIDE
CMP_HIDDEN = 128
SEL_BLOCK = 64
SEL_TOPK = 16
WINDOW = 512
FORCE_BONUS = 1e4
RET_HEADS = 8
RET_DK = 64
RET_DV = 64
ROPE_BASE = 10000.0
D_FF = 2816
EPS = 1e-6
NEG = -1e30

NSA_QW = NSA_HEADS * NSA_DH
NSA_KVW = NSA_GROUPS * NSA_DH
RET_QW = RET_HEADS * RET_DK
RET_VW = RET_HEADS * RET_DV

LANES = 128
HALF = LANES // 2
VMEM_LIMIT = 56 * 1024 * 1024
FFN_CHUNK = 256
TOKEN_TILE = 512
FFN_TOKEN_TILE = 1024
RET_CHUNK = 256
RET_CHUNKS_PER_STEP = 2
ATT_Q = 256
VAL_ROWS = NSA_DH + 16
LOG2E = 1.4426950408889634
FLASH_SUBTILES = 2
RUN_AHEAD = 6
LOOP_TILES = 2
RANK_SPAN = 16
LOOP_AHEAD = 2


def _dot(a, b):
    return jnp.dot(a, b, preferred_element_type=F32)


def _dot_nt(a, b):
    return lax.dot_general(a, b, (((1,), (1,)), ((), ())), preferred_element_type=F32)


def _dot_tn(a, b):
    return lax.dot_general(a, b, (((0,), (0,)), ((), ())), preferred_element_type=F32)


def _silu(a):
    return a * jax.nn.sigmoid(a)


def _norm_mod(x, nw, sc, sh):
    ms = jnp.mean(x * x, axis=-1, keepdims=True)
    gain = nw * (1.0 + sc)
    return (x * lax.rsqrt(ms + EPS)) * gain + sh


def _full_spec(shape):
    zeros = (0,) * len(shape)
    return pl.BlockSpec(shape, lambda *_: zeros)


def _resident_spec(shape):
    zeros = (0,) * len(shape)
    return pl.BlockSpec(shape, lambda *_: zeros, pipeline_mode=pl.Buffered(1))


def _params(sem):
    return pltpu.CompilerParams(dimension_semantics=sem, vmem_limit_bytes=VMEM_LIMIT)


def _mod_kernel(c_ref, w_ref, b_ref, o_ref):
    c = c_ref[...]
    o_ref[...] = _dot(_silu(c).astype(BF16), w_ref[...].astype(BF16)) + b_ref[...]


def _modulation(c, ada_w, ada_b):
    B, D = c.shape
    N = ada_w.shape[1]
    tn = N // 8
    return pl.pallas_call(
        _mod_kernel,
        grid=(N // tn,),
        in_specs=[pl.BlockSpec((B, D), lambda j: (0, 0)),
                  pl.BlockSpec((D, tn), lambda j: (0, j)),
                  pl.BlockSpec((1, tn), lambda j: (0, j))],
        out_specs=pl.BlockSpec((B, tn), lambda j: (0, j)),
        out_shape=jax.ShapeDtypeStruct((B, N), F32),
        compiler_params=_params(("arbitrary",)),
        name="modulation",
    )(c, ada_w, ada_b.reshape(1, N))


def _ffn_kernel(x_ref, mod_ref, nw_ref, fw_ref, wi_ref, wo_ref, o_ref, u_scr, h_scr,
                *, mod_base, final_norm):
    x = x_ref[0]
    sh = mod_ref[0, mod_base:mod_base + 1, :]
    sc = mod_ref[0, mod_base + 1:mod_base + 2, :]
    gt = mod_ref[0, mod_base + 2:mod_base + 3, :]
    u_scr[...] = _norm_mod(x, nw_ref[...], sc, sh).astype(BF16)
    dff = wo_ref.shape[0]
    for c0 in range(0, dff, FFN_CHUNK):
        u = u_scr[...]
        a = _dot(u, wi_ref[:, c0:c0 + FFN_CHUNK])
        b = _dot(u, wi_ref[:, dff + c0:dff + c0 + FFN_CHUNK])
        h_scr[:, c0:c0 + FFN_CHUNK] = (_silu(a) * b).astype(BF16)
    y = x + 0.5 * gt * _dot(h_scr[...], wo_ref[...])
    if final_norm:
        ms = jnp.mean(y * y, axis=-1, keepdims=True)
        y = y * lax.rsqrt(ms + EPS) * fw_ref[...]
    o_ref[0] = y


def _ffn(x, mod, norm_w, final_w, w_in, w_out, *, mod_base, final_norm, tm):
    B, S, D = x.shape
    assert w_out.shape[0] % FFN_CHUNK == 0
    wi = w_in.astype(BF16)
    wo = w_out.astype(BF16)
    kern = functools.partial(_ffn_kernel, mod_base=mod_base, final_norm=final_norm)
    return pl.pallas_call(
        kern,
        grid=(B, S // tm),
        in_specs=[pl.BlockSpec((1, tm, D), lambda b, i: (b, i, 0)),
                  pl.BlockSpec((1, 9, D), lambda b, i: (b, 0, 0)),
                  _full_spec((1, D)), _full_spec((1, D)),
                  _resident_spec(wi.shape), _resident_spec(wo.shape)],
        out_specs=pl.BlockSpec((1, tm, D), lambda b, i: (b, i, 0)),
        out_shape=jax.ShapeDtypeStruct((B, S, D), F32),
        scratch_shapes=[pltpu.VMEM((tm, D), BF16), pltpu.VMEM((tm, w_out.shape[0]), BF16)],
        compiler_params=_params(("arbitrary", "arbitrary")),
        name="ffn_final" if final_norm else "ffn",
    )(x, mod, norm_w.reshape(1, D), final_w.reshape(1, D), wi, wo)


def _proj_kernel(h_ref, mod_ref, nw_ref, wq_ref, wkv_ref, wg_ref, wr_ref, cos_ref, sin_ref,
                 q_ref, cv_ref, kaug_ref, vst_ref, kw_ref, vwt_ref, g_ref, rq_ref, rk_ref, rv_ref,
                 u_scr, cv_scr):
    tm = h_ref.shape[1]
    x = h_ref[0]
    sh = mod_ref[0, 3:4, :]
    sc = mod_ref[0, 4:5, :]
    u_scr[...] = _norm_mod(x, nw_ref[...], sc, sh).astype(BF16)
    u = u_scr[...]

    q_ref[0] = _dot_nt(u, wq_ref[...]).astype(BF16)

    kv = _dot_nt(u, wkv_ref[...])
    for t in range(2):
        cv_scr[t] = kv[:, t * LANES:(t + 1) * LANES]
        for tok in range(CMP_STRIDE):
            rows = cv_scr[t, pl.ds(tok, tm // CMP_STRIDE, stride=CMP_STRIDE), :]
            cv_ref[t, 0, :, tok * LANES:(tok + 1) * LANES] = rows.astype(BF16)
    ks = kv[:, 2 * LANES:3 * LANES]
    lane = lax.broadcasted_iota(jnp.int32, (tm, LANES), 1)
    row = lax.broadcasted_iota(jnp.int32, (tm, LANES), 0)
    low = lane < HALF
    blk = (pl.program_id(1) * tm + row) // SEL_BLOCK
    onehot = jnp.where(lane - HALF == blk, 1.0, 0.0)
    kaug_ref[0, 0] = jnp.where(low, ks, onehot).astype(BF16)
    kaug_ref[0, 1] = jnp.where(low, pltpu.roll(ks, HALF, 1), onehot).astype(BF16)
    kw_ref[0] = kv[:, 4 * LANES:5 * LANES].astype(BF16)
    kt = vst_ref.shape[4]
    ones = jnp.ones((VAL_ROWS - NSA_DH, tm), F32)
    for ref, col in ((vst_ref, 3), (vwt_ref, 5)):
        vt = kv[:, col * LANES:(col + 1) * LANES].T
        for g in range(NSA_GROUPS):
            aug = jnp.concatenate([vt[g * HALF:(g + 1) * HALF], ones], axis=0).astype(BF16)
            for c in range(tm // kt):
                ref[0, g, c] = aug[:, c * kt:(c + 1) * kt]

    g_ref[0] = jax.nn.sigmoid(_dot_nt(u, wg_ref[...]))

    r = _dot_nt(u, wr_ref[...])
    cos = cos_ref[...]
    sin = sin_ref[...]
    first = (lane & (RET_DK // 2)) == 0
    npair = RET_QW // LANES
    for j in range(2 * npair):
        xb = r[:, j * LANES:(j + 1) * LANES]
        partner = jnp.where(first, pltpu.roll(xb, LANES - RET_DK // 2, 1), pltpu.roll(xb, RET_DK // 2, 1))
        y = (xb * cos + partner * sin).astype(BF16)
        if j < npair:
            rq_ref[0, :, j * LANES:(j + 1) * LANES] = y
        else:
            rk_ref[0, :, (j - npair) * LANES:(j - npair + 1) * LANES] = y
    rv_ref[0] = r[:, 2 * RET_QW:2 * RET_QW + RET_VW].astype(BF16)


def _projection(h, mod, norm_w, w_in, *, tm):
    B, S, D = h.shape
    o = np.cumsum([0, NSA_QW] + [NSA_KVW] * 6 + [3 * NSA_HEADS, RET_QW, RET_QW, RET_VW])
    scale_q = NSA_DH ** -0.5 * LOG2E
    scale_k = RET_DK ** -0.5
    wt = w_in.T
    wq = (wt[o[0]:o[1]] * scale_q).astype(BF16)
    wkv = wt[o[1]:o[7]].astype(BF16)
    wg = jnp.pad(wt[o[7]:o[8]], ((0, LANES - 3 * NSA_HEADS), (0, 0))).astype(BF16)
    wr = jnp.concatenate([wt[o[8]:o[9]], wt[o[9]:o[10]] * scale_k, wt[o[10]:o[11]]], axis=0).astype(BF16)
    half = RET_DK // 2
    lane = np.arange(LANES)
    inv = ROPE_BASE ** (-np.arange(half, dtype=np.float64) / half)
    ang = np.arange(S, dtype=np.float64)[:, None] * inv[lane % half][None, :]
    cos_t = jnp.asarray(np.cos(ang), F32)
    sin_t = jnp.asarray(np.where((lane % RET_DK) < half, -np.sin(ang), np.sin(ang)), F32)
    tok = lambda w: pl.BlockSpec((1, tm, w), lambda b, i: (b, i, 0))
    kt = min(ATT_Q, S)
    vt_spec = pl.BlockSpec((1, NSA_GROUPS, tm // kt, VAL_ROWS, kt), lambda b, i: (b, 0, i, 0, 0))
    vt_shape = jax.ShapeDtypeStruct((B, NSA_GROUPS, S // kt, VAL_ROWS, kt), BF16)
    outs = pl.pallas_call(
        _proj_kernel,
        grid=(B, S // tm),
        in_specs=[tok(D), pl.BlockSpec((1, 9, D), lambda b, i: (b, 0, 0)), _full_spec((1, D)),
                  _full_spec(wq.shape), _full_spec(wkv.shape), _full_spec(wg.shape), _full_spec(wr.shape),
                  pl.BlockSpec((tm, LANES), lambda b, i: (i, 0)), pl.BlockSpec((tm, LANES), lambda b, i: (i, 0))],
        out_specs=[tok(NSA_QW),
                   pl.BlockSpec((2, 1, tm // CMP_STRIDE, CMP_STRIDE * LANES), lambda b, i: (0, b, i, 0)),
                   pl.BlockSpec((1, NSA_GROUPS, tm, LANES), lambda b, i: (b, 0, i, 0)),
                   vt_spec, tok(LANES), vt_spec, tok(LANES),
                   tok(RET_QW), tok(RET_QW), tok(RET_VW)],
        out_shape=[jax.ShapeDtypeStruct((B, S, NSA_QW), BF16),
                   jax.ShapeDtypeStruct((2, B, S // CMP_STRIDE, CMP_STRIDE * LANES), BF16),
                   jax.ShapeDtypeStruct((B, NSA_GROUPS, S, LANES), BF16),
                   vt_shape,
                   jax.ShapeDtypeStruct((B, S, LANES), BF16),
                   vt_shape,
                   jax.ShapeDtypeStruct((B, S, LANES), F32),
                   jax.ShapeDtypeStruct((B, S, RET_QW), BF16),
                   jax.ShapeDtypeStruct((B, S, RET_QW), BF16),
                   jax.ShapeDtypeStruct((B, S, RET_VW), BF16)],
        scratch_shapes=[pltpu.VMEM((tm, D), BF16), pltpu.VMEM((2, tm, LANES), F32)],
        compiler_params=_params(("arbitrary", "arbitrary")),
        name="projection",
    )(h, mod, norm_w.reshape(1, D), wq, wkv, wg, wr, cos_t, sin_t)
    return outs


def _compress_kernel(ch_ref, w1c_ref, pe_ref, w1_ref, w2_ref, o_ref, ot_ref):
    ch = ch_ref[0, 0]
    nc = ch.shape[0]
    ab = _dot(ch, w1c_ref[0])
    pt = _dot(pe_ref[0], w1_ref[0])[0:1, :]
    pt2 = jnp.concatenate([pt, pt], axis=1)
    nh = NSA_GROUPS * CMP_HIDDEN
    hid = ab[:, :nh] + pltpu.roll(ab[:, nh:], nc - 1, 0) + pt2
    out = _dot(_silu(hid).astype(BF16), w2_ref[0])
    row = lax.broadcasted_iota(jnp.int32, out.shape, 0)
    out = jnp.where(row < nc - 1, out, 0.0)
    o_ref[0, 0] = out.astype(BF16)
    ot_ref[0, 0, 0:LANES, :] = out.T.astype(BF16)
    ot_ref[0, 0, LANES:, :] = jnp.ones((ot_ref.shape[2] - LANES, nc), BF16)


def _compress(cv, pe, w1, w2):
    _, B, nc, width = cv.shape
    chunks = cv
    eye = jnp.eye(NSA_GROUPS, dtype=F32)
    nh = NSA_GROUPS * CMP_HIDDEN

    def expand(w):
        w = w.reshape(CMP_STRIDE, NSA_DH, CMP_HIDDEN)
        return jnp.einsum("ldj,gh->lgdhj", w, eye).reshape(width, nh)

    half = CMP_STRIDE * NSA_DH
    w1c = jnp.stack([jnp.concatenate([expand(w[:half]), expand(w[half:])], axis=1) for w in w1]).astype(BF16)
    w2bd = jnp.stack([jnp.einsum("jd,gh->gjhd", w, eye).reshape(nh, NSA_KVW) for w in w2]).astype(BF16)
    pe8 = jnp.stack([jnp.broadcast_to(p.reshape(1, CMP_LEN * NSA_DH), (8, CMP_LEN * NSA_DH)) for p in pe]).astype(BF16)
    w1s = jnp.stack(w1).astype(BF16)
    sel = lambda shape: pl.BlockSpec((1,) + shape, lambda t, b: (t,) + (0,) * len(shape))
    return pl.pallas_call(
        _compress_kernel,
        grid=(2, B),
        in_specs=[pl.BlockSpec((1, 1, nc, width), lambda t, b: (t, b, 0, 0)),
                  sel(w1c.shape[1:]), sel(pe8.shape[1:]), sel(w1s.shape[1:]), sel(w2bd.shape[1:])],
        out_specs=[pl.BlockSpec((1, 1, nc, LANES), lambda t, b: (t, b, 0, 0)),
                   pl.BlockSpec((1, 1, LANES + 16, nc), lambda t, b: (t, b, 0, 0))],
        out_shape=[jax.ShapeDtypeStruct((2, B, nc, LANES), BF16),
                   jax.ShapeDtypeStruct((2, B, LANES + 16, nc), BF16)],
        compiler_params=_params(("arbitrary", "arbitrary")),
        name="compress",
    )(chunks, w1c, pe8, w1s, w2bd)


def _select_kernel(q_ref, kc_ref, vct_ref, g_ref, ovl_ref, qaug_ref, ocmp_ref, imp_scr, rank_scr, cap_scr,
                   *, n_sel, n_tiles):
    Q = q_ref.shape[1]
    NC = kc_ref.shape[2]
    NB = ovl_ref.shape[0]
    qi = pl.program_id(1)
    q0 = qi * Q
    gt = g_ref[0].T
    any_allowed = q0 + lax.broadcasted_iota(jnp.int32, (1, Q), 1) >= CMP_LEN - 1
    j_blk = lax.broadcasted_iota(jnp.int32, (NB, Q), 0)
    t_blk = q0 + lax.broadcasted_iota(jnp.int32, (NB, Q), 1)
    cur = t_blk // SEL_BLOCK
    forced = (j_blk == 0) | (j_blk == cur) | (j_blk == cur - 1)
    valid = j_blk * SEL_BLOCK <= t_blk
    j_sub = lax.broadcasted_iota(jnp.int32, (8, Q), 0)
    per_tile = Q // SEL_BLOCK

    def compressed_branch(rows):
        kcf = kc_ref[0, 0, 0:rows, :].astype(F32)
        kcr = pltpu.roll(kcf, HALF, 1)
        lowk = lax.broadcasted_iota(jnp.int32, (rows, LANES), 1) < HALF
        k_even = [jnp.where(lowk, kcf, 0.0).astype(BF16), jnp.where(lowk, kcr, 0.0).astype(BF16)]
        k_odd = [jnp.where(lowk, 0.0, kcr).astype(BF16), jnp.where(lowk, 0.0, kcf).astype(BF16)]
        vct = vct_ref[0, 0, :, 0:rows]
        ovl = ovl_ref[:, 0:rows]
        n_row = lax.broadcasted_iota(jnp.int32, (rows, Q), 0)
        t_col = q0 + lax.broadcasted_iota(jnp.int32, (rows, Q), 1)
        cap_scr[0:rows, :] = jnp.where(n_row * CMP_STRIDE + (CMP_LEN - 1) <= t_col, -NEG, NEG)

        scores = []
        for pb in range(NSA_HEADS // 2):
            g, pp = divmod(pb, NSA_HPG // 2)
            qt = q_ref[0, :, pb * LANES:(pb + 1) * LANES].astype(F32).T.astype(BF16)
            for e, kk in enumerate((k_even[g], k_odd[g])):
                scores.append(_dot(kk, qt))
                col = (2 * pp + e) * Q
                qaug_ref[0, g, 0, 0:HALF, col:col + Q] = qt[e * HALF:(e + 1) * HALF]

        for g in range(NSA_GROUPS):
            imp = jnp.zeros((NB, Q), F32)
            for hh in range(NSA_HPG):
                h = g * NSA_HPG + hh
                s = jnp.minimum(scores[h], cap_scr[0:rows, :])
                mx = jnp.max(s, axis=0, keepdims=True)
                ex = jnp.exp2(s - mx).astype(BF16)
                oa = _dot(vct, ex)
                inv = jnp.where(any_allowed, 1.0 / oa[LANES:LANES + 1], 0.0)
                ocmp_ref[0, 0, h * NSA_DH:(h + 1) * NSA_DH, :] = (gt[3 * h:3 * h + 1] * inv) * oa[g * HALF:(g + 1) * HALF]
                imp = imp + _dot(ovl, ex) * inv

            imp = jnp.where(forced, imp + FORCE_BONUS, imp)
            imp_scr[g] = jnp.where(valid, imp, -FORCE_BONUS)

    if n_tiles % 2 == 0 and NC % 32 == 0:
        @pl.when(qi < n_tiles // 2)
        def _():
            compressed_branch(NC // 2)

        @pl.when(qi >= n_tiles // 2)
        def _():
            compressed_branch(NC)
    else:
        compressed_branch(NC)
    rank_scr[...] = jnp.zeros_like(rank_scr)

    for c0 in range(0, NB, RANK_SPAN):
        for b0 in range(0, NB, RANK_SPAN):
            @pl.when(max(c0, b0) // per_tile <= qi)
            def _():
                slabs = range(b0 // 8, (b0 + RANK_SPAN) // 8)
                for g in range(NSA_GROUPS):
                    vals = [imp_scr[g, 8 * sb:8 * sb + 8, :] for sb in slabs]
                    cnts = [rank_scr[g, 8 * sb:8 * sb + 8, :] for sb in slabs]
                    for jp in range(c0, c0 + RANK_SPAN):
                        r = jnp.broadcast_to(imp_scr[g, jp:jp + 1, :], (8, Q))
                        for n, sb in enumerate(slabs):
                            v = vals[n]
                            if 8 * sb > jp:
                                ahead = jnp.where(r >= v, 1, 0)
                            elif 8 * sb + 7 <= jp:
                                ahead = jnp.where(r > v, 1, 0)
                            else:
                                ahead = jnp.where(j_sub + 8 * sb > jp, jnp.where(r >= v, 1, 0), jnp.where(r > v, 1, 0))
                            cnts[n] = cnts[n] + ahead
                    for n, sb in enumerate(slabs):
                        rank_scr[g, 8 * sb:8 * sb + 8, :] = cnts[n]

    for g in range(NSA_GROUPS):
        sel = (rank_scr[g] < n_sel) & valid
        bias = jnp.where(sel, 0.0, NEG).astype(BF16)
        for hh in range(NSA_HPG):
            qaug_ref[0, g, 0, HALF:HALF + NB, hh * Q:(hh + 1) * Q] = bias


def _select(q, kc, vct, gates):
    B, S, _ = q.shape
    Q = min(ATT_Q, S)
    NC = kc.shape[2]
    nc, ns = NC - 1, S // SEL_BLOCK
    NB = LANES - HALF
    assert ns <= NB and Q % SEL_BLOCK == 0
    c_start = np.arange(nc) * CMP_STRIDE
    s_start = np.arange(ns) * SEL_BLOCK
    overlap = ((c_start[:, None] < s_start[None, :] + SEL_BLOCK) &
               (c_start[:, None] + CMP_LEN > s_start[None, :])).astype(np.float32)
    ovl = np.zeros((NB, NC), np.float32)
    ovl[:ns, :nc] = overlap.T
    kern = functools.partial(_select_kernel, n_sel=min(SEL_TOPK, ns), n_tiles=S // Q)
    return pl.pallas_call(
        kern,
        grid=(B, S // Q),
        in_specs=[pl.BlockSpec((1, Q, NSA_QW), lambda b, i: (b, i, 0)),
                  pl.BlockSpec((1, 1, NC, LANES), lambda b, i: (0, b, 0, 0)),
                  pl.BlockSpec((1, 1) + vct.shape[2:], lambda b, i: (1, b, 0, 0)),
                  pl.BlockSpec((1, Q, LANES), lambda b, i: (b, i, 0)),
                  _full_spec((NB, NC))],
        out_specs=[pl.BlockSpec((1, NSA_GROUPS, 1, LANES, NSA_HPG * Q), lambda b, i: (b, 0, i, 0, 0)),
                   pl.BlockSpec((1, 1, NSA_QW, Q), lambda b, i: (b, i, 0, 0))],
        out_shape=[jax.ShapeDtypeStruct((B, NSA_GROUPS, S // Q, LANES, NSA_HPG * Q), BF16),
                   jax.ShapeDtypeStruct((B, S // Q, NSA_QW, Q), F32)],
        scratch_shapes=[pltpu.VMEM((NSA_GROUPS, NB, Q), F32), pltpu.VMEM((NSA_GROUPS, NB, Q), jnp.int32),
                        pltpu.VMEM((NC, Q), F32)],
        compiler_params=_params(("arbitrary", "arbitrary")),
        name="select",
    )(q, kc, vct, gates, jnp.asarray(ovl, BF16))


def _flash_kernel(qaug_ref, kaug_ref, vst_ref, kw_ref, vwt_ref, g_ref, ocmp_ref, o_ref,
                  m_scr, acc_scr, qw_scr, cap_scr, s_scr):
    QS = qaug_ref.shape[2]
    R = qaug_ref.shape[4]
    Q = R // NSA_HPG
    KT = kaug_ref.shape[2] // vst_ref.shape[2]
    WT = WINDOW // KT
    SLC, WIN = 0, 1
    CAUSAL, EDGE = 0, 1
    t0 = pl.program_id(1) * QS
    k_row = lax.broadcasted_iota(jnp.int32, (KT, Q), 0)
    i_col = lax.broadcasted_iota(jnp.int32, (KT, Q), 1)
    big = -NEG
    cap_scr[CAUSAL] = jnp.where(k_row <= i_col, big, NEG)
    cap_scr[EDGE] = jnp.where(k_row > i_col, big, NEG)
    m_scr[...] = jnp.full_like(m_scr, NEG)
    acc_scr[...] = jnp.zeros_like(acc_scr)
    zeros = jnp.zeros((HALF, R), BF16)
    for qs in range(QS):
        qw_scr[qs, 0] = jnp.concatenate([qaug_ref[0, 0, qs, 0:HALF, :], zeros], axis=0)
        qw_scr[qs, 1] = jnp.concatenate([zeros, qaug_ref[0, 1, qs, 0:HALF, :]], axis=0)

    def scores(ch):
        br, qs, g, hh, kj, _ = ch
        off = pl.multiple_of(kj * KT, KT)
        if br == SLC:
            return _dot(kaug_ref[0, g, pl.ds(off, KT), :], qaug_ref[0, g, qs, :, hh * Q:(hh + 1) * Q])
        return _dot(kw_ref[0, pl.ds(off, KT), :], qw_scr[qs, g, :, hh * Q:(hh + 1) * Q])

    def absorb(ch, s):
        br, qs, g, hh, kj, cap = ch
        cols = slice(hh * Q, (hh + 1) * Q)
        if cap is not None:
            s = jnp.minimum(s, cap_scr[cap])
        m_prev = m_scr[br, qs, g, :, cols]
        m_new = jnp.maximum(m_prev, jnp.max(s, axis=0, keepdims=True))
        p = jnp.exp2(s - m_new)
        alpha = jnp.exp2(m_prev - m_new)
        vt = vst_ref[0, g, kj] if br == SLC else vwt_ref[0, g, kj]
        acc_scr[br, qs, g, :, cols] = alpha * acc_scr[br, qs, g, :, cols] + _dot(vt, p.astype(BF16))
        m_scr[br, qs, g, :, cols] = m_new

    NSH = QS * NSA_HEADS
    RING = 2 * NSH

    def run(chains, ready=()):
        order = list(ready) + list(chains)
        assert len(ready) + RUN_AHEAD < RING
        for t in range(-RUN_AHEAD, len(order)):
            n = t + RUN_AHEAD
            if len(ready) <= n < len(order):
                s_scr[(n % RING) // NSH, n % NSH] = scores(order[n])
            if t >= 0:
                absorb(order[t], s_scr[(t % RING) // NSH, t % NSH])

    def tile(br, qs, kj, cap):
        return [(br, qs, g, hh, kj, cap) for g in range(NSA_GROUPS) for hh in range(NSA_HPG)]

    def shared(kj, first_cap=None):
        return [ch for qs in range(QS) for ch in tile(SLC, qs, kj, first_cap if qs == 0 else None)]

    def issue(kj, slot, n):
        s_scr[slot, n] = scores(shared(kj)[n])

    for n in range(NSH):
        issue(0, 0, n)

    def pipelined(first, count):
        produce = [(first + k + 1, (k + 1) % 2, n) for k in range(count) for n in range(NSH)]
        absorbs = [(first + k, k % 2, n) for k in range(count) for n in range(NSH)]
        for t in range(-LOOP_AHEAD, len(absorbs)):
            if t + LOOP_AHEAD < len(produce):
                issue(*produce[t + LOOP_AHEAD])
            if t >= 0:
                kj, slot, n = absorbs[t]
                absorb(shared(kj)[n], s_scr[slot, n])

    def slc_body(j, carry):
        pipelined(LOOP_TILES * j, LOOP_TILES)
        return carry

    lax.fori_loop(0, t0 // LOOP_TILES, slc_body, 0)
    for left in range(2, LOOP_TILES, 2):
        @pl.when(t0 % LOOP_TILES == left)
        def _():
            pipelined(t0 - left, left)
    ready = shared(t0, CAUSAL)

    always = []
    later = {}
    for qs in range(QS):
        for kk in range(1, qs):
            always += tile(SLC, qs, t0 + kk, None)
        if qs > 0:
            always += tile(SLC, qs, t0 + qs, CAUSAL)
        always += tile(WIN, qs, t0 + qs, CAUSAL)
        for back in range(1, WT + 1):
            chains = tile(WIN, qs, t0 + qs - back, EDGE if back == WT else None)
            if back <= qs:
                always += chains
            else:
                later.setdefault(-(-(back - qs) // QS) * QS, []).extend(chains)
    if len(later) == 1:
        (need, chains), = later.items()

        @pl.when(t0 >= need)
        def _():
            run(always + chains, ready)

        @pl.when(t0 < need)
        def _():
            run(always, ready)
    else:
        run(always, ready)
        for need, chains in sorted(later.items()):
            @pl.when(t0 >= need)
            def _():
                run(chains)

    for qs in range(QS):
        gt = g_ref[0, qs * Q:(qs + 1) * Q, :].T
        for pb in range(NSA_HEADS // 2):
            halves = []
            for e in range(2):
                h = 2 * pb + e
                g, hh = divmod(h, NSA_HPG)
                cols = slice(hh * Q, (hh + 1) * Q)
                o = ocmp_ref[0, qs, h * NSA_DH:(h + 1) * NSA_DH, :]
                for br in (SLC, WIN):
                    scale = gt[3 * h + 1 + br:3 * h + 2 + br] / acc_scr[br, qs, g, NSA_DH:NSA_DH + 1, cols]
                    o = o + scale * acc_scr[br, qs, g, 0:NSA_DH, cols]
                halves.append(o)
            o_ref[0, qs * Q:(qs + 1) * Q, pb * LANES:(pb + 1) * LANES] = jnp.concatenate(halves, axis=0).T.astype(BF16)


def _flash(qaug, kaug, vst, kw, vwt, gates, ocmp):
    B, _, NQ, _, R = qaug.shape
    Q = R // NSA_HPG
    S = NQ * Q
    QS = min(FLASH_SUBTILES, NQ)
    assert WINDOW % Q == 0 and vst.shape[4] == Q and NQ % QS == 0 and QS % 2 == 0
    per_b = lambda shape: pl.BlockSpec((1,) + shape, lambda b, i: (b,) + (0,) * len(shape))
    return pl.pallas_call(
        _flash_kernel,
        grid=(B, NQ // QS),
        in_specs=[pl.BlockSpec((1, NSA_GROUPS, QS, LANES, R), lambda b, i: (b, 0, i, 0, 0)),
                  per_b((NSA_GROUPS, S, LANES)), per_b(vst.shape[1:]), per_b((S, LANES)), per_b(vwt.shape[1:]),
                  pl.BlockSpec((1, QS * Q, LANES), lambda b, i: (b, i, 0)),
                  pl.BlockSpec((1, QS, NSA_QW, Q), lambda b, i: (b, i, 0, 0))],
        out_specs=pl.BlockSpec((1, QS * Q, NSA_QW), lambda b, i: (b, i, 0)),
        out_shape=jax.ShapeDtypeStruct((B, S, NSA_QW), BF16),
        scratch_shapes=[pltpu.VMEM((2, QS, NSA_GROUPS, 1, R), F32),
                        pltpu.VMEM((2, QS, NSA_GROUPS, VAL_ROWS, R), F32),
                        pltpu.VMEM((QS, NSA_GROUPS, LANES, R), BF16), pltpu.VMEM((2, Q, Q), F32),
                        pltpu.VMEM((2, QS * NSA_HEADS, Q, Q), F32)],
        compiler_params=_params(("arbitrary", "arbitrary")),
        name="flash",
    )(qaug, kaug, vst, kw, vwt, gates, ocmp)


def _retention_kernel(rq_ref, rk_ref, rv_ref, dmat_ref, qdec_ref, kdec_ref, cdec_ref, nw_ref, o_ref, st_scr):
    C = dmat_ref.shape[1]

    @pl.when(pl.program_id(1) == 0)
    def _():
        st_scr[...] = jnp.zeros_like(st_scr)

    low = lax.broadcasted_iota(jnp.int32, (C, LANES), 1) < HALF
    same_head = ((lax.broadcasted_iota(jnp.int32, (LANES, LANES), 0) < HALF) ==
                 (lax.broadcasted_iota(jnp.int32, (LANES, LANES), 1) < HALF))
    head_mean = jnp.where(same_head, 1.0 / RET_DV, 0.0).astype(BF16)
    n_chunks = rq_ref.shape[1] // C
    pairs = [(cc, p) for cc in range(n_chunks) for p in range(RET_HEADS // 2)]
    rows = lambda cc: slice(cc * C, (cc + 1) * C)
    lanes = lambda p: slice(p * LANES, (p + 1) * LANES)
    state = [st_scr[p] for p in range(RET_HEADS // 2)]
    scores, cross = {}, {}
    for cc, p in pairs:
        q2, k2, v2 = rq_ref[0, rows(cc), lanes(p)], rk_ref[0, rows(cc), lanes(p)], rv_ref[0, rows(cc), lanes(p)]
        q2f = q2.astype(F32)
        qe = jnp.where(low, q2f, 0.0).astype(BF16)
        qo = jnp.where(low, 0.0, q2f).astype(BF16)
        scores[cc, p] = (_dot_nt(qe, k2), _dot_nt(qo, k2))
        cross[cc, p] = _dot(q2, state[p].astype(BF16))
        kd = (k2.astype(F32) * kdec_ref[:, lanes(p)]).astype(BF16)
        state[p] = state[p] * cdec_ref[p:p + 1, :] + jnp.where(same_head, _dot_tn(kd, v2), 0.0)
    for p in range(RET_HEADS // 2):
        st_scr[p] = state[p]
    outs = {}
    for cc, p in pairs:
        v2 = rv_ref[0, rows(cc), lanes(p)]
        ie = (scores[cc, p][0] * dmat_ref[2 * p]).astype(BF16)
        io = (scores[cc, p][1] * dmat_ref[2 * p + 1]).astype(BF16)
        intra = jnp.where(low, _dot(ie, v2), _dot(io, v2))
        outs[cc, p] = intra + cross[cc, p] * qdec_ref[:, lanes(p)]
    means = {key: _dot(o.astype(BF16), head_mean) for key, o in outs.items()}
    devs = {key: outs[key] - means[key] for key in outs}
    variances = {key: _dot((d * d).astype(BF16), head_mean) for key, d in devs.items()}
    for cc, p in pairs:
        o_ref[0, rows(cc), lanes(p)] = devs[cc, p] * lax.rsqrt(variances[cc, p] + EPS) * nw_ref[:, lanes(p)]


def _retention(rq, rk, rv, ret_norm_w):
    B, S, _ = rq.shape
    C = min(RET_CHUNK, S)
    H = RET_HEADS
    log_g = np.log(1.0 - 2.0 ** (-5.0 - np.arange(H, dtype=np.float64)))
    i = np.arange(C, dtype=np.float64)
    diff = i[:, None] - i[None, :]
    as_f32 = lambda a: jnp.asarray(a, F32)
    dmat = as_f32(np.where(diff >= 0, np.exp(log_g[:, None, None] * np.maximum(diff, 0.0)), 0.0))
    per_lane = lambda a: np.repeat(a, RET_DK, axis=-1)
    qdec = as_f32(per_lane(np.exp(log_g[None, :] * (i[:, None] + 1.0))))
    kdec = as_f32(per_lane(np.exp(log_g[None, :] * (C - 1.0 - i[:, None]))))
    cdec = as_f32(per_lane(np.exp(log_g * C)[None, :]).reshape(H // 2, LANES))
    span = C * min(RET_CHUNKS_PER_STEP, S // C)
    tok = lambda w: pl.BlockSpec((1, span, w), lambda b, i: (b, i, 0))
    return pl.pallas_call(
        _retention_kernel,
        grid=(B, S // span),
        in_specs=[tok(RET_QW), tok(RET_QW), tok(RET_VW),
                  _full_spec((H, C, C)), _full_spec((C, RET_QW)), _full_spec((C, RET_QW)),
                  _full_spec((H // 2, LANES)), _full_spec((1, RET_VW))],
        out_specs=tok(RET_VW),
        out_shape=jax.ShapeDtypeStruct((B, S, RET_VW), F32),
        scratch_shapes=[pltpu.VMEM((H // 2, LANES, LANES), F32)],
        compiler_params=_params(("arbitrary", "arbitrary")),
        name="retention",
    )(rq, rk, rv, dmat, qdec, kdec, cdec, ret_norm_w.reshape(1, RET_VW))


def _mix_kernel(h_ref, mod_ref, nw_ref, onsa_ref, oret_ref, wrg_ref, wga_ref, wgb_ref, wn_ref, wr_ref, wo_ref, o_ref):
    x = h_ref[0]
    sh = mod_ref[0, 3:4, :]
    sc = mod_ref[0, 4:5, :]
    gt = mod_ref[0, 5:6, :]
    u = _norm_mod(x, nw_ref[...], sc, sh).astype(BF16)
    rg = _dot_nt(u, wrg_ref[...])
    y_nsa = _dot(onsa_ref[0], wn_ref[...])
    ga = jax.nn.sigmoid(_dot_nt(u, wga_ref[...]))
    oret = (oret_ref[0] * _silu(rg)).astype(BF16)
    gb = jax.nn.sigmoid(_dot_nt(u, wgb_ref[...]))
    y_ret = _dot(oret, wr_ref[...])
    mixed = (ga * y_nsa + gb * y_ret).astype(BF16)
    o_ref[0] = x + gt * _dot(mixed, wo_ref[...])


def _mix(h, mod, norm_w, o_nsa, o_ret, w_in, w_nsa_up, w_ret_up, w_out, *, tm):
    B, S, D = h.shape
    o = np.cumsum([0, NSA_QW] + [NSA_KVW] * 6 + [3 * NSA_HEADS, RET_QW, RET_QW, RET_VW, RET_VW, D, D])
    wt = w_in.T
    wrg = wt[o[11]:o[12]].astype(BF16)
    wga = wt[o[12]:o[13]].astype(BF16)
    wgb = wt[o[13]:o[14]].astype(BF16)
    wn, wr, wo = w_nsa_up.astype(BF16), w_ret_up.astype(BF16), w_out.astype(BF16)
    tok = lambda w: pl.BlockSpec((1, tm, w), lambda b, i: (b, i, 0))
    return pl.pallas_call(
        _mix_kernel,
        grid=(B, S // tm),
        in_specs=[tok(D), pl.BlockSpec((1, 9, D), lambda b, i: (b, 0, 0)), _full_spec((1, D)),
                  tok(NSA_QW), tok(RET_VW),
                  _full_spec(wrg.shape), _full_spec(wga.shape), _full_spec(wgb.shape),
                  _full_spec(wn.shape), _full_spec(wr.shape), _full_spec(wo.shape)],
        out_specs=tok(D),
        out_shape=jax.ShapeDtypeStruct((B, S, D), F32),
        compiler_params=_params(("arbitrary", "arbitrary")),
        name="mix",
    )(h, mod, norm_w.reshape(1, D), o_nsa, o_ret, wrg, wga, wgb, wn, wr, wo)


def kernel(x, c, ada_w, ada_b, norm1_w, ffn1_w_in, ffn1_w_out, norm2_w, w_in, cmp_k_pe, cmp_k_w1, cmp_k_w2,
           cmp_v_pe, cmp_v_w1, cmp_v_w2, ret_norm_w, w_nsa_up, w_ret_up, w_out, norm3_w, ffn2_w_in, ffn2_w_out,
           final_norm_w):
    B, S, D = x.shape
    depth = ada_w.shape[0]
    assert depth >= 1
    tm = min(TOKEN_TILE, S)
    tm_ffn = min(FFN_TOKEN_TILE, S)
    h = x
    for l in range(depth):
        last = l == depth - 1
        mod = _modulation(c, ada_w[l], ada_b[l]).reshape(B, 9, D)
        h = _ffn(h, mod, norm1_w[l], final_norm_w, ffn1_w_in[l], ffn1_w_out[l], mod_base=0, final_norm=False, tm=tm_ffn)
        q, cv, kaug, vst, kw, vwt, gates, rq, rk, rv = _projection(h, mod, norm2_w[l], w_in[l], tm=tm)
        kvc, kvct = _compress(cv, (cmp_k_pe[l], cmp_v_pe[l]), (cmp_k_w1[l], cmp_v_w1[l]), (cmp_k_w2[l], cmp_v_w2[l]))
        qaug, ocmp = _select(q, kvc, kvct, gates)
        o_nsa = _flash(qaug, kaug, vst, kw, vwt, gates, ocmp)
        o_ret = _retention(rq, rk, rv, ret_norm_w[l])
        h = _mix(h, mod, norm2_w[l], o_nsa, o_ret, w_in[l], w_nsa_up[l], w_ret_up[l], w_out[l], tm=tm)
        h = _ffn(h, mod, norm3_w[l], final_norm_w, ffn2_w_in[l], ffn2_w_out[l], mod_base=6, final_norm=last, tm=tm_ffn)
    return h
```

```python
import functools

import numpy as np
import jax
import jax.numpy as jnp
from jax import lax
from jax.experimental import pallas as pl
from jax.experimental.pallas import tpu as pltpu

F32 = jnp.float32
BF16 = jnp.bfloat16

NSA_HEADS = 8
NSA_GROUPS = 2
NSA_HPG = NSA_HEADS // NSA_GROUPS
NSA_DH = 64
CMP_STRIDE = 16
CMP_LEN = 2 * CMP_STRIDE
CMP_HIDDEN = 128
SEL_BLOCK = 64
SEL_TOPK = 16
WINDOW = 512
FORCE_BONUS = 1e4
RET_HEADS = 8
RET_DK = 64
RET_DV = 64
ROPE_BASE = 10000.0
D_FF = 2816
EPS = 1e-6
NEG = -1e30

NSA_QW = NSA_HEADS * NSA_DH
NSA_KVW = NSA_GROUPS * NSA_DH
RET_QW = RET_HEADS * RET_DK
RET_VW = RET_HEADS * RET_DV

LANES = 128
HALF = LANES // 2
VMEM_LIMIT = 56 * 1024 * 1024
FFN_CHUNK = 256
TOKEN_TILE = 512
FFN_TOKEN_TILE = 1024
RET_CHUNK = 256
RET_CHUNKS_PER_STEP = 2
ATT_Q = 256
VAL_ROWS = NSA_DH + 16
LOG2E = 1.4426950408889634
FLASH_SUBTILES = 2
RUN_AHEAD = 6
LOOP_TILES = 2
CMP_PARTS = 4
RANK_SPAN = 16
LOOP_AHEAD = 2


def _dot(a, b):
    return jnp.dot(a, b, preferred_element_type=F32)


def _dot_nt(a, b):
    return lax.dot_general(a, b, (((1,), (1,)), ((), ())), preferred_element_type=F32)


def _dot_tn(a, b):
    return lax.dot_general(a, b, (((0,), (0,)), ((), ())), preferred_element_type=F32)


def _silu(a):
    return a * jax.nn.sigmoid(a)


def _norm_mod(x, nw, sc, sh):
    ms = jnp.mean(x * x, axis=-1, keepdims=True)
    gain = nw * (1.0 + sc)
    return (x * lax.rsqrt(ms + EPS)) * gain + sh


def _full_spec(shape):
    zeros = (0,) * len(shape)
    return pl.BlockSpec(shape, lambda *_: zeros)


def _resident_spec(shape):
    zeros = (0,) * len(shape)
    return pl.BlockSpec(shape, lambda *_: zeros, pipeline_mode=pl.Buffered(1))


def _params(sem):
    return pltpu.CompilerParams(dimension_semantics=sem, vmem_limit_bytes=VMEM_LIMIT)


def _mod_kernel(c_ref, w_ref, b_ref, o_ref):
    c = c_ref[...]
    o_ref[...] = _dot(_silu(c).astype(BF16), w_ref[...].astype(BF16)) + b_ref[...]


def _modulation(c, ada_w, ada_b):
    B, D = c.shape
    N = ada_w.shape[1]
    tn = N // 8
    return pl.pallas_call(
        _mod_kernel,
        grid=(N // tn,),
        in_specs=[pl.BlockSpec((B, D), lambda j: (0, 0)),
                  pl.BlockSpec((D, tn), lambda j: (0, j)),
                  pl.BlockSpec((1, tn), lambda j: (0, j))],
        out_specs=pl.BlockSpec((B, tn), lambda j: (0, j)),
        out_shape=jax.ShapeDtypeStruct((B, N), F32),
        compiler_params=_params(("arbitrary",)),
        name="modulation",
    )(c, ada_w, ada_b.reshape(1, N))


def _ffn_kernel(x_ref, mod_ref, nw_ref, fw_ref, wi_ref, wo_ref, o_ref, u_scr, h_scr,
                *, mod_base, final_norm):
    x = x_ref[0]
    sh = mod_ref[0, mod_base:mod_base + 1, :]
    sc = mod_ref[0, mod_base + 1:mod_base + 2, :]
    gt = mod_ref[0, mod_base + 2:mod_base + 3, :]
    u_scr[...] = _norm_mod(x, nw_ref[...], sc, sh).astype(BF16)
    dff = wo_ref.shape[0]
    for c0 in range(0, dff, FFN_CHUNK):
        u = u_scr[...]
        a = _dot(u, wi_ref[:, c0:c0 + FFN_CHUNK])
        b = _dot(u, wi_ref[:, dff + c0:dff + c0 + FFN_CHUNK])
        h_scr[:, c0:c0 + FFN_CHUNK] = (_silu(a) * b).astype(BF16)
    y = x + 0.5 * gt * _dot(h_scr[...], wo_ref[...])
    if final_norm:
        ms = jnp.mean(y * y, axis=-1, keepdims=True)
        y = y * lax.rsqrt(ms + EPS) * fw_ref[...]
    o_ref[0] = y


def _ffn(x, mod, norm_w, final_w, w_in, w_out, *, mod_base, final_norm, tm):
    B, S, D = x.shape
    assert w_out.shape[0] % FFN_CHUNK == 0
    wi = w_in.astype(BF16)
    wo = w_out.astype(BF16)
    kern = functools.partial(_ffn_kernel, mod_base=mod_base, final_norm=final_norm)
    return pl.pallas_call(
        kern,
        grid=(B, S // tm),
        in_specs=[pl.BlockSpec((1, tm, D), lambda b, i: (b, i, 0)),
                  pl.BlockSpec((1, 9, D), lambda b, i: (b, 0, 0)),
                  _full_spec((1, D)), _full_spec((1, D)),
                  _resident_spec(wi.shape), _resident_spec(wo.shape)],
        out_specs=pl.BlockSpec((1, tm, D), lambda b, i: (b, i, 0)),
        out_shape=jax.ShapeDtypeStruct((B, S, D), F32),
        scratch_shapes=[pltpu.VMEM((tm, D), BF16), pltpu.VMEM((tm, w_out.shape[0]), BF16)],
        compiler_params=_params(("arbitrary", "arbitrary")),
        name="ffn_final" if final_norm else "ffn",
    )(x, mod, norm_w.reshape(1, D), final_w.reshape(1, D), wi, wo)


def _proj_kernel(h_ref, mod_ref, nw_ref, wq_ref, wkv_ref, wg_ref, wr_ref, cos_ref, sin_ref,
                 q_ref, cv_ref, kaug_ref, vst_ref, kw_ref, vwt_ref, g_ref, rq_ref, rk_ref, rv_ref,
                 u_scr, cv_scr):
    tm = h_ref.shape[1]
    x = h_ref[0]
    sh = mod_ref[0, 3:4, :]
    sc = mod_ref[0, 4:5, :]
    u_scr[...] = _norm_mod(x, nw_ref[...], sc, sh).astype(BF16)
    u = u_scr[...]

    q_ref[0] = _dot_nt(u, wq_ref[...]).astype(BF16)

    kv = _dot_nt(u, wkv_ref[...])
    for t in range(2):
        cv_scr[t] = kv[:, t * LANES:(t + 1) * LANES]
        for tok in range(CMP_STRIDE):
            rows = cv_scr[t, pl.ds(tok, tm // CMP_STRIDE, stride=CMP_STRIDE), :]
            cv_ref[t, 0, :, tok * LANES:(tok + 1) * LANES] = rows.astype(BF16)
    ks = kv[:, 2 * LANES:3 * LANES]
    lane = lax.broadcasted_iota(jnp.int32, (tm, LANES), 1)
    row = lax.broadcasted_iota(jnp.int32, (tm, LANES), 0)
    low = lane < HALF
    blk = (pl.program_id(1) * tm + row) // SEL_BLOCK
    onehot = jnp.where(lane - HALF == blk, 1.0, 0.0)
    kaug_ref[0, 0] = jnp.where(low, ks, onehot).astype(BF16)
    kaug_ref[0, 1] = jnp.where(low, pltpu.roll(ks, HALF, 1), onehot).astype(BF16)
    kw_ref[0] = kv[:, 4 * LANES:5 * LANES].astype(BF16)
    kt = vst_ref.shape[4]
    ones = jnp.ones((VAL_ROWS - NSA_DH, tm), F32)
    for ref, col in ((vst_ref, 3), (vwt_ref, 5)):
        vt = kv[:, col * LANES:(col + 1) * LANES].T
        for g in range(NSA_GROUPS):
            aug = jnp.concatenate([vt[g * HALF:(g + 1) * HALF], ones], axis=0).astype(BF16)
            for c in range(tm // kt):
                ref[0, g, c] = aug[:, c * kt:(c + 1) * kt]

    g_ref[0] = jax.nn.sigmoid(_dot_nt(u, wg_ref[...]))

    r = _dot_nt(u, wr_ref[...])
    cos = cos_ref[...]
    sin = sin_ref[...]
    first = (lane & (RET_DK // 2)) == 0
    npair = RET_QW // LANES
    for j in range(2 * npair):
        xb = r[:, j * LANES:(j + 1) * LANES]
        partner = jnp.where(first, pltpu.roll(xb, LANES - RET_DK // 2, 1), pltpu.roll(xb, RET_DK // 2, 1))
        y = (xb * cos + partner * sin).astype(BF16)
        if j < npair:
            rq_ref[0, :, j * LANES:(j + 1) * LANES] = y
        else:
            rk_ref[0, :, (j - npair) * LANES:(j - npair + 1) * LANES] = y
    rv_ref[0] = r[:, 2 * RET_QW:2 * RET_QW + RET_VW].astype(BF16)


def _projection(h, mod, norm_w, w_in, *, tm):
    B, S, D = h.shape
    o = np.cumsum([0, NSA_QW] + [NSA_KVW] * 6 + [3 * NSA_HEADS, RET_QW, RET_QW, RET_VW])
    scale_q = NSA_DH ** -0.5 * LOG2E
    scale_k = RET_DK ** -0.5
    wt = w_in.T
    wq = (wt[o[0]:o[1]] * scale_q).astype(BF16)
    wkv = wt[o[1]:o[7]].astype(BF16)
    wg = jnp.pad(wt[o[7]:o[8]], ((0, LANES - 3 * NSA_HEADS), (0, 0))).astype(BF16)
    wr = jnp.concatenate([wt[o[8]:o[9]], wt[o[9]:o[10]] * scale_k, wt[o[10]:o[11]]], axis=0).astype(BF16)
    half = RET_DK // 2
    lane = np.arange(LANES)
    inv = ROPE_BASE ** (-np.arange(half, dtype=np.float64) / half)
    ang = np.arange(S, dtype=np.float64)[:, None] * inv[lane % half][None, :]
    cos_t = jnp.asarray(np.cos(ang), F32)
    sin_t = jnp.asarray(np.where((lane % RET_DK) < half, -np.sin(ang), np.sin(ang)), F32)
    tok = lambda w: pl.BlockSpec((1, tm, w), lambda b, i: (b, i, 0))
    kt = min(ATT_Q, S)
    vt_spec = pl.BlockSpec((1, NSA_GROUPS, tm // kt, VAL_ROWS, kt), lambda b, i: (b, 0, i, 0, 0))
    vt_shape = jax.ShapeDtypeStruct((B, NSA_GROUPS, S // kt, VAL_ROWS, kt), BF16)
    outs = pl.pallas_call(
        _proj_kernel,
        grid=(B, S // tm),
        in_specs=[tok(D), pl.BlockSpec((1, 9, D), lambda b, i: (b, 0, 0)), _full_spec((1, D)),
                  _full_spec(wq.shape), _full_spec(wkv.shape), _full_spec(wg.shape), _full_spec(wr.shape),
                  pl.BlockSpec((tm, LANES), lambda b, i: (i, 0)), pl.BlockSpec((tm, LANES), lambda b, i: (i, 0))],
        out_specs=[tok(NSA_QW),
                   pl.BlockSpec((2, 1, tm // CMP_STRIDE, CMP_STRIDE * LANES), lambda b, i: (0, b, i, 0)),
                   pl.BlockSpec((1, NSA_GROUPS, tm, LANES), lambda b, i: (b, 0, i, 0)),
                   vt_spec, tok(LANES), vt_spec, tok(LANES),
                   tok(RET_QW), tok(RET_QW), tok(RET_VW)],
        out_shape=[jax.ShapeDtypeStruct((B, S, NSA_QW), BF16),
                   jax.ShapeDtypeStruct((2, B, S // CMP_STRIDE, CMP_STRIDE * LANES), BF16),
                   jax.ShapeDtypeStruct((B, NSA_GROUPS, S, LANES), BF16),
                   vt_shape,
                   jax.ShapeDtypeStruct((B, S, LANES), BF16),
                   vt_shape,
                   jax.ShapeDtypeStruct((B, S, LANES), F32),
                   jax.ShapeDtypeStruct((B, S, RET_QW), BF16),
                   jax.ShapeDtypeStruct((B, S, RET_QW), BF16),
                   jax.ShapeDtypeStruct((B, S, RET_VW), BF16)],
        scratch_shapes=[pltpu.VMEM((tm, D), BF16), pltpu.VMEM((2, tm, LANES), F32)],
        compiler_params=_params(("arbitrary", "arbitrary")),
        name="projection",
    )(h, mod, norm_w.reshape(1, D), wq, wkv, wg, wr, cos_t, sin_t)
    return outs


def _compress_kernel(ch_ref, w1c_ref, pe_ref, w1_ref, w2_ref, o_ref, ot_ref):
    ch = ch_ref[0, 0]
    nc = ch.shape[0]
    ab = _dot(ch, w1c_ref[0])
    pt = _dot(pe_ref[0], w1_ref[0])[0:1, :]
    pt2 = jnp.concatenate([pt, pt], axis=1)
    nh = NSA_GROUPS * CMP_HIDDEN
    hid = ab[:, :nh] + pltpu.roll(ab[:, nh:], nc - 1, 0) + pt2
    out = _dot(_silu(hid).astype(BF16), w2_ref[0])
    row = lax.broadcasted_iota(jnp.int32, out.shape, 0)
    out = jnp.where(row < nc - 1, out, 0.0)
    o_ref[0, 0] = out.astype(BF16)
    ot_ref[0, 0, 0:LANES, :] = out.T.astype(BF16)
    ot_ref[0, 0, LANES:, :] = jnp.ones((ot_ref.shape[2] - LANES, nc), BF16)


def _compress(cv, pe, w1, w2):
    _, B, nc, width = cv.shape
    chunks = cv
    eye = jnp.eye(NSA_GROUPS, dtype=F32)
    nh = NSA_GROUPS * CMP_HIDDEN

    def expand(w):
        w = w.reshape(CMP_STRIDE, NSA_DH, CMP_HIDDEN)
        return jnp.einsum("ldj,gh->lgdhj", w, eye).reshape(width, nh)

    half = CMP_STRIDE * NSA_DH
    w1c = jnp.stack([jnp.concatenate([expand(w[:half]), expand(w[half:])], axis=1) for w in w1]).astype(BF16)
    w2bd = jnp.stack([jnp.einsum("jd,gh->gjhd", w, eye).reshape(nh, NSA_KVW) for w in w2]).astype(BF16)
    pe8 = jnp.stack([jnp.broadcast_to(p.reshape(1, CMP_LEN * NSA_DH), (8, CMP_LEN * NSA_DH)) for p in pe]).astype(BF16)
    w1s = jnp.stack(w1).astype(BF16)
    sel = lambda shape: pl.BlockSpec((1,) + shape, lambda t, b: (t,) + (0,) * len(shape))
    return pl.pallas_call(
        _compress_kernel,
        grid=(2, B),
        in_specs=[pl.BlockSpec((1, 1, nc, width), lambda t, b: (t, b, 0, 0)),
                  sel(w1c.shape[1:]), sel(pe8.shape[1:]), sel(w1s.shape[1:]), sel(w2bd.shape[1:])],
        out_specs=[pl.BlockSpec((1, 1, nc, LANES), lambda t, b: (t, b, 0, 0)),
                   pl.BlockSpec((1, 1, LANES + 16, nc), lambda t, b: (t, b, 0, 0))],
        out_shape=[jax.ShapeDtypeStruct((2, B, nc, LANES), BF16),
                   jax.ShapeDtypeStruct((2, B, LANES + 16, nc), BF16)],
        compiler_params=_params(("arbitrary", "arbitrary")),
        name="compress",
    )(chunks, w1c, pe8, w1s, w2bd)


def _select_kernel(q_ref, kc_ref, vct_ref, g_ref, ovl_ref, qaug_ref, ocmp_ref, imp_scr, rank_scr, cap_scr,
                   *, n_sel, n_tiles):
    Q = q_ref.shape[1]
    NC = kc_ref.shape[2]
    NB = ovl_ref.shape[0]
    qi = pl.program_id(1)
    q0 = qi * Q
    gt = g_ref[0].T
    any_allowed = q0 + lax.broadcasted_iota(jnp.int32, (1, Q), 1) >= CMP_LEN - 1
    j_blk = lax.broadcasted_iota(jnp.int32, (NB, Q), 0)
    t_blk = q0 + lax.broadcasted_iota(jnp.int32, (NB, Q), 1)
    cur = t_blk // SEL_BLOCK
    forced = (j_blk == 0) | (j_blk == cur) | (j_blk == cur - 1)
    valid = j_blk * SEL_BLOCK <= t_blk
    j_sub = lax.broadcasted_iota(jnp.int32, (8, Q), 0)
    per_tile = Q // SEL_BLOCK

    def compressed_branch(rows):
        kcf = kc_ref[0, 0, 0:rows, :].astype(F32)
        kcr = pltpu.roll(kcf, HALF, 1)
        lowk = lax.broadcasted_iota(jnp.int32, (rows, LANES), 1) < HALF
        k_even = [jnp.where(lowk, kcf, 0.0).astype(BF16), jnp.where(lowk, kcr, 0.0).astype(BF16)]
        k_odd = [jnp.where(lowk, 0.0, kcr).astype(BF16), jnp.where(lowk, 0.0, kcf).astype(BF16)]
        vct = vct_ref[0, 0, :, 0:rows]
        ovl = ovl_ref[:, 0:rows]
        n_row = lax.broadcasted_iota(jnp.int32, (rows, Q), 0)
        t_col = q0 + lax.broadcasted_iota(jnp.int32, (rows, Q), 1)
        cap_scr[0:rows, :] = jnp.where(n_row * CMP_STRIDE + (CMP_LEN - 1) <= t_col, -NEG, NEG)

        scores = []
        for pb in range(NSA_HEADS // 2):
            g, pp = divmod(pb, NSA_HPG // 2)
            qt = q_ref[0, :, pb * LANES:(pb + 1) * LANES].astype(F32).T.astype(BF16)
            for e, kk in enumerate((k_even[g], k_odd[g])):
                scores.append(_dot(kk, qt))
                col = (2 * pp + e) * Q
                qaug_ref[0, g, 0, 0:HALF, col:col + Q] = qt[e * HALF:(e + 1) * HALF]

        for g in range(NSA_GROUPS):
            imp = jnp.zeros((NB, Q), F32)
            for hh in range(NSA_HPG):
                h = g * NSA_HPG + hh
                s = jnp.minimum(scores[h], cap_scr[0:rows, :])
                mx = jnp.max(s, axis=0, keepdims=True)
                ex = jnp.exp2(s - mx).astype(BF16)
                oa = _dot(vct, ex)
                inv = jnp.where(any_allowed, 1.0 / oa[LANES:LANES + 1], 0.0)
                ocmp_ref[0, 0, h * NSA_DH:(h + 1) * NSA_DH, :] = (gt[3 * h:3 * h + 1] * inv) * oa[g * HALF:(g + 1) * HALF]
                imp = imp + _dot(ovl, ex) * inv

            imp = jnp.where(forced, imp + FORCE_BONUS, imp)
            imp_scr[g] = jnp.where(valid, imp, -FORCE_BONUS)

    parts = CMP_PARTS if n_tiles % CMP_PARTS == 0 and NC % (16 * CMP_PARTS) == 0 else 1
    for c in range(parts):
        @pl.when((qi >= c * (n_tiles // parts)) & (qi < (c + 1) * (n_tiles // parts)))
        def _():
            compressed_branch((c + 1) * (NC // parts))
    rank_scr[...] = jnp.zeros_like(rank_scr)

    for c0 in range(0, NB, RANK_SPAN):
        @pl.when(c0 // per_tile <= qi)
        def _():
            slabs = range(NB // 8)
            for g in range(NSA_GROUPS):
                vals = [imp_scr[g, 8 * sb:8 * sb + 8, :] for sb in slabs]
                cnts = [rank_scr[g, 8 * sb:8 * sb + 8, :] for sb in slabs]
                for jp in range(c0, c0 + RANK_SPAN):
                    r = jnp.broadcast_to(imp_scr[g, jp:jp + 1, :], (8, Q))
                    for sb in slabs:
                        v = vals[sb]
                        if 8 * sb > jp:
                            ahead = r >= v
                        elif 8 * sb + 7 <= jp:
                            ahead = r > v
                        else:
                            ahead = (r > v) | ((r == v) & (j_sub + 8 * sb > jp))
                        cnts[sb] = jnp.where(ahead, cnts[sb] + 1, cnts[sb])
                for sb in slabs:
                    rank_scr[g, 8 * sb:8 * sb + 8, :] = cnts[sb]

    for g in range(NSA_GROUPS):
        sel = (rank_scr[g] < n_sel) & valid
        bias = jnp.where(sel, 0.0, NEG).astype(BF16)
        for hh in range(NSA_HPG):
            qaug_ref[0, g, 0, HALF:HALF + NB, hh * Q:(hh + 1) * Q] = bias


def _select(q, kc, vct, gates):
    B, S, _ = q.shape
    Q = min(ATT_Q, S)
    NC = kc.shape[2]
    nc, ns = NC - 1, S // SEL_BLOCK
    NB = LANES - HALF
    assert ns <= NB and Q % SEL_BLOCK == 0
    c_start = np.arange(nc) * CMP_STRIDE
    s_start = np.arange(ns) * SEL_BLOCK
    overlap = ((c_start[:, None] < s_start[None, :] + SEL_BLOCK) &
               (c_start[:, None] + CMP_LEN > s_start[None, :])).astype(np.float32)
    ovl = np.zeros((NB, NC), np.float32)
    ovl[:ns, :nc] = overlap.T
    kern = functools.partial(_select_kernel, n_sel=min(SEL_TOPK, ns), n_tiles=S // Q)
    return pl.pallas_call(
        kern,
        grid=(B, S // Q),
        in_specs=[pl.BlockSpec((1, Q, NSA_QW), lambda b, i: (b, i, 0)),
                  pl.BlockSpec((1, 1, NC, LANES), lambda b, i: (0, b, 0, 0)),
                  pl.BlockSpec((1, 1) + vct.shape[2:], lambda b, i: (1, b, 0, 0)),
                  pl.BlockSpec((1, Q, LANES), lambda b, i: (b, i, 0)),
                  _full_spec((NB, NC))],
        out_specs=[pl.BlockSpec((1, NSA_GROUPS, 1, LANES, NSA_HPG * Q), lambda b, i: (b, 0, i, 0, 0)),
                   pl.BlockSpec((1, 1, NSA_QW, Q), lambda b, i: (b, i, 0, 0))],
        out_shape=[jax.ShapeDtypeStruct((B, NSA_GROUPS, S // Q, LANES, NSA_HPG * Q), BF16),
                   jax.ShapeDtypeStruct((B, S // Q, NSA_QW, Q), F32)],
        scratch_shapes=[pltpu.VMEM((NSA_GROUPS, NB, Q), F32), pltpu.VMEM((NSA_GROUPS, NB, Q), jnp.int32),
                        pltpu.VMEM((NC, Q), F32)],
        compiler_params=_params(("arbitrary", "arbitrary")),
        name="select",
    )(q, kc, vct, gates, jnp.asarray(ovl, BF16))


def _flash_kernel(qaug_ref, kaug_ref, vst_ref, kw_ref, vwt_ref, g_ref, ocmp_ref, o_ref,
                  m_scr, acc_scr, qw_scr, cap_scr, s_scr):
    QS = qaug_ref.shape[2]
    R = qaug_ref.shape[4]
    Q = R // NSA_HPG
    KT = kaug_ref.shape[2] // vst_ref.shape[2]
    WT = WINDOW // KT
    SLC, WIN = 0, 1
    CAUSAL, EDGE = 0, 1
    t0 = pl.program_id(1) * QS
    k_row = lax.broadcasted_iota(jnp.int32, (KT, Q), 0)
    i_col = lax.broadcasted_iota(jnp.int32, (KT, Q), 1)
    big = -NEG
    cap_scr[CAUSAL] = jnp.where(k_row <= i_col, big, NEG)
    cap_scr[EDGE] = jnp.where(k_row > i_col, big, NEG)
    m_scr[...] = jnp.full_like(m_scr, NEG)
    acc_scr[...] = jnp.zeros_like(acc_scr)
    zeros = jnp.zeros((HALF, R), BF16)
    for qs in range(QS):
        qw_scr[qs, 0] = jnp.concatenate([qaug_ref[0, 0, qs, 0:HALF, :], zeros], axis=0)
        qw_scr[qs, 1] = jnp.concatenate([zeros, qaug_ref[0, 1, qs, 0:HALF, :]], axis=0)

    def scores(ch):
        br, qs, g, hh, kj, _ = ch
        off = pl.multiple_of(kj * KT, KT)
        if br == SLC:
            return _dot(kaug_ref[0, g, pl.ds(off, KT), :], qaug_ref[0, g, qs, :, hh * Q:(hh + 1) * Q])
        return _dot(kw_ref[0, pl.ds(off, KT), :], qw_scr[qs, g, :, hh * Q:(hh + 1) * Q])

    def absorb(ch, s):
        br, qs, g, hh, kj, cap = ch
        cols = slice(hh * Q, (hh + 1) * Q)
        if cap is not None:
            s = jnp.minimum(s, cap_scr[cap])
        m_prev = m_scr[br, qs, g, :, cols]
        m_new = jnp.maximum(m_prev, jnp.max(s, axis=0, keepdims=True))
        p = jnp.exp2(s - m_new)
        alpha = jnp.exp2(m_prev - m_new)
        vt = vst_ref[0, g, kj] if br == SLC else vwt_ref[0, g, kj]
        acc_scr[br, qs, g, :, cols] = alpha * acc_scr[br, qs, g, :, cols] + _dot(vt, p.astype(BF16))
        m_scr[br, qs, g, :, cols] = m_new

    NSH = QS * NSA_HEADS
    RING = 2 * NSH

    def run(chains, ready=()):
        order = list(ready) + list(chains)
        assert len(ready) + RUN_AHEAD < RING
        for t in range(-RUN_AHEAD, len(order)):
            n = t + RUN_AHEAD
            if len(ready) <= n < len(order):
                s_scr[(n % RING) // NSH, n % NSH] = scores(order[n])
            if t >= 0:
                absorb(order[t], s_scr[(t % RING) // NSH, t % NSH])

    def tile(br, qs, kj, cap):
        return [(br, qs, g, hh, kj, cap) for g in range(NSA_GROUPS) for hh in range(NSA_HPG)]

    def shared(kj, first_cap=None):
        return [ch for qs in range(QS) for ch in tile(SLC, qs, kj, first_cap if qs == 0 else None)]

    def issue(kj, slot, n):
        s_scr[slot, n] = scores(shared(kj)[n])

    for n in range(NSH):
        issue(0, 0, n)

    def pipelined(first, count):
        produce = [(first + k + 1, (k + 1) % 2, n) for k in range(count) for n in range(NSH)]
        absorbs = [(first + k, k % 2, n) for k in range(count) for n in range(NSH)]
        for t in range(-LOOP_AHEAD, len(absorbs)):
            if t + LOOP_AHEAD < len(produce):
                issue(*produce[t + LOOP_AHEAD])
            if t >= 0:
                kj, slot, n = absorbs[t]
                absorb(shared(kj)[n], s_scr[slot, n])

    def slc_body(j, carry):
        pipelined(LOOP_TILES * j, LOOP_TILES)
        return carry

    lax.fori_loop(0, t0 // LOOP_TILES, slc_body, 0)
    for left in range(2, LOOP_TILES, 2):
        @pl.when(t0 % LOOP_TILES == left)
        def _():
            pipelined(t0 - left, left)
    ready = shared(t0, CAUSAL)

    always = []
    later = {}
    for qs in range(QS):
        for kk in range(1, qs):
            always += tile(SLC, qs, t0 + kk, None)
        if qs > 0:
            always += tile(SLC, qs, t0 + qs, CAUSAL)
        always += tile(WIN, qs, t0 + qs, CAUSAL)
        for back in range(1, WT + 1):
            chains = tile(WIN, qs, t0 + qs - back, EDGE if back == WT else None)
            if back <= qs:
                always += chains
            else:
                later.setdefault(-(-(back - qs) // QS) * QS, []).extend(chains)
    if len(later) == 1:
        (need, chains), = later.items()

        @pl.when(t0 >= need)
        def _():
            run(always + chains, ready)

        @pl.when(t0 < need)
        def _():
            run(always, ready)
    else:
        run(always, ready)
        for need, chains in sorted(later.items()):
            @pl.when(t0 >= need)
            def _():
                run(chains)

    for qs in range(QS):
        gt = g_ref[0, qs * Q:(qs + 1) * Q, :].T
        for pb in range(NSA_HEADS // 2):
            halves = []
            for e in range(2):
                h = 2 * pb + e
                g, hh = divmod(h, NSA_HPG)
                cols = slice(hh * Q, (hh + 1) * Q)
                o = ocmp_ref[0, qs, h * NSA_DH:(h + 1) * NSA_DH, :]
                for br in (SLC, WIN):
                    scale = gt[3 * h + 1 + br:3 * h + 2 + br] / acc_scr[br, qs, g, NSA_DH:NSA_DH + 1, cols]
                    o = o + scale * acc_scr[br, qs, g, 0:NSA_DH, cols]
                halves.append(o)
            o_ref[0, qs * Q:(qs + 1) * Q, pb * LANES:(pb + 1) * LANES] = jnp.concatenate(halves, axis=0).T.astype(BF16)


def _flash(qaug, kaug, vst, kw, vwt, gates, ocmp):
    B, _, NQ, _, R = qaug.shape
    Q = R // NSA_HPG
    S = NQ * Q
    QS = min(FLASH_SUBTILES, NQ)
    assert WINDOW % Q == 0 and vst.shape[4] == Q and NQ % QS == 0 and QS % 2 == 0
    per_b = lambda shape: pl.BlockSpec((1,) + shape, lambda b, i: (b,) + (0,) * len(shape))
    return pl.pallas_call(
        _flash_kernel,
        grid=(B, NQ // QS),
        in_specs=[pl.BlockSpec((1, NSA_GROUPS, QS, LANES, R), lambda b, i: (b, 0, i, 0, 0)),
                  per_b((NSA_GROUPS, S, LANES)), per_b(vst.shape[1:]), per_b((S, LANES)), per_b(vwt.shape[1:]),
                  pl.BlockSpec((1, QS * Q, LANES), lambda b, i: (b, i, 0)),
                  pl.BlockSpec((1, QS, NSA_QW, Q), lambda b, i: (b, i, 0, 0))],
        out_specs=pl.BlockSpec((1, QS * Q, NSA_QW), lambda b, i: (b, i, 0)),
        out_shape=jax.ShapeDtypeStruct((B, S, NSA_QW), BF16),
        scratch_shapes=[pltpu.VMEM((2, QS, NSA_GROUPS, 1, R), F32),
                        pltpu.VMEM((2, QS, NSA_GROUPS, VAL_ROWS, R), F32),
                        pltpu.VMEM((QS, NSA_GROUPS, LANES, R), BF16), pltpu.VMEM((2, Q, Q), F32),
                        pltpu.VMEM((2, QS * NSA_HEADS, Q, Q), F32)],
        compiler_params=_params(("arbitrary", "arbitrary")),
        name="flash",
    )(qaug, kaug, vst, kw, vwt, gates, ocmp)


def _retention_kernel(rq_ref, rk_ref, rv_ref, dmat_ref, qdec_ref, kdec_ref, cdec_ref, nw_ref, o_ref, st_scr):
    C = dmat_ref.shape[1]

    @pl.when(pl.program_id(1) == 0)
    def _():
        st_scr[...] = jnp.zeros_like(st_scr)

    low = lax.broadcasted_iota(jnp.int32, (C, LANES), 1) < HALF
    same_head = ((lax.broadcasted_iota(jnp.int32, (LANES, LANES), 0) < HALF) ==
                 (lax.broadcasted_iota(jnp.int32, (LANES, LANES), 1) < HALF))
    head_mean = jnp.where(same_head, 1.0 / RET_DV, 0.0).astype(BF16)
    n_chunks = rq_ref.shape[1] // C
    pairs = [(cc, p) for cc in range(n_chunks) for p in range(RET_HEADS // 2)]
    rows = lambda cc: slice(cc * C, (cc + 1) * C)
    lanes = lambda p: slice(p * LANES, (p + 1) * LANES)
    state = [st_scr[p] for p in range(RET_HEADS // 2)]
    scores, cross = {}, {}
    for cc, p in pairs:
        q2, k2, v2 = rq_ref[0, rows(cc), lanes(p)], rk_ref[0, rows(cc), lanes(p)], rv_ref[0, rows(cc), lanes(p)]
        q2f = q2.astype(F32)
        qe = jnp.where(low, q2f, 0.0).astype(BF16)
        qo = jnp.where(low, 0.0, q2f).astype(BF16)
        scores[cc, p] = (_dot_nt(qe, k2), _dot_nt(qo, k2))
        cross[cc, p] = _dot(q2, state[p].astype(BF16))
        kd = (k2.astype(F32) * kdec_ref[:, lanes(p)]).astype(BF16)
        state[p] = state[p] * cdec_ref[p:p + 1, :] + jnp.where(same_head, _dot_tn(kd, v2), 0.0)
    for p in range(RET_HEADS // 2):
        st_scr[p] = state[p]
    outs = {}
    for cc, p in pairs:
        v2 = rv_ref[0, rows(cc), lanes(p)]
        ie = (scores[cc, p][0] * dmat_ref[2 * p]).astype(BF16)
        io = (scores[cc, p][1] * dmat_ref[2 * p + 1]).astype(BF16)
        intra = jnp.where(low, _dot(ie, v2), _dot(io, v2))
        outs[cc, p] = intra + cross[cc, p] * qdec_ref[:, lanes(p)]
    means = {key: _dot(o.astype(BF16), head_mean) for key, o in outs.items()}
    devs = {key: outs[key] - means[key] for key in outs}
    variances = {key: _dot((d * d).astype(BF16), head_mean) for key, d in devs.items()}
    for cc, p in pairs:
        o_ref[0, rows(cc), lanes(p)] = devs[cc, p] * lax.rsqrt(variances[cc, p] + EPS) * nw_ref[:, lanes(p)]


def _retention(rq, rk, rv, ret_norm_w):
    B, S, _ = rq.shape
    C = min(RET_CHUNK, S)
    H = RET_HEADS
    log_g = np.log(1.0 - 2.0 ** (-5.0 - np.arange(H, dtype=np.float64)))
    i = np.arange(C, dtype=np.float64)
    diff = i[:, None] - i[None, :]
    as_f32 = lambda a: jnp.asarray(a, F32)
    dmat = as_f32(np.where(diff >= 0, np.exp(log_g[:, None, None] * np.maximum(diff, 0.0)), 0.0))
    per_lane = lambda a: np.repeat(a, RET_DK, axis=-1)
    qdec = as_f32(per_lane(np.exp(log_g[None, :] * (i[:, None] + 1.0))))
    kdec = as_f32(per_lane(np.exp(log_g[None, :] * (C - 1.0 - i[:, None]))))
    cdec = as_f32(per_lane(np.exp(log_g * C)[None, :]).reshape(H // 2, LANES))
    span = C * min(RET_CHUNKS_PER_STEP, S // C)
    tok = lambda w: pl.BlockSpec((1, span, w), lambda b, i: (b, i, 0))
    return pl.pallas_call(
        _retention_kernel,
        grid=(B, S // span),
        in_specs=[tok(RET_QW), tok(RET_QW), tok(RET_VW),
                  _full_spec((H, C, C)), _full_spec((C, RET_QW)), _full_spec((C, RET_QW)),
                  _full_spec((H // 2, LANES)), _full_spec((1, RET_VW))],
        out_specs=tok(RET_VW),
        out_shape=jax.ShapeDtypeStruct((B, S, RET_VW), F32),
        scratch_shapes=[pltpu.VMEM((H // 2, LANES, LANES), F32)],
        compiler_params=_params(("arbitrary", "arbitrary")),
        name="retention",
    )(rq, rk, rv, dmat, qdec, kdec, cdec, ret_norm_w.reshape(1, RET_VW))


def _mix_kernel(h_ref, mod_ref, nw_ref, onsa_ref, oret_ref, wg_ref, wn_ref, wr_ref, wo_ref, o_ref):
    x = h_ref[0]
    sh = mod_ref[0, 3:4, :]
    sc = mod_ref[0, 4:5, :]
    gt = mod_ref[0, 5:6, :]
    u = _norm_mod(x, nw_ref[...], sc, sh).astype(BF16)
    gates = _dot_nt(u, wg_ref[...])
    y_nsa = _dot(onsa_ref[0], wn_ref[...])
    oret = (oret_ref[0] * _silu(gates[:, 0:RET_VW])).astype(BF16)
    y_ret = _dot(oret, wr_ref[...])
    D = x.shape[1]
    ga = jax.nn.sigmoid(gates[:, RET_VW:RET_VW + D])
    gb = jax.nn.sigmoid(gates[:, RET_VW + D:RET_VW + 2 * D])
    mixed = (ga * y_nsa + gb * y_ret).astype(BF16)
    o_ref[0] = x + gt * _dot(mixed, wo_ref[...])


def _mix(h, mod, norm_w, o_nsa, o_ret, w_in, w_nsa_up, w_ret_up, w_out, *, tm):
    B, S, D = h.shape
    o = np.cumsum([0, NSA_QW] + [NSA_KVW] * 6 + [3 * NSA_HEADS, RET_QW, RET_QW, RET_VW, RET_VW, D, D])
    wt = w_in.T
    wg = wt[o[11]:o[14]].astype(BF16)
    wn, wr, wo = w_nsa_up.astype(BF16), w_ret_up.astype(BF16), w_out.astype(BF16)
    tok = lambda w: pl.BlockSpec((1, tm, w), lambda b, i: (b, i, 0))
    return pl.pallas_call(
        _mix_kernel,
        grid=(B, S // tm),
        in_specs=[tok(D), pl.BlockSpec((1, 9, D), lambda b, i: (b, 0, 0)), _full_spec((1, D)),
                  tok(NSA_QW), tok(RET_VW),
                  _resident_spec(wg.shape),
                  _resident_spec(wn.shape), _resident_spec(wr.shape), _resident_spec(wo.shape)],
        out_specs=tok(D),
        out_shape=jax.ShapeDtypeStruct((B, S, D), F32),
        compiler_params=_params(("arbitrary", "arbitrary")),
        name="mix",
    )(h, mod, norm_w.reshape(1, D), o_nsa, o_ret, wg, wn, wr, wo)


def kernel(x, c, ada_w, ada_b, norm1_w, ffn1_w_in, ffn1_w_out, norm2_w, w_in, cmp_k_pe, cmp_k_w1, cmp_k_w2,
           cmp_v_pe, cmp_v_w1, cmp_v_w2, ret_norm_w, w_nsa_up, w_ret_up, w_out, norm3_w, ffn2_w_in, ffn2_w_out,
           final_norm_w):
    B, S, D = x.shape
    depth = ada_w.shape[0]
    assert depth >= 1
    tm = min(TOKEN_TILE, S)
    tm_ffn = min(FFN_TOKEN_TILE, S)
    h = x
    for l in range(depth):
        last = l == depth - 1
        mod = _modulation(c, ada_w[l], ada_b[l]).reshape(B, 9, D)
        h = _ffn(h, mod, norm1_w[l], final_norm_w, ffn1_w_in[l], ffn1_w_out[l], mod_base=0, final_norm=False, tm=tm_ffn)
        q, cv, kaug, vst, kw, vwt, gates, rq, rk, rv = _projection(h, mod, norm2_w[l], w_in[l], tm=tm)
        kvc, kvct = _compress(cv, (cmp_k_pe[l], cmp_v_pe[l]), (cmp_k_w1[l], cmp_v_w1[l]), (cmp_k_w2[l], cmp_v_w2[l]))
        qaug, ocmp = _select(q, kvc, kvct, gates)
        o_nsa = _flash(qaug, kaug, vst, kw, vwt, gates, ocmp)
        o_ret = _retention(rq, rk, rv, ret_norm_w[l])
        h = _mix(h, mod, norm2_w[l], o_nsa, o_ret, w_in[l], w_nsa_up[l], w_ret_up[l], w_out[l], tm=tm)
        h = _ffn(h, mod, norm3_w[l], final_norm_w, ffn2_w_in[l], ffn2_w_out[l], mod_base=6, final_norm=last, tm=tm_ffn)
    return h
```

```python
import functools

import numpy as np
import jax
import jax.numpy as jnp
from jax import lax
from jax.experimental import pallas as pl
from jax.experimental.pallas import tpu as pltpu

F32 = jnp.float32
BF16 = jnp.bfloat16

NSA_HEADS = 8
NSA_GROUPS = 2
NSA_HPG = NSA_HEADS // NSA_GROUPS
NSA_DH = 64
CMP_STRIDE = 16
CMP_LEN = 2 * CMP_STRIDE
CMP_HIDDEN = 128
SEL_BLOCK = 64
SEL_TOPK = 16
WINDOW = 512
FORCE_BONUS = 1e4
RET_HEADS = 8
RET_DK = 64
RET_DV = 64
ROPE_BASE = 10000.0
EPS = 1e-6
NEG = -1e30
N_MOD = 9

NSA_QW = NSA_HEADS * NSA_DH
NSA_KVW = NSA_GROUPS * NSA_DH
RET_QW = RET_HEADS * RET_DK
RET_VW = RET_HEADS * RET_DV

LANES = 128
HALF = LANES // 2
BF16_ROWS = 16
VMEM_LIMIT = 56 * 1024 * 1024
LOG2E = 1.4426950408889634
MOD_COL_BLOCKS = 8
FFN_CHUNK = 256
TOKEN_TILE = 512
FFN_TOKEN_TILE = 1024
RET_CHUNK = 256
RET_CHUNKS_PER_STEP = 2
ATT_Q = 256
VAL_ROWS = NSA_DH + BF16_ROWS
CMP_PARTS = 4
RANK_SPAN = 16
FLASH_SUBTILES = 2
LOOP_TILES = 2
RUN_AHEAD = 6
LOOP_AHEAD = 2


def _dot(a, b):
    return jnp.dot(a, b, preferred_element_type=F32)


def _dot_nt(a, b):
    return lax.dot_general(a, b, (((1,), (1,)), ((), ())), preferred_element_type=F32)


def _dot_tn(a, b):
    return lax.dot_general(a, b, (((0,), (0,)), ((), ())), preferred_element_type=F32)


def _silu(a):
    return a * jax.nn.sigmoid(a)


def _norm_mod(x, nw, sc, sh):
    ms = jnp.mean(x * x, axis=-1, keepdims=True)
    gain = nw * (1.0 + sc)
    return (x * lax.rsqrt(ms + EPS)) * gain + sh


def _full_spec(shape):
    zeros = (0,) * len(shape)
    return pl.BlockSpec(shape, lambda *_: zeros)


def _resident_spec(shape):
    zeros = (0,) * len(shape)
    return pl.BlockSpec(shape, lambda *_: zeros, pipeline_mode=pl.Buffered(1))


def _params(sem):
    return pltpu.CompilerParams(dimension_semantics=sem, vmem_limit_bytes=VMEM_LIMIT)


def _mod_kernel(c_ref, w_ref, b_ref, o_ref):
    c = c_ref[...]
    o_ref[...] = _dot(_silu(c).astype(BF16), w_ref[...].astype(BF16)) + b_ref[...]


def _modulation(c, ada_w, ada_b):
    B, D = c.shape
    N = ada_w.shape[1]
    tn = N // MOD_COL_BLOCKS
    return pl.pallas_call(
        _mod_kernel,
        grid=(N // tn,),
        in_specs=[pl.BlockSpec((B, D), lambda j: (0, 0)),
                  pl.BlockSpec((D, tn), lambda j: (0, j)),
                  pl.BlockSpec((1, tn), lambda j: (0, j))],
        out_specs=pl.BlockSpec((B, tn), lambda j: (0, j)),
        out_shape=jax.ShapeDtypeStruct((B, N), F32),
        compiler_params=_params(("arbitrary",)),
        name="modulation",
    )(c, ada_w, ada_b.reshape(1, N))


def _ffn_kernel(x_ref, mod_ref, nw_ref, fw_ref, wi_ref, wo_ref, o_ref, u_scr, h_scr,
                *, mod_base, final_norm):
    x = x_ref[0]
    sh = mod_ref[0, mod_base:mod_base + 1, :]
    sc = mod_ref[0, mod_base + 1:mod_base + 2, :]
    gt = mod_ref[0, mod_base + 2:mod_base + 3, :]
    u_scr[...] = _norm_mod(x, nw_ref[...], sc, sh).astype(BF16)
    dff = wo_ref.shape[0]
    for c0 in range(0, dff, FFN_CHUNK):
        u = u_scr[...]
        a = _dot(u, wi_ref[:, c0:c0 + FFN_CHUNK])
        b = _dot(u, wi_ref[:, dff + c0:dff + c0 + FFN_CHUNK])
        h_scr[:, c0:c0 + FFN_CHUNK] = (_silu(a) * b).astype(BF16)
    y = x + 0.5 * gt * _dot(h_scr[...], wo_ref[...])
    if final_norm:
        ms = jnp.mean(y * y, axis=-1, keepdims=True)
        y = y * lax.rsqrt(ms + EPS) * fw_ref[...]
    o_ref[0] = y


def _ffn(x, mod, norm_w, final_w, w_in, w_out, *, mod_base, final_norm, tm):
    B, S, D = x.shape
    assert w_out.shape[0] % FFN_CHUNK == 0
    wi = w_in.astype(BF16)
    wo = w_out.astype(BF16)
    kern = functools.partial(_ffn_kernel, mod_base=mod_base, final_norm=final_norm)
    return pl.pallas_call(
        kern,
        grid=(B, S // tm),
        in_specs=[pl.BlockSpec((1, tm, D), lambda b, i: (b, i, 0)),
                  pl.BlockSpec((1, N_MOD, D), lambda b, i: (b, 0, 0)),
                  _full_spec((1, D)), _full_spec((1, D)),
                  _resident_spec(wi.shape), _resident_spec(wo.shape)],
        out_specs=pl.BlockSpec((1, tm, D), lambda b, i: (b, i, 0)),
        out_shape=jax.ShapeDtypeStruct((B, S, D), F32),
        scratch_shapes=[pltpu.VMEM((tm, D), BF16), pltpu.VMEM((tm, w_out.shape[0]), BF16)],
        compiler_params=_params(("arbitrary", "arbitrary")),
        name="ffn_final" if final_norm else "ffn",
    )(x, mod, norm_w.reshape(1, D), final_w.reshape(1, D), wi, wo)


def _proj_kernel(h_ref, mod_ref, nw_ref, wq_ref, wkv_ref, wg_ref, wr_ref, cos_ref, sin_ref,
                 q_ref, cv_ref, kaug_ref, vst_ref, kw_ref, vwt_ref, g_ref, rq_ref, rk_ref, rv_ref,
                 u_scr, cv_scr):
    tm = h_ref.shape[1]
    x = h_ref[0]
    sh = mod_ref[0, 3:4, :]
    sc = mod_ref[0, 4:5, :]
    u_scr[...] = _norm_mod(x, nw_ref[...], sc, sh).astype(BF16)
    u = u_scr[...]

    q_ref[0] = _dot_nt(u, wq_ref[...]).astype(BF16)

    kv = _dot_nt(u, wkv_ref[...])
    for t in range(2):
        cv_scr[t] = kv[:, t * LANES:(t + 1) * LANES]
        for tok in range(CMP_STRIDE):
            rows = cv_scr[t, pl.ds(tok, tm // CMP_STRIDE, stride=CMP_STRIDE), :]
            cv_ref[t, 0, :, tok * LANES:(tok + 1) * LANES] = rows.astype(BF16)
    ks = kv[:, 2 * LANES:3 * LANES]
    lane = lax.broadcasted_iota(jnp.int32, (tm, LANES), 1)
    row = lax.broadcasted_iota(jnp.int32, (tm, LANES), 0)
    low = lane < HALF
    blk = (pl.program_id(1) * tm + row) // SEL_BLOCK
    onehot = jnp.where(lane - HALF == blk, 1.0, 0.0)
    kaug_ref[0, 0] = jnp.where(low, ks, onehot).astype(BF16)
    kaug_ref[0, 1] = jnp.where(low, pltpu.roll(ks, HALF, 1), onehot).astype(BF16)
    kw_ref[0] = kv[:, 4 * LANES:5 * LANES].astype(BF16)
    kt = vst_ref.shape[4]
    ones = jnp.ones((VAL_ROWS - NSA_DH, tm), F32)
    for ref, col in ((vst_ref, 3), (vwt_ref, 5)):
        vt = kv[:, col * LANES:(col + 1) * LANES].T
        for g in range(NSA_GROUPS):
            aug = jnp.concatenate([vt[g * HALF:(g + 1) * HALF], ones], axis=0).astype(BF16)
            for c in range(tm // kt):
                ref[0, g, c] = aug[:, c * kt:(c + 1) * kt]

    g_ref[0] = jax.nn.sigmoid(_dot_nt(u, wg_ref[...]))

    r = _dot_nt(u, wr_ref[...])
    cos = cos_ref[...]
    sin = sin_ref[...]
    first = (lane & (RET_DK // 2)) == 0
    npair = RET_QW // LANES
    for j in range(2 * npair):
        xb = r[:, j * LANES:(j + 1) * LANES]
        partner = jnp.where(first, pltpu.roll(xb, LANES - RET_DK // 2, 1), pltpu.roll(xb, RET_DK // 2, 1))
        y = (xb * cos + partner * sin).astype(BF16)
        if j < npair:
            rq_ref[0, :, j * LANES:(j + 1) * LANES] = y
        else:
            rk_ref[0, :, (j - npair) * LANES:(j - npair + 1) * LANES] = y
    rv_ref[0] = r[:, 2 * RET_QW:2 * RET_QW + RET_VW].astype(BF16)


def _projection(h, mod, norm_w, w_in, *, tm):
    B, S, D = h.shape
    o = np.cumsum([0, NSA_QW] + [NSA_KVW] * 6 + [3 * NSA_HEADS, RET_QW, RET_QW, RET_VW])
    scale_q = NSA_DH ** -0.5 * LOG2E
    scale_k = RET_DK ** -0.5
    wt = w_in.T
    wq = (wt[o[0]:o[1]] * scale_q).astype(BF16)
    wkv = wt[o[1]:o[7]].astype(BF16)
    wg = jnp.pad(wt[o[7]:o[8]], ((0, LANES - 3 * NSA_HEADS), (0, 0))).astype(BF16)
    wr = jnp.concatenate([wt[o[8]:o[9]], wt[o[9]:o[10]] * scale_k, wt[o[10]:o[11]]], axis=0).astype(BF16)
    half = RET_DK // 2
    lane = np.arange(LANES)
    inv = ROPE_BASE ** (-np.arange(half, dtype=np.float64) / half)
    ang = np.arange(S, dtype=np.float64)[:, None] * inv[lane % half][None, :]
    cos_t = jnp.asarray(np.cos(ang), F32)
    sin_t = jnp.asarray(np.where((lane % RET_DK) < half, -np.sin(ang), np.sin(ang)), F32)
    tok = lambda w: pl.BlockSpec((1, tm, w), lambda b, i: (b, i, 0))
    kt = min(ATT_Q, S)
    vt_spec = pl.BlockSpec((1, NSA_GROUPS, tm // kt, VAL_ROWS, kt), lambda b, i: (b, 0, i, 0, 0))
    vt_shape = jax.ShapeDtypeStruct((B, NSA_GROUPS, S // kt, VAL_ROWS, kt), BF16)
    outs = pl.pallas_call(
        _proj_kernel,
        grid=(B, S // tm),
        in_specs=[tok(D), pl.BlockSpec((1, N_MOD, D), lambda b, i: (b, 0, 0)), _full_spec((1, D)),
                  _full_spec(wq.shape), _full_spec(wkv.shape), _full_spec(wg.shape), _full_spec(wr.shape),
                  pl.BlockSpec((tm, LANES), lambda b, i: (i, 0)), pl.BlockSpec((tm, LANES), lambda b, i: (i, 0))],
        out_specs=[tok(NSA_QW),
                   pl.BlockSpec((2, 1, tm // CMP_STRIDE, CMP_STRIDE * LANES), lambda b, i: (0, b, i, 0)),
                   pl.BlockSpec((1, NSA_GROUPS, tm, LANES), lambda b, i: (b, 0, i, 0)),
                   vt_spec, tok(LANES), vt_spec, tok(LANES),
                   tok(RET_QW), tok(RET_QW), tok(RET_VW)],
        out_shape=[jax.ShapeDtypeStruct((B, S, NSA_QW), BF16),
                   jax.ShapeDtypeStruct((2, B, S // CMP_STRIDE, CMP_STRIDE * LANES), BF16),
                   jax.ShapeDtypeStruct((B, NSA_GROUPS, S, LANES), BF16),
                   vt_shape,
                   jax.ShapeDtypeStruct((B, S, LANES), BF16),
                   vt_shape,
                   jax.ShapeDtypeStruct((B, S, LANES), F32),
                   jax.ShapeDtypeStruct((B, S, RET_QW), BF16),
                   jax.ShapeDtypeStruct((B, S, RET_QW), BF16),
                   jax.ShapeDtypeStruct((B, S, RET_VW), BF16)],
        scratch_shapes=[pltpu.VMEM((tm, D), BF16), pltpu.VMEM((2, tm, LANES), F32)],
        compiler_params=_params(("arbitrary", "arbitrary")),
        name="projection",
    )(h, mod, norm_w.reshape(1, D), wq, wkv, wg, wr, cos_t, sin_t)
    return outs


def _compress_kernel(ch_ref, w1c_ref, pe_ref, w1_ref, w2_ref, o_ref, ot_ref):
    ch = ch_ref[0, 0]
    nc = ch.shape[0]
    ab = _dot(ch, w1c_ref[0])
    pt = _dot(pe_ref[0], w1_ref[0])[0:1, :]
    pt2 = jnp.concatenate([pt, pt], axis=1)
    nh = NSA_GROUPS * CMP_HIDDEN
    hid = ab[:, :nh] + pltpu.roll(ab[:, nh:], nc - 1, 0) + pt2
    out = _dot(_silu(hid).astype(BF16), w2_ref[0])
    row = lax.broadcasted_iota(jnp.int32, out.shape, 0)
    out = jnp.where(row < nc - 1, out, 0.0)
    o_ref[0, 0] = out.astype(BF16)
    ot_ref[0, 0, 0:LANES, :] = out.T.astype(BF16)
    ot_ref[0, 0, LANES:, :] = jnp.ones((ot_ref.shape[2] - LANES, nc), BF16)


def _compress(cv, pe, w1, w2):
    _, B, nc, width = cv.shape
    chunks = cv
    eye = jnp.eye(NSA_GROUPS, dtype=F32)
    nh = NSA_GROUPS * CMP_HIDDEN

    def expand(w):
        w = w.reshape(CMP_STRIDE, NSA_DH, CMP_HIDDEN)
        return jnp.einsum("ldj,gh->lgdhj", w, eye).reshape(width, nh)

    half = CMP_STRIDE * NSA_DH
    w1c = jnp.stack([jnp.concatenate([expand(w[:half]), expand(w[half:])], axis=1) for w in w1]).astype(BF16)
    w2bd = jnp.stack([jnp.einsum("jd,gh->gjhd", w, eye).reshape(nh, NSA_KVW) for w in w2]).astype(BF16)
    pe8 = jnp.stack([jnp.broadcast_to(p.reshape(1, CMP_LEN * NSA_DH), (8, CMP_LEN * NSA_DH)) for p in pe]).astype(BF16)
    w1s = jnp.stack(w1).astype(BF16)
    sel = lambda shape: pl.BlockSpec((1,) + shape, lambda t, b: (t,) + (0,) * len(shape))
    return pl.pallas_call(
        _compress_kernel,
        grid=(2, B),
        in_specs=[pl.BlockSpec((1, 1, nc, width), lambda t, b: (t, b, 0, 0)),
                  sel(w1c.shape[1:]), sel(pe8.shape[1:]), sel(w1s.shape[1:]), sel(w2bd.shape[1:])],
        out_specs=[pl.BlockSpec((1, 1, nc, LANES), lambda t, b: (t, b, 0, 0)),
                   pl.BlockSpec((1, 1, LANES + BF16_ROWS, nc), lambda t, b: (t, b, 0, 0))],
        out_shape=[jax.ShapeDtypeStruct((2, B, nc, LANES), BF16),
                   jax.ShapeDtypeStruct((2, B, LANES + BF16_ROWS, nc), BF16)],
        compiler_params=_params(("arbitrary", "arbitrary")),
        name="compress",
    )(chunks, w1c, pe8, w1s, w2bd)


def _select_kernel(q_ref, kc_ref, vct_ref, g_ref, ovl_ref, qaug_ref, ocmp_ref, imp_scr, rank_scr, cap_scr,
                   *, n_sel, n_tiles):
    Q = q_ref.shape[1]
    NC = kc_ref.shape[2]
    NB = ovl_ref.shape[0]
    qi = pl.program_id(1)
    q0 = qi * Q
    gt = g_ref[0].T
    any_allowed = q0 + lax.broadcasted_iota(jnp.int32, (1, Q), 1) >= CMP_LEN - 1
    j_blk = lax.broadcasted_iota(jnp.int32, (NB, Q), 0)
    t_blk = q0 + lax.broadcasted_iota(jnp.int32, (NB, Q), 1)
    cur = t_blk // SEL_BLOCK
    forced = (j_blk == 0) | (j_blk == cur) | (j_blk == cur - 1)
    valid = j_blk * SEL_BLOCK <= t_blk
    j_sub = lax.broadcasted_iota(jnp.int32, (8, Q), 0)
    per_tile = Q // SEL_BLOCK

    def compressed_branch(rows):
        kcf = kc_ref[0, 0, 0:rows, :].astype(F32)
        kcr = pltpu.roll(kcf, HALF, 1)
        lowk = lax.broadcasted_iota(jnp.int32, (rows, LANES), 1) < HALF
        k_even = [jnp.where(lowk, kcf, 0.0).astype(BF16), jnp.where(lowk, kcr, 0.0).astype(BF16)]
        k_odd = [jnp.where(lowk, 0.0, kcr).astype(BF16), jnp.where(lowk, 0.0, kcf).astype(BF16)]
        vct = vct_ref[0, 0, :, 0:rows]
        ovl = ovl_ref[:, 0:rows]
        n_row = lax.broadcasted_iota(jnp.int32, (rows, Q), 0)
        t_col = q0 + lax.broadcasted_iota(jnp.int32, (rows, Q), 1)
        cap_scr[0:rows, :] = jnp.where(n_row * CMP_STRIDE + (CMP_LEN - 1) <= t_col, -NEG, NEG)

        scores = []
        for pb in range(NSA_HEADS // 2):
            g, pp = divmod(pb, NSA_HPG // 2)
            qt = q_ref[0, :, pb * LANES:(pb + 1) * LANES].astype(F32).T.astype(BF16)
            for e, kk in enumerate((k_even[g], k_odd[g])):
                scores.append(_dot(kk, qt))
                col = (2 * pp + e) * Q
                qaug_ref[0, g, 0, 0:HALF, col:col + Q] = qt[e * HALF:(e + 1) * HALF]

        for g in range(NSA_GROUPS):
            imp = jnp.zeros((NB, Q), F32)
            for hh in range(NSA_HPG):
                h = g * NSA_HPG + hh
                s = jnp.minimum(scores[h], cap_scr[0:rows, :])
                mx = jnp.max(s, axis=0, keepdims=True)
                ex = jnp.exp2(s - mx).astype(BF16)
                oa = _dot(vct, ex)
                inv = jnp.where(any_allowed, 1.0 / oa[LANES:LANES + 1], 0.0)
                ocmp_ref[0, 0, h * NSA_DH:(h + 1) * NSA_DH, :] = (gt[3 * h:3 * h + 1] * inv) * oa[g * HALF:(g + 1) * HALF]
                imp = imp + _dot(ovl, ex) * inv

            imp = jnp.where(forced, imp + FORCE_BONUS, imp)
            imp_scr[g] = jnp.where(valid, imp, -FORCE_BONUS)

    parts = CMP_PARTS if n_tiles % CMP_PARTS == 0 and NC % (16 * CMP_PARTS) == 0 else 1
    for c in range(parts):
        @pl.when((qi >= c * (n_tiles // parts)) & (qi < (c + 1) * (n_tiles // parts)))
        def _():
            compressed_branch((c + 1) * (NC // parts))
    rank_scr[...] = jnp.zeros_like(rank_scr)

    for c0 in range(0, NB, RANK_SPAN):
        @pl.when(c0 // per_tile <= qi)
        def _():
            slabs = range(NB // 8)
            for g in range(NSA_GROUPS):
                vals = [imp_scr[g, 8 * sb:8 * sb + 8, :] for sb in slabs]
                cnts = [rank_scr[g, 8 * sb:8 * sb + 8, :] for sb in slabs]
                for jp in range(c0, c0 + RANK_SPAN):
                    r = jnp.broadcast_to(imp_scr[g, jp:jp + 1, :], (8, Q))
                    for sb in slabs:
                        v = vals[sb]
                        if 8 * sb > jp:
                            ahead = r >= v
                        elif 8 * sb + 7 <= jp:
                            ahead = r > v
                        else:
                            ahead = (r > v) | ((r == v) & (j_sub + 8 * sb > jp))
                        cnts[sb] = jnp.where(ahead, cnts[sb] + 1, cnts[sb])
                for sb in slabs:
                    rank_scr[g, 8 * sb:8 * sb + 8, :] = cnts[sb]

    for g in range(NSA_GROUPS):
        sel = (rank_scr[g] < n_sel) & valid
        bias = jnp.where(sel, 0.0, NEG).astype(BF16)
        for hh in range(NSA_HPG):
            qaug_ref[0, g, 0, HALF:HALF + NB, hh * Q:(hh + 1) * Q] = bias


def _select(q, kc, vct, gates):
    B, S, _ = q.shape
    Q = min(ATT_Q, S)
    NC = kc.shape[2]
    nc, ns = NC - 1, S // SEL_BLOCK
    NB = LANES - HALF
    assert ns <= NB and Q % SEL_BLOCK == 0
    c_start = np.arange(nc) * CMP_STRIDE
    s_start = np.arange(ns) * SEL_BLOCK
    overlap = ((c_start[:, None] < s_start[None, :] + SEL_BLOCK) &
               (c_start[:, None] + CMP_LEN > s_start[None, :])).astype(np.float32)
    ovl = np.zeros((NB, NC), np.float32)
    ovl[:ns, :nc] = overlap.T
    kern = functools.partial(_select_kernel, n_sel=min(SEL_TOPK, ns), n_tiles=S // Q)
    return pl.pallas_call(
        kern,
        grid=(B, S // Q),
        in_specs=[pl.BlockSpec((1, Q, NSA_QW), lambda b, i: (b, i, 0)),
                  pl.BlockSpec((1, 1, NC, LANES), lambda b, i: (0, b, 0, 0)),
                  pl.BlockSpec((1, 1) + vct.shape[2:], lambda b, i: (1, b, 0, 0)),
                  pl.BlockSpec((1, Q, LANES), lambda b, i: (b, i, 0)),
                  _full_spec((NB, NC))],
        out_specs=[pl.BlockSpec((1, NSA_GROUPS, 1, LANES, NSA_HPG * Q), lambda b, i: (b, 0, i, 0, 0)),
                   pl.BlockSpec((1, 1, NSA_QW, Q), lambda b, i: (b, i, 0, 0))],
        out_shape=[jax.ShapeDtypeStruct((B, NSA_GROUPS, S // Q, LANES, NSA_HPG * Q), BF16),
                   jax.ShapeDtypeStruct((B, S // Q, NSA_QW, Q), F32)],
        scratch_shapes=[pltpu.VMEM((NSA_GROUPS, NB, Q), F32), pltpu.VMEM((NSA_GROUPS, NB, Q), jnp.int32),
                        pltpu.VMEM((NC, Q), F32)],
        compiler_params=_params(("arbitrary", "arbitrary")),
        name="select",
    )(q, kc, vct, gates, jnp.asarray(ovl, BF16))


def _flash_kernel(qaug_ref, kaug_ref, vst_ref, kw_ref, vwt_ref, g_ref, ocmp_ref, o_ref,
                  m_scr, acc_scr, qw_scr, cap_scr, s_scr):
    QS = qaug_ref.shape[2]
    R = qaug_ref.shape[4]
    Q = R // NSA_HPG
    KT = kaug_ref.shape[2] // vst_ref.shape[2]
    WT = WINDOW // KT
    SLC, WIN = 0, 1
    CAUSAL, EDGE = 0, 1
    t0 = pl.program_id(1) * QS
    k_row = lax.broadcasted_iota(jnp.int32, (KT, Q), 0)
    i_col = lax.broadcasted_iota(jnp.int32, (KT, Q), 1)
    big = -NEG
    cap_scr[CAUSAL] = jnp.where(k_row <= i_col, big, NEG)
    cap_scr[EDGE] = jnp.where(k_row > i_col, big, NEG)
    m_scr[SLC] = jnp.full(m_scr.shape[1:], NEG, F32)
    acc_scr[SLC] = jnp.zeros(acc_scr.shape[1:], F32)
    zeros = jnp.zeros((HALF, R), BF16)
    for qs in range(QS):
        qw_scr[qs, 0] = jnp.concatenate([qaug_ref[0, 0, qs, 0:HALF, :], zeros], axis=0)
        qw_scr[qs, 1] = jnp.concatenate([zeros, qaug_ref[0, 1, qs, 0:HALF, :]], axis=0)

    def scores(ch):
        br, qs, g, hh, kj, _, _ = ch
        off = pl.multiple_of(kj * KT, KT)
        if br == SLC:
            return _dot(kaug_ref[0, g, pl.ds(off, KT), :], qaug_ref[0, g, qs, :, hh * Q:(hh + 1) * Q])
        return _dot(kw_ref[0, pl.ds(off, KT), :], qw_scr[qs, g, :, hh * Q:(hh + 1) * Q])

    def absorb(ch, s):
        br, qs, g, hh, kj, cap, fresh = ch
        cols = slice(hh * Q, (hh + 1) * Q)
        if cap is not None:
            s = jnp.minimum(s, cap_scr[cap])
        vt = vst_ref[0, g, kj] if br == SLC else vwt_ref[0, g, kj]
        if fresh:
            m_new = jnp.max(s, axis=0, keepdims=True)
            acc_scr[br, qs, g, :, cols] = _dot(vt, jnp.exp2(s - m_new).astype(BF16))
        else:
            m_prev = m_scr[br, qs, g, :, cols]
            m_new = jnp.maximum(m_prev, jnp.max(s, axis=0, keepdims=True))
            p = jnp.exp2(s - m_new)
            alpha = jnp.exp2(m_prev - m_new)
            acc_scr[br, qs, g, :, cols] = alpha * acc_scr[br, qs, g, :, cols] + _dot(vt, p.astype(BF16))
        m_scr[br, qs, g, :, cols] = m_new

    NSH = QS * NSA_HEADS
    RING = 2 * NSH

    def run(chains, ready=()):
        order = list(ready) + list(chains)
        assert len(ready) + RUN_AHEAD < RING
        for t in range(-RUN_AHEAD, len(order)):
            n = t + RUN_AHEAD
            if len(ready) <= n < len(order):
                s_scr[(n % RING) // NSH, n % NSH] = scores(order[n])
            if t >= 0:
                absorb(order[t], s_scr[(t % RING) // NSH, t % NSH])

    def tile(br, qs, kj, cap, fresh=False):
        return [(br, qs, g, hh, kj, cap, fresh) for g in range(NSA_GROUPS) for hh in range(NSA_HPG)]

    def shared(kj, first_cap=None):
        return [ch for qs in range(QS) for ch in tile(SLC, qs, kj, first_cap if qs == 0 else None)]

    def issue(kj, slot, n):
        s_scr[slot, n] = scores(shared(kj)[n])

    for n in range(NSH):
        issue(0, 0, n)

    def pipelined(first, count):
        produce = [(first + k + 1, (k + 1) % 2, n) for k in range(count) for n in range(NSH)]
        absorbs = [(first + k, k % 2, n) for k in range(count) for n in range(NSH)]
        for t in range(-LOOP_AHEAD, len(absorbs)):
            if t + LOOP_AHEAD < len(produce):
                issue(*produce[t + LOOP_AHEAD])
            if t >= 0:
                kj, slot, n = absorbs[t]
                absorb(shared(kj)[n], s_scr[slot, n])

    def slc_body(j, carry):
        pipelined(LOOP_TILES * j, LOOP_TILES)
        return carry

    lax.fori_loop(0, t0 // LOOP_TILES, slc_body, 0)
    for left in range(2, LOOP_TILES, 2):
        @pl.when(t0 % LOOP_TILES == left)
        def _():
            pipelined(t0 - left, left)
    ready = shared(t0, CAUSAL)

    always = []
    later = {}
    for qs in range(QS):
        for kk in range(1, qs):
            always += tile(SLC, qs, t0 + kk, None)
        if qs > 0:
            always += tile(SLC, qs, t0 + qs, CAUSAL)
        always += tile(WIN, qs, t0 + qs, CAUSAL, fresh=True)
        for back in range(1, WT + 1):
            chains = tile(WIN, qs, t0 + qs - back, EDGE if back == WT else None)
            if back <= qs:
                always += chains
            else:
                later.setdefault(-(-(back - qs) // QS) * QS, []).extend(chains)
    if len(later) == 1:
        (need, chains), = later.items()

        @pl.when(t0 >= need)
        def _():
            run(always + chains, ready)

        @pl.when(t0 < need)
        def _():
            run(always, ready)
    else:
        run(always, ready)
        for need, chains in sorted(later.items()):
            @pl.when(t0 >= need)
            def _():
                run(chains)

    for qs in range(QS):
        gt = g_ref[0, qs * Q:(qs + 1) * Q, :].T
        for pb in range(NSA_HEADS // 2):
            halves = []
            for e in range(2):
                h = 2 * pb + e
                g, hh = divmod(h, NSA_HPG)
                cols = slice(hh * Q, (hh + 1) * Q)
                o = ocmp_ref[0, qs, h * NSA_DH:(h + 1) * NSA_DH, :]
                for br in (SLC, WIN):
                    scale = gt[3 * h + 1 + br:3 * h + 2 + br] / acc_scr[br, qs, g, NSA_DH:NSA_DH + 1, cols]
                    o = o + scale * acc_scr[br, qs, g, 0:NSA_DH, cols]
                halves.append(o)
            o_ref[0, qs * Q:(qs + 1) * Q, pb * LANES:(pb + 1) * LANES] = jnp.concatenate(halves, axis=0).T.astype(BF16)


def _flash(qaug, kaug, vst, kw, vwt, gates, ocmp):
    B, _, NQ, _, R = qaug.shape
    Q = R // NSA_HPG
    S = NQ * Q
    QS = min(FLASH_SUBTILES, NQ)
    assert WINDOW % Q == 0 and vst.shape[4] == Q and NQ % QS == 0 and QS % 2 == 0
    per_b = lambda shape: pl.BlockSpec((1,) + shape, lambda b, i: (b,) + (0,) * len(shape))
    return pl.pallas_call(
        _flash_kernel,
        grid=(B, NQ // QS),
        in_specs=[pl.BlockSpec((1, NSA_GROUPS, QS, LANES, R), lambda b, i: (b, 0, i, 0, 0)),
                  per_b((NSA_GROUPS, S, LANES)), per_b(vst.shape[1:]), per_b((S, LANES)), per_b(vwt.shape[1:]),
                  pl.BlockSpec((1, QS * Q, LANES), lambda b, i: (b, i, 0)),
                  pl.BlockSpec((1, QS, NSA_QW, Q), lambda b, i: (b, i, 0, 0))],
        out_specs=pl.BlockSpec((1, QS * Q, NSA_QW), lambda b, i: (b, i, 0)),
        out_shape=jax.ShapeDtypeStruct((B, S, NSA_QW), BF16),
        scratch_shapes=[pltpu.VMEM((2, QS, NSA_GROUPS, 1, R), F32),
                        pltpu.VMEM((2, QS, NSA_GROUPS, VAL_ROWS, R), F32),
                        pltpu.VMEM((QS, NSA_GROUPS, LANES, R), BF16), pltpu.VMEM((2, Q, Q), F32),
                        pltpu.VMEM((2, QS * NSA_HEADS, Q, Q), F32)],
        compiler_params=_params(("arbitrary", "arbitrary")),
        name="flash",
    )(qaug, kaug, vst, kw, vwt, gates, ocmp)


def _retention_kernel(rq_ref, rk_ref, rv_ref, dmat_ref, qdec_ref, kdec_ref, cdec_ref, nw_ref, o_ref, st_scr):
    C = dmat_ref.shape[1]

    @pl.when(pl.program_id(1) == 0)
    def _():
        st_scr[...] = jnp.zeros_like(st_scr)

    low = lax.broadcasted_iota(jnp.int32, (C, LANES), 1) < HALF
    same_head = ((lax.broadcasted_iota(jnp.int32, (LANES, LANES), 0) < HALF) ==
                 (lax.broadcasted_iota(jnp.int32, (LANES, LANES), 1) < HALF))
    head_mean = jnp.where(same_head, 1.0 / RET_DV, 0.0).astype(BF16)
    n_chunks = rq_ref.shape[1] // C
    pairs = [(cc, p) for cc in range(n_chunks) for p in range(RET_HEADS // 2)]
    rows = lambda cc: slice(cc * C, (cc + 1) * C)
    lanes = lambda p: slice(p * LANES, (p + 1) * LANES)
    state = [st_scr[p] for p in range(RET_HEADS // 2)]
    scores, cross = {}, {}
    for cc, p in pairs:
        q2, k2, v2 = rq_ref[0, rows(cc), lanes(p)], rk_ref[0, rows(cc), lanes(p)], rv_ref[0, rows(cc), lanes(p)]
        q2f = q2.astype(F32)
        qe = jnp.where(low, q2f, 0.0).astype(BF16)
        qo = jnp.where(low, 0.0, q2f).astype(BF16)
        scores[cc, p] = (_dot_nt(qe, k2), _dot_nt(qo, k2))
        cross[cc, p] = _dot(q2, state[p].astype(BF16))
        kd = (k2.astype(F32) * kdec_ref[:, lanes(p)]).astype(BF16)
        state[p] = state[p] * cdec_ref[p:p + 1, :] + jnp.where(same_head, _dot_tn(kd, v2), 0.0)
    for p in range(RET_HEADS // 2):
        st_scr[p] = state[p]
    outs = {}
    for cc, p in pairs:
        v2 = rv_ref[0, rows(cc), lanes(p)]
        ie = (scores[cc, p][0] * dmat_ref[2 * p]).astype(BF16)
        io = (scores[cc, p][1] * dmat_ref[2 * p + 1]).astype(BF16)
        intra = jnp.where(low, _dot(ie, v2), _dot(io, v2))
        outs[cc, p] = intra + cross[cc, p] * qdec_ref[:, lanes(p)]
    means = {key: _dot(o.astype(BF16), head_mean) for key, o in outs.items()}
    devs = {key: outs[key] - means[key] for key in outs}
    variances = {key: _dot((d * d).astype(BF16), head_mean) for key, d in devs.items()}
    for cc, p in pairs:
        o_ref[0, rows(cc), lanes(p)] = devs[cc, p] * lax.rsqrt(variances[cc, p] + EPS) * nw_ref[:, lanes(p)]


def _retention(rq, rk, rv, ret_norm_w):
    B, S, _ = rq.shape
    C = min(RET_CHUNK, S)
    H = RET_HEADS
    log_g = np.log(1.0 - 2.0 ** (-5.0 - np.arange(H, dtype=np.float64)))
    i = np.arange(C, dtype=np.float64)
    diff = i[:, None] - i[None, :]
    as_f32 = lambda a: jnp.asarray(a, F32)
    dmat = as_f32(np.where(diff >= 0, np.exp(log_g[:, None, None] * np.maximum(diff, 0.0)), 0.0))
    per_lane = lambda a: np.repeat(a, RET_DK, axis=-1)
    qdec = as_f32(per_lane(np.exp(log_g[None, :] * (i[:, None] + 1.0))))
    kdec = as_f32(per_lane(np.exp(log_g[None, :] * (C - 1.0 - i[:, None]))))
    cdec = as_f32(per_lane(np.exp(log_g * C)[None, :]).reshape(H // 2, LANES))
    span = C * min(RET_CHUNKS_PER_STEP, S // C)
    tok = lambda w: pl.BlockSpec((1, span, w), lambda b, i: (b, i, 0))
    return pl.pallas_call(
        _retention_kernel,
        grid=(B, S // span),
        in_specs=[tok(RET_QW), tok(RET_QW), tok(RET_VW),
                  _full_spec((H, C, C)), _full_spec((C, RET_QW)), _full_spec((C, RET_QW)),
                  _full_spec((H // 2, LANES)), _full_spec((1, RET_VW))],
        out_specs=tok(RET_VW),
        out_shape=jax.ShapeDtypeStruct((B, S, RET_VW), F32),
        scratch_shapes=[pltpu.VMEM((H // 2, LANES, LANES), F32)],
        compiler_params=_params(("arbitrary", "arbitrary")),
        name="retention",
    )(rq, rk, rv, dmat, qdec, kdec, cdec, ret_norm_w.reshape(1, RET_VW))


def _mix_kernel(h_ref, mod_ref, nw_ref, onsa_ref, oret_ref, wg_ref, wn_ref, wr_ref, wo_ref, o_ref):
    x = h_ref[0]
    sh = mod_ref[0, 3:4, :]
    sc = mod_ref[0, 4:5, :]
    gt = mod_ref[0, 5:6, :]
    u = _norm_mod(x, nw_ref[...], sc, sh).astype(BF16)
    gates = _dot_nt(u, wg_ref[...])
    y_nsa = _dot(onsa_ref[0], wn_ref[...])
    oret = (oret_ref[0] * _silu(gates[:, 0:RET_VW])).astype(BF16)
    y_ret = _dot(oret, wr_ref[...])
    D = x.shape[1]
    ga = jax.nn.sigmoid(gates[:, RET_VW:RET_VW + D])
    gb = jax.nn.sigmoid(gates[:, RET_VW + D:RET_VW + 2 * D])
    mixed = (ga * y_nsa + gb * y_ret).astype(BF16)
    o_ref[0] = x + gt * _dot(mixed, wo_ref[...])


def _mix(h, mod, norm_w, o_nsa, o_ret, w_in, w_nsa_up, w_ret_up, w_out, *, tm):
    B, S, D = h.shape
    o = np.cumsum([0, NSA_QW] + [NSA_KVW] * 6 + [3 * NSA_HEADS, RET_QW, RET_QW, RET_VW, RET_VW, D, D])
    wt = w_in.T
    wg = wt[o[11]:o[14]].astype(BF16)
    wn, wr, wo = w_nsa_up.astype(BF16), w_ret_up.astype(BF16), w_out.astype(BF16)
    tok = lambda w: pl.BlockSpec((1, tm, w), lambda b, i: (b, i, 0))
    return pl.pallas_call(
        _mix_kernel,
        grid=(B, S // tm),
        in_specs=[tok(D), pl.BlockSpec((1, N_MOD, D), lambda b, i: (b, 0, 0)), _full_spec((1, D)),
                  tok(NSA_QW), tok(RET_VW),
                  _resident_spec(wg.shape),
                  _resident_spec(wn.shape), _resident_spec(wr.shape), _resident_spec(wo.shape)],
        out_specs=tok(D),
        out_shape=jax.ShapeDtypeStruct((B, S, D), F32),
        compiler_params=_params(("arbitrary", "arbitrary")),
        name="mix",
    )(h, mod, norm_w.reshape(1, D), o_nsa, o_ret, wg, wn, wr, wo)


def kernel(x, c, ada_w, ada_b, norm1_w, ffn1_w_in, ffn1_w_out, norm2_w, w_in, cmp_k_pe, cmp_k_w1, cmp_k_w2,
           cmp_v_pe, cmp_v_w1, cmp_v_w2, ret_norm_w, w_nsa_up, w_ret_up, w_out, norm3_w, ffn2_w_in, ffn2_w_out,
           final_norm_w):
    B, S, D = x.shape
    depth = ada_w.shape[0]
    assert depth >= 1
    tm = min(TOKEN_TILE, S)
    tm_ffn = min(FFN_TOKEN_TILE, S)
    h = x
    for l in range(depth):
        last = l == depth - 1
        mod = _modulation(c, ada_w[l], ada_b[l]).reshape(B, N_MOD, D)
        h = _ffn(h, mod, norm1_w[l], final_norm_w, ffn1_w_in[l], ffn1_w_out[l], mod_base=0, final_norm=False, tm=tm_ffn)
        q, cv, kaug, vst, kw, vwt, gates, rq, rk, rv = _projection(h, mod, norm2_w[l], w_in[l], tm=tm)
        kvc, kvct = _compress(cv, (cmp_k_pe[l], cmp_v_pe[l]), (cmp_k_w1[l], cmp_v_w1[l]), (cmp_k_w2[l], cmp_v_w2[l]))
        qaug, ocmp = _select(q, kvc, kvct, gates)
        o_nsa = _flash(qaug, kaug, vst, kw, vwt, gates, ocmp)
        o_ret = _retention(rq, rk, rv, ret_norm_w[l])
        h = _mix(h, mod, norm2_w[l], o_nsa, o_ret, w_in[l], w_nsa_up[l], w_ret_up[l], w_out[l], tm=tm)
        h = _ffn(h, mod, norm3_w[l], final_norm_w, ffn2_w_in[l], ffn2_w_out[l], mod_base=6, final_norm=last, tm=tm_ffn)
    return h
```

```python
import functools

import numpy as np
import jax
import jax.numpy as jnp
from jax import lax
from jax.experimental import pallas as pl
from jax.experimental.pallas import tpu as pltpu

F32 = jnp.float32
BF16 = jnp.bfloat16

NSA_HEADS = 8
NSA_GROUPS = 2
NSA_HPG = NSA_HEADS // NSA_GROUPS
NSA_DH = 64
CMP_STRIDE = 16
CMP_LEN = 2 * CMP_STRIDE
CMP_HIDDEN = 128
SEL_BLOCK = 64
SEL_TOPK = 16
WINDOW = 512
FORCE_BONUS = 1e4
RET_HEADS = 8
RET_DK = 64
RET_DV = 64
ROPE_BASE = 10000.0
EPS = 1e-6
NEG = -1e30
N_MOD = 9

NSA_QW = NSA_HEADS * NSA_DH
NSA_KVW = NSA_GROUPS * NSA_DH
RET_QW = RET_HEADS * RET_DK
RET_VW = RET_HEADS * RET_DV

LANES = 128
HALF = LANES // 2
BF16_ROWS = 16
VMEM_LIMIT = 56 * 1024 * 1024
LOG2E = 1.4426950408889634
MOD_COL_BLOCKS = 8
FFN_CHUNK = 256
TOKEN_TILE = 512
WIDE_TOKEN_TILE = 1024
RET_CHUNK = 256
RET_CHUNKS_PER_STEP = 4
ATT_Q = 256
VAL_ROWS = NSA_DH + BF16_ROWS
CMP_PARTS = 4
RANK_SPAN = 16
FLASH_SUBTILES = 2
LOOP_TILES = 2
RUN_AHEAD = 6
LOOP_AHEAD = 2


def _dot(a, b):
    return jnp.dot(a, b, preferred_element_type=F32)


def _dot_nt(a, b):
    return lax.dot_general(a, b, (((1,), (1,)), ((), ())), preferred_element_type=F32)


def _dot_tn(a, b):
    return lax.dot_general(a, b, (((0,), (0,)), ((), ())), preferred_element_type=F32)


def _silu(a):
    return a * jax.nn.sigmoid(a)


def _norm_mod(x, nw, sc, sh):
    ms = jnp.mean(x * x, axis=-1, keepdims=True)
    gain = nw * (1.0 + sc)
    return (x * lax.rsqrt(ms + EPS)) * gain + sh


def _full_spec(shape):
    zeros = (0,) * len(shape)
    return pl.BlockSpec(shape, lambda *_: zeros)


def _resident_spec(shape):
    zeros = (0,) * len(shape)
    return pl.BlockSpec(shape, lambda *_: zeros, pipeline_mode=pl.Buffered(1))


def _params(sem):
    return pltpu.CompilerParams(dimension_semantics=sem, vmem_limit_bytes=VMEM_LIMIT)


def _mod_kernel(c_ref, w_ref, b_ref, o_ref):
    c = c_ref[...]
    o_ref[...] = _dot(_silu(c).astype(BF16), w_ref[...].astype(BF16)) + b_ref[...]


def _modulation(c, ada_w, ada_b):
    B, D = c.shape
    N = ada_w.shape[1]
    tn = N // MOD_COL_BLOCKS
    return pl.pallas_call(
        _mod_kernel,
        grid=(N // tn,),
        in_specs=[pl.BlockSpec((B, D), lambda j: (0, 0)),
                  pl.BlockSpec((D, tn), lambda j: (0, j)),
                  pl.BlockSpec((1, tn), lambda j: (0, j))],
        out_specs=pl.BlockSpec((B, tn), lambda j: (0, j)),
        out_shape=jax.ShapeDtypeStruct((B, N), F32),
        compiler_params=_params(("arbitrary",)),
        name="modulation",
    )(c, ada_w, ada_b.reshape(1, N))


def _ffn_kernel(x_ref, mod_ref, nw_ref, fw_ref, wi_ref, wo_ref, o_ref, u_scr, h_scr,
                *, mod_base, final_norm):
    x = x_ref[0]
    sh = mod_ref[0, mod_base:mod_base + 1, :]
    sc = mod_ref[0, mod_base + 1:mod_base + 2, :]
    gt = mod_ref[0, mod_base + 2:mod_base + 3, :]
    u_scr[...] = _norm_mod(x, nw_ref[...], sc, sh).astype(BF16)
    dff = wo_ref.shape[0]
    for c0 in range(0, dff, FFN_CHUNK):
        u = u_scr[...]
        a = _dot(u, wi_ref[:, c0:c0 + FFN_CHUNK])
        b = _dot(u, wi_ref[:, dff + c0:dff + c0 + FFN_CHUNK])
        h_scr[:, c0:c0 + FFN_CHUNK] = (_silu(a) * b).astype(BF16)
    y = x + 0.5 * gt * _dot(h_scr[...], wo_ref[...])
    if final_norm:
        ms = jnp.mean(y * y, axis=-1, keepdims=True)
        y = y * lax.rsqrt(ms + EPS) * fw_ref[...]
    o_ref[0] = y


def _ffn(x, mod, norm_w, final_w, w_in, w_out, *, mod_base, final_norm, tm):
    B, S, D = x.shape
    assert w_out.shape[0] % FFN_CHUNK == 0
    wi = w_in.astype(BF16)
    wo = w_out.astype(BF16)
    kern = functools.partial(_ffn_kernel, mod_base=mod_base, final_norm=final_norm)
    return pl.pallas_call(
        kern,
        grid=(B, S // tm),
        in_specs=[pl.BlockSpec((1, tm, D), lambda b, i: (b, i, 0)),
                  pl.BlockSpec((1, N_MOD, D), lambda b, i: (b, 0, 0)),
                  _full_spec((1, D)), _full_spec((1, D)),
                  _resident_spec(wi.shape), _resident_spec(wo.shape)],
        out_specs=pl.BlockSpec((1, tm, D), lambda b, i: (b, i, 0)),
        out_shape=jax.ShapeDtypeStruct((B, S, D), F32),
        scratch_shapes=[pltpu.VMEM((tm, D), BF16), pltpu.VMEM((tm, w_out.shape[0]), BF16)],
        compiler_params=_params(("arbitrary", "arbitrary")),
        name="ffn_final" if final_norm else "ffn",
    )(x, mod, norm_w.reshape(1, D), final_w.reshape(1, D), wi, wo)


def _proj_kernel(h_ref, mod_ref, nw_ref, wq_ref, wkv_ref, wg_ref, wr_ref, cos_ref, sin_ref,
                 q_ref, cv_ref, kaug_ref, vst_ref, kw_ref, vwt_ref, g_ref, rq_ref, rk_ref, rv_ref,
                 u_scr, cv_scr):
    tm = h_ref.shape[1]
    x = h_ref[0]
    sh = mod_ref[0, 3:4, :]
    sc = mod_ref[0, 4:5, :]
    u_scr[...] = _norm_mod(x, nw_ref[...], sc, sh).astype(BF16)
    u = u_scr[...]

    q_ref[0] = _dot_nt(u, wq_ref[...]).astype(BF16)

    kv = _dot_nt(u, wkv_ref[...])
    for t in range(2):
        cv_scr[t] = kv[:, t * LANES:(t + 1) * LANES]
        for tok in range(CMP_STRIDE):
            rows = cv_scr[t, pl.ds(tok, tm // CMP_STRIDE, stride=CMP_STRIDE), :]
            cv_ref[t, 0, :, tok * LANES:(tok + 1) * LANES] = rows.astype(BF16)
    ks = kv[:, 2 * LANES:3 * LANES]
    lane = lax.broadcasted_iota(jnp.int32, (tm, LANES), 1)
    row = lax.broadcasted_iota(jnp.int32, (tm, LANES), 0)
    low = lane < HALF
    blk = (pl.program_id(1) * tm + row) // SEL_BLOCK
    onehot = jnp.where(lane - HALF == blk, 1.0, 0.0)
    kaug_ref[0, 0] = jnp.where(low, ks, onehot).astype(BF16)
    kaug_ref[0, 1] = jnp.where(low, pltpu.roll(ks, HALF, 1), onehot).astype(BF16)
    kw_ref[0] = kv[:, 4 * LANES:5 * LANES].astype(BF16)
    kt = vst_ref.shape[4]
    ones = jnp.ones((VAL_ROWS - NSA_DH, tm), F32)
    for ref, col in ((vst_ref, 3), (vwt_ref, 5)):
        vt = kv[:, col * LANES:(col + 1) * LANES].T
        for g in range(NSA_GROUPS):
            aug = jnp.concatenate([vt[g * HALF:(g + 1) * HALF], ones], axis=0).astype(BF16)
            for c in range(tm // kt):
                ref[0, g, c] = aug[:, c * kt:(c + 1) * kt]

    g_ref[0] = jax.nn.sigmoid(_dot_nt(u, wg_ref[...]))

    r = _dot_nt(u, wr_ref[...])
    cos = cos_ref[...]
    sin = sin_ref[...]
    first = (lane & (RET_DK // 2)) == 0
    npair = RET_QW // LANES
    for j in range(2 * npair):
        xb = r[:, j * LANES:(j + 1) * LANES]
        partner = jnp.where(first, pltpu.roll(xb, LANES - RET_DK // 2, 1), pltpu.roll(xb, RET_DK // 2, 1))
        y = (xb * cos + partner * sin).astype(BF16)
        if j < npair:
            rq_ref[0, :, j * LANES:(j + 1) * LANES] = y
        else:
            rk_ref[0, :, (j - npair) * LANES:(j - npair + 1) * LANES] = y
    rv_ref[0] = r[:, 2 * RET_QW:2 * RET_QW + RET_VW].astype(BF16)


def _projection(h, mod, norm_w, w_in, *, tm):
    B, S, D = h.shape
    o = np.cumsum([0, NSA_QW] + [NSA_KVW] * 6 + [3 * NSA_HEADS, RET_QW, RET_QW, RET_VW])
    scale_q = NSA_DH ** -0.5 * LOG2E
    scale_k = RET_DK ** -0.5
    wt = w_in.T
    wq = (wt[o[0]:o[1]] * scale_q).astype(BF16)
    wkv = wt[o[1]:o[7]].astype(BF16)
    wg = jnp.pad(wt[o[7]:o[8]], ((0, LANES - 3 * NSA_HEADS), (0, 0))).astype(BF16)
    wr = jnp.concatenate([wt[o[8]:o[9]], wt[o[9]:o[10]] * scale_k, wt[o[10]:o[11]]], axis=0).astype(BF16)
    half = RET_DK // 2
    lane = np.arange(LANES)
    inv = ROPE_BASE ** (-np.arange(half, dtype=np.float64) / half)
    ang = np.arange(S, dtype=np.float64)[:, None] * inv[lane % half][None, :]
    cos_t = jnp.asarray(np.cos(ang), F32)
    sin_t = jnp.asarray(np.where((lane % RET_DK) < half, -np.sin(ang), np.sin(ang)), F32)
    tok = lambda w: pl.BlockSpec((1, tm, w), lambda b, i: (b, i, 0))
    kt = min(ATT_Q, S)
    vt_spec = pl.BlockSpec((1, NSA_GROUPS, tm // kt, VAL_ROWS, kt), lambda b, i: (b, 0, i, 0, 0))
    vt_shape = jax.ShapeDtypeStruct((B, NSA_GROUPS, S // kt, VAL_ROWS, kt), BF16)
    outs = pl.pallas_call(
        _proj_kernel,
        grid=(B, S // tm),
        in_specs=[tok(D), pl.BlockSpec((1, N_MOD, D), lambda b, i: (b, 0, 0)), _full_spec((1, D)),
                  _full_spec(wq.shape), _full_spec(wkv.shape), _full_spec(wg.shape), _full_spec(wr.shape),
                  pl.BlockSpec((tm, LANES), lambda b, i: (i, 0)), pl.BlockSpec((tm, LANES), lambda b, i: (i, 0))],
        out_specs=[tok(NSA_QW),
                   pl.BlockSpec((2, 1, tm // CMP_STRIDE, CMP_STRIDE * LANES), lambda b, i: (0, b, i, 0)),
                   pl.BlockSpec((1, NSA_GROUPS, tm, LANES), lambda b, i: (b, 0, i, 0)),
                   vt_spec, tok(LANES), vt_spec, tok(LANES),
                   tok(RET_QW), tok(RET_QW), tok(RET_VW)],
        out_shape=[jax.ShapeDtypeStruct((B, S, NSA_QW), BF16),
                   jax.ShapeDtypeStruct((2, B, S // CMP_STRIDE, CMP_STRIDE * LANES), BF16),
                   jax.ShapeDtypeStruct((B, NSA_GROUPS, S, LANES), BF16),
                   vt_shape,
                   jax.ShapeDtypeStruct((B, S, LANES), BF16),
                   vt_shape,
                   jax.ShapeDtypeStruct((B, S, LANES), F32),
                   jax.ShapeDtypeStruct((B, S, RET_QW), BF16),
                   jax.ShapeDtypeStruct((B, S, RET_QW), BF16),
                   jax.ShapeDtypeStruct((B, S, RET_VW), BF16)],
        scratch_shapes=[pltpu.VMEM((tm, D), BF16), pltpu.VMEM((2, tm, LANES), F32)],
        compiler_params=_params(("arbitrary", "arbitrary")),
        name="projection",
    )(h, mod, norm_w.reshape(1, D), wq, wkv, wg, wr, cos_t, sin_t)
    return outs


def _compress_kernel(ch_ref, w1c_ref, pe_ref, w1_ref, w2_ref, o_ref, ot_ref):
    ch = ch_ref[0, 0]
    nc = ch.shape[0]
    ab = _dot(ch, w1c_ref[0])
    pt = _dot(pe_ref[0], w1_ref[0])[0:1, :]
    pt2 = jnp.concatenate([pt, pt], axis=1)
    nh = NSA_GROUPS * CMP_HIDDEN
    hid = ab[:, :nh] + pltpu.roll(ab[:, nh:], nc - 1, 0) + pt2
    out = _dot(_silu(hid).astype(BF16), w2_ref[0])
    row = lax.broadcasted_iota(jnp.int32, out.shape, 0)
    out = jnp.where(row < nc - 1, out, 0.0)
    o_ref[0, 0] = out.astype(BF16)
    ot_ref[0, 0, 0:LANES, :] = out.T.astype(BF16)
    ot_ref[0, 0, LANES:, :] = jnp.ones((ot_ref.shape[2] - LANES, nc), BF16)


def _compress(cv, pe, w1, w2):
    _, B, nc, width = cv.shape
    chunks = cv
    eye = jnp.eye(NSA_GROUPS, dtype=F32)
    nh = NSA_GROUPS * CMP_HIDDEN

    def expand(w):
        w = w.reshape(CMP_STRIDE, NSA_DH, CMP_HIDDEN)
        return jnp.einsum("ldj,gh->lgdhj", w, eye).reshape(width, nh)

    half = CMP_STRIDE * NSA_DH
    w1c = jnp.stack([jnp.concatenate([expand(w[:half]), expand(w[half:])], axis=1) for w in w1]).astype(BF16)
    w2bd = jnp.stack([jnp.einsum("jd,gh->gjhd", w, eye).reshape(nh, NSA_KVW) for w in w2]).astype(BF16)
    pe8 = jnp.stack([jnp.broadcast_to(p.reshape(1, CMP_LEN * NSA_DH), (8, CMP_LEN * NSA_DH)) for p in pe]).astype(BF16)
    w1s = jnp.stack(w1).astype(BF16)
    sel = lambda shape: pl.BlockSpec((1,) + shape, lambda t, b: (t,) + (0,) * len(shape))
    return pl.pallas_call(
        _compress_kernel,
        grid=(2, B),
        in_specs=[pl.BlockSpec((1, 1, nc, width), lambda t, b: (t, b, 0, 0)),
                  sel(w1c.shape[1:]), sel(pe8.shape[1:]), sel(w1s.shape[1:]), sel(w2bd.shape[1:])],
        out_specs=[pl.BlockSpec((1, 1, nc, LANES), lambda t, b: (t, b, 0, 0)),
                   pl.BlockSpec((1, 1, LANES + BF16_ROWS, nc), lambda t, b: (t, b, 0, 0))],
        out_shape=[jax.ShapeDtypeStruct((2, B, nc, LANES), BF16),
                   jax.ShapeDtypeStruct((2, B, LANES + BF16_ROWS, nc), BF16)],
        compiler_params=_params(("arbitrary", "arbitrary")),
        name="compress",
    )(chunks, w1c, pe8, w1s, w2bd)


def _select_kernel(q_ref, kc_ref, vct_ref, g_ref, ovl_ref, qaug_ref, ocmp_ref, imp_scr, rank_scr, cap_scr,
                   *, n_sel, n_tiles):
    Q = q_ref.shape[1]
    NC = kc_ref.shape[2]
    NB = ovl_ref.shape[0]
    qi = pl.program_id(1)
    q0 = qi * Q
    gt = g_ref[0].T
    any_allowed = q0 + lax.broadcasted_iota(jnp.int32, (1, Q), 1) >= CMP_LEN - 1
    j_blk = lax.broadcasted_iota(jnp.int32, (NB, Q), 0)
    t_blk = q0 + lax.broadcasted_iota(jnp.int32, (NB, Q), 1)
    cur = t_blk // SEL_BLOCK
    forced = (j_blk == 0) | (j_blk == cur) | (j_blk == cur - 1)
    valid = j_blk * SEL_BLOCK <= t_blk
    j_sub = lax.broadcasted_iota(jnp.int32, (8, Q), 0)
    per_tile = Q // SEL_BLOCK

    def compressed_branch(rows):
        kcf = kc_ref[0, 0, 0:rows, :].astype(F32)
        kcr = pltpu.roll(kcf, HALF, 1)
        lowk = lax.broadcasted_iota(jnp.int32, (rows, LANES), 1) < HALF
        k_even = [jnp.where(lowk, kcf, 0.0).astype(BF16), jnp.where(lowk, kcr, 0.0).astype(BF16)]
        k_odd = [jnp.where(lowk, 0.0, kcr).astype(BF16), jnp.where(lowk, 0.0, kcf).astype(BF16)]
        vct = vct_ref[0, 0, :, 0:rows]
        ovl = ovl_ref[:, 0:rows]
        n_row = lax.broadcasted_iota(jnp.int32, (rows, Q), 0)
        t_col = q0 + lax.broadcasted_iota(jnp.int32, (rows, Q), 1)
        cap_scr[0:rows, :] = jnp.where(n_row * CMP_STRIDE + (CMP_LEN - 1) <= t_col, -NEG, NEG)

        scores = []
        for pb in range(NSA_HEADS // 2):
            g, pp = divmod(pb, NSA_HPG // 2)
            qt = q_ref[0, :, pb * LANES:(pb + 1) * LANES].astype(F32).T.astype(BF16)
            for e, kk in enumerate((k_even[g], k_odd[g])):
                scores.append(_dot(kk, qt))
                col = (2 * pp + e) * Q
                qaug_ref[0, g, 0, 0:HALF, col:col + Q] = qt[e * HALF:(e + 1) * HALF]

        for g in range(NSA_GROUPS):
            imp = jnp.zeros((NB, Q), F32)
            for hh in range(NSA_HPG):
                h = g * NSA_HPG + hh
                s = jnp.minimum(scores[h], cap_scr[0:rows, :])
                mx = jnp.max(s, axis=0, keepdims=True)
                ex = jnp.exp2(s - mx).astype(BF16)
                oa = _dot(vct, ex)
                inv = jnp.where(any_allowed, 1.0 / oa[LANES:LANES + 1], 0.0)
                ocmp_ref[0, 0, h * NSA_DH:(h + 1) * NSA_DH, :] = (gt[3 * h:3 * h + 1] * inv) * oa[g * HALF:(g + 1) * HALF]
                imp = imp + _dot(ovl, ex) * inv

            imp = jnp.where(forced, imp + FORCE_BONUS, imp)
            imp_scr[g] = jnp.where(valid, imp, -FORCE_BONUS)

    parts = CMP_PARTS if n_tiles % CMP_PARTS == 0 and NC % (16 * CMP_PARTS) == 0 else 1
    for c in range(parts):
        @pl.when((qi >= c * (n_tiles // parts)) & (qi < (c + 1) * (n_tiles // parts)))
        def _():
            compressed_branch((c + 1) * (NC // parts))
    rank_scr[...] = jnp.zeros_like(rank_scr)

    for c0 in range(0, NB, RANK_SPAN):
        @pl.when(c0 // per_tile <= qi)
        def _():
            slabs = range(NB // 8)
            for g in range(NSA_GROUPS):
                vals = [imp_scr[g, 8 * sb:8 * sb + 8, :] for sb in slabs]
                cnts = [rank_scr[g, 8 * sb:8 * sb + 8, :] for sb in slabs]
                for jp in range(c0, c0 + RANK_SPAN):
                    r = jnp.broadcast_to(imp_scr[g, jp:jp + 1, :], (8, Q))
                    for sb in slabs:
                        v = vals[sb]
                        if 8 * sb > jp:
                            ahead = r >= v
                        elif 8 * sb + 7 <= jp:
                            ahead = r > v
                        else:
                            ahead = (r > v) | ((r == v) & (j_sub + 8 * sb > jp))
                        cnts[sb] = jnp.where(ahead, cnts[sb] + 1, cnts[sb])
                for sb in slabs:
                    rank_scr[g, 8 * sb:8 * sb + 8, :] = cnts[sb]

    for g in range(NSA_GROUPS):
        sel = (rank_scr[g] < n_sel) & valid
        bias = jnp.where(sel, 0.0, NEG).astype(BF16)
        for hh in range(NSA_HPG):
            qaug_ref[0, g, 0, HALF:HALF + NB, hh * Q:(hh + 1) * Q] = bias


def _select(q, kc, vct, gates):
    B, S, _ = q.shape
    Q = min(ATT_Q, S)
    NC = kc.shape[2]
    nc, ns = NC - 1, S // SEL_BLOCK
    NB = LANES - HALF
    assert ns <= NB and Q % SEL_BLOCK == 0
    c_start = np.arange(nc) * CMP_STRIDE
    s_start = np.arange(ns) * SEL_BLOCK
    overlap = ((c_start[:, None] < s_start[None, :] + SEL_BLOCK) &
               (c_start[:, None] + CMP_LEN > s_start[None, :])).astype(np.float32)
    ovl = np.zeros((NB, NC), np.float32)
    ovl[:ns, :nc] = overlap.T
    kern = functools.partial(_select_kernel, n_sel=min(SEL_TOPK, ns), n_tiles=S // Q)
    return pl.pallas_call(
        kern,
        grid=(B, S // Q),
        in_specs=[pl.BlockSpec((1, Q, NSA_QW), lambda b, i: (b, i, 0)),
                  pl.BlockSpec((1, 1, NC, LANES), lambda b, i: (0, b, 0, 0)),
                  pl.BlockSpec((1, 1) + vct.shape[2:], lambda b, i: (1, b, 0, 0)),
                  pl.BlockSpec((1, Q, LANES), lambda b, i: (b, i, 0)),
                  _full_spec((NB, NC))],
        out_specs=[pl.BlockSpec((1, NSA_GROUPS, 1, LANES, NSA_HPG * Q), lambda b, i: (b, 0, i, 0, 0)),
                   pl.BlockSpec((1, 1, NSA_QW, Q), lambda b, i: (b, i, 0, 0))],
        out_shape=[jax.ShapeDtypeStruct((B, NSA_GROUPS, S // Q, LANES, NSA_HPG * Q), BF16),
                   jax.ShapeDtypeStruct((B, S // Q, NSA_QW, Q), F32)],
        scratch_shapes=[pltpu.VMEM((NSA_GROUPS, NB, Q), F32), pltpu.VMEM((NSA_GROUPS, NB, Q), jnp.int32),
                        pltpu.VMEM((NC, Q), F32)],
        compiler_params=_params(("arbitrary", "arbitrary")),
        name="select",
    )(q, kc, vct, gates, jnp.asarray(ovl, BF16))


def _flash_kernel(qaug_ref, kaug_ref, vst_ref, kw_ref, vwt_ref, g_ref, ocmp_ref, o_ref,
                  m_scr, acc_scr, qw_scr, cap_scr, s_scr):
    QS = qaug_ref.shape[2]
    R = qaug_ref.shape[4]
    Q = R // NSA_HPG
    KT = kaug_ref.shape[2] // vst_ref.shape[2]
    WT = WINDOW // KT
    SLC, WIN = 0, 1
    CAUSAL, EDGE = 0, 1
    t0 = pl.program_id(1) * QS
    k_row = lax.broadcasted_iota(jnp.int32, (KT, Q), 0)
    i_col = lax.broadcasted_iota(jnp.int32, (KT, Q), 1)
    big = -NEG
    cap_scr[CAUSAL] = jnp.where(k_row <= i_col, big, NEG)
    cap_scr[EDGE] = jnp.where(k_row > i_col, big, NEG)
    m_scr[SLC] = jnp.full(m_scr.shape[1:], NEG, F32)
    acc_scr[SLC] = jnp.zeros(acc_scr.shape[1:], F32)
    zeros = jnp.zeros((HALF, R), BF16)
    for qs in range(QS):
        qw_scr[qs, 0] = jnp.concatenate([qaug_ref[0, 0, qs, 0:HALF, :], zeros], axis=0)
        qw_scr[qs, 1] = jnp.concatenate([zeros, qaug_ref[0, 1, qs, 0:HALF, :]], axis=0)

    def scores(ch):
        br, qs, g, hh, kj, _, _ = ch
        off = pl.multiple_of(kj * KT, KT)
        if br == SLC:
            return _dot(kaug_ref[0, g, pl.ds(off, KT), :], qaug_ref[0, g, qs, :, hh * Q:(hh + 1) * Q])
        return _dot(kw_ref[0, pl.ds(off, KT), :], qw_scr[qs, g, :, hh * Q:(hh + 1) * Q])

    def absorb(ch, s):
        br, qs, g, hh, kj, cap, fresh = ch
        cols = slice(hh * Q, (hh + 1) * Q)
        if cap is not None:
            s = jnp.minimum(s, cap_scr[cap])
        vt = vst_ref[0, g, kj] if br == SLC else vwt_ref[0, g, kj]
        if fresh:
            m_new = jnp.max(s, axis=0, keepdims=True)
            acc_scr[br, qs, g, :, cols] = _dot(vt, jnp.exp2(s - m_new).astype(BF16))
        else:
            m_prev = m_scr[br, qs, g, :, cols]
            m_new = jnp.maximum(m_prev, jnp.max(s, axis=0, keepdims=True))
            p = jnp.exp2(s - m_new)
            alpha = jnp.exp2(m_prev - m_new)
            acc_scr[br, qs, g, :, cols] = alpha * acc_scr[br, qs, g, :, cols] + _dot(vt, p.astype(BF16))
        m_scr[br, qs, g, :, cols] = m_new

    NSH = QS * NSA_HEADS
    RING = 2 * NSH

    def run(chains, ready=()):
        order = list(ready) + list(chains)
        assert len(ready) + RUN_AHEAD < RING
        for t in range(-RUN_AHEAD, len(order)):
            n = t + RUN_AHEAD
            if len(ready) <= n < len(order):
                s_scr[(n % RING) // NSH, n % NSH] = scores(order[n])
            if t >= 0:
                absorb(order[t], s_scr[(t % RING) // NSH, t % NSH])

    def tile(br, qs, kj, cap, fresh=False):
        return [(br, qs, g, hh, kj, cap, fresh) for g in range(NSA_GROUPS) for hh in range(NSA_HPG)]

    def shared(kj, first_cap=None):
        return [ch for qs in range(QS) for ch in tile(SLC, qs, kj, first_cap if qs == 0 else None)]

    def issue(kj, slot, n):
        s_scr[slot, n] = scores(shared(kj)[n])

    for n in range(NSH):
        issue(0, 0, n)

    def pipelined(first, count):
        produce = [(first + k + 1, (k + 1) % 2, n) for k in range(count) for n in range(NSH)]
        absorbs = [(first + k, k % 2, n) for k in range(count) for n in range(NSH)]
        for t in range(-LOOP_AHEAD, len(absorbs)):
            if t + LOOP_AHEAD < len(produce):
                issue(*produce[t + LOOP_AHEAD])
            if t >= 0:
                kj, slot, n = absorbs[t]
                absorb(shared(kj)[n], s_scr[slot, n])

    def slc_body(j, carry):
        pipelined(LOOP_TILES * j, LOOP_TILES)
        return carry

    lax.fori_loop(0, t0 // LOOP_TILES, slc_body, 0)
    for left in range(2, LOOP_TILES, 2):
        @pl.when(t0 % LOOP_TILES == left)
        def _():
            pipelined(t0 - left, left)
    ready = shared(t0, CAUSAL)

    always = []
    later = {}
    for qs in range(QS):
        for kk in range(1, qs):
            always += tile(SLC, qs, t0 + kk, None)
        if qs > 0:
            always += tile(SLC, qs, t0 + qs, CAUSAL)
        always += tile(WIN, qs, t0 + qs, CAUSAL, fresh=True)
        for back in range(1, WT + 1):
            chains = tile(WIN, qs, t0 + qs - back, EDGE if back == WT else None)
            if back <= qs:
                always += chains
            else:
                later.setdefault(-(-(back - qs) // QS) * QS, []).extend(chains)
    if len(later) == 1:
        (need, chains), = later.items()

        @pl.when(t0 >= need)
        def _():
            run(always + chains, ready)

        @pl.when(t0 < need)
        def _():
            run(always, ready)
    else:
        run(always, ready)
        for need, chains in sorted(later.items()):
            @pl.when(t0 >= need)
            def _():
                run(chains)

    for qs in range(QS):
        gt = g_ref[0, qs * Q:(qs + 1) * Q, :].T
        for pb in range(NSA_HEADS // 2):
            halves = []
            for e in range(2):
                h = 2 * pb + e
                g, hh = divmod(h, NSA_HPG)
                cols = slice(hh * Q, (hh + 1) * Q)
                o = ocmp_ref[0, qs, h * NSA_DH:(h + 1) * NSA_DH, :]
                for br in (SLC, WIN):
                    scale = gt[3 * h + 1 + br:3 * h + 2 + br] / acc_scr[br, qs, g, NSA_DH:NSA_DH + 1, cols]
                    o = o + scale * acc_scr[br, qs, g, 0:NSA_DH, cols]
                halves.append(o)
            o_ref[0, qs * Q:(qs + 1) * Q, pb * LANES:(pb + 1) * LANES] = jnp.concatenate(halves, axis=0).T.astype(BF16)


def _flash(qaug, kaug, vst, kw, vwt, gates, ocmp):
    B, _, NQ, _, R = qaug.shape
    Q = R // NSA_HPG
    S = NQ * Q
    QS = min(FLASH_SUBTILES, NQ)
    assert WINDOW % Q == 0 and vst.shape[4] == Q and NQ % QS == 0 and QS % 2 == 0
    per_b = lambda shape: pl.BlockSpec((1,) + shape, lambda b, i: (b,) + (0,) * len(shape))
    return pl.pallas_call(
        _flash_kernel,
        grid=(B, NQ // QS),
        in_specs=[pl.BlockSpec((1, NSA_GROUPS, QS, LANES, R), lambda b, i: (b, 0, i, 0, 0)),
                  per_b((NSA_GROUPS, S, LANES)), per_b(vst.shape[1:]), per_b((S, LANES)), per_b(vwt.shape[1:]),
                  pl.BlockSpec((1, QS * Q, LANES), lambda b, i: (b, i, 0)),
                  pl.BlockSpec((1, QS, NSA_QW, Q), lambda b, i: (b, i, 0, 0))],
        out_specs=pl.BlockSpec((1, QS * Q, NSA_QW), lambda b, i: (b, i, 0)),
        out_shape=jax.ShapeDtypeStruct((B, S, NSA_QW), BF16),
        scratch_shapes=[pltpu.VMEM((2, QS, NSA_GROUPS, 1, R), F32),
                        pltpu.VMEM((2, QS, NSA_GROUPS, VAL_ROWS, R), F32),
                        pltpu.VMEM((QS, NSA_GROUPS, LANES, R), BF16), pltpu.VMEM((2, Q, Q), F32),
                        pltpu.VMEM((2, QS * NSA_HEADS, Q, Q), F32)],
        compiler_params=_params(("arbitrary", "arbitrary")),
        name="flash",
    )(qaug, kaug, vst, kw, vwt, gates, ocmp)


def _retention_kernel(rq_ref, rk_ref, rv_ref, dmat_ref, qdec_ref, kdec_ref, cdec_ref, nw_ref, o_ref, st_scr):
    C = dmat_ref.shape[1]

    @pl.when(pl.program_id(1) == 0)
    def _():
        st_scr[...] = jnp.zeros_like(st_scr)

    low = lax.broadcasted_iota(jnp.int32, (C, LANES), 1) < HALF
    same_head = ((lax.broadcasted_iota(jnp.int32, (LANES, LANES), 0) < HALF) ==
                 (lax.broadcasted_iota(jnp.int32, (LANES, LANES), 1) < HALF))
    head_mean = jnp.where(same_head, 1.0 / RET_DV, 0.0).astype(BF16)
    n_chunks = rq_ref.shape[1] // C
    pairs = [(cc, p) for cc in range(n_chunks) for p in range(RET_HEADS // 2)]
    rows = lambda cc: slice(cc * C, (cc + 1) * C)
    lanes = lambda p: slice(p * LANES, (p + 1) * LANES)
    state = [st_scr[p] for p in range(RET_HEADS // 2)]
    scores, cross = {}, {}
    for cc, p in pairs:
        q2, k2, v2 = rq_ref[0, rows(cc), lanes(p)], rk_ref[0, rows(cc), lanes(p)], rv_ref[0, rows(cc), lanes(p)]
        q2f = q2.astype(F32)
        qe = jnp.where(low, q2f, 0.0).astype(BF16)
        qo = jnp.where(low, 0.0, q2f).astype(BF16)
        scores[cc, p] = (_dot_nt(qe, k2), _dot_nt(qo, k2))
        cross[cc, p] = _dot(q2, state[p].astype(BF16))
        kd = (k2.astype(F32) * kdec_ref[:, lanes(p)]).astype(BF16)
        state[p] = state[p] * cdec_ref[p:p + 1, :] + jnp.where(same_head, _dot_tn(kd, v2), 0.0)
    for p in range(RET_HEADS // 2):
        st_scr[p] = state[p]
    outs = {}
    for cc, p in pairs:
        v2 = rv_ref[0, rows(cc), lanes(p)]
        ie = (scores[cc, p][0] * dmat_ref[2 * p]).astype(BF16)
        io = (scores[cc, p][1] * dmat_ref[2 * p + 1]).astype(BF16)
        intra = jnp.where(low, _dot(ie, v2), _dot(io, v2))
        outs[cc, p] = intra + cross[cc, p] * qdec_ref[:, lanes(p)]
    means = {key: _dot(o.astype(BF16), head_mean) for key, o in outs.items()}
    devs = {key: outs[key] - means[key] for key in outs}
    variances = {key: _dot((d * d).astype(BF16), head_mean) for key, d in devs.items()}
    for cc, p in pairs:
        o_ref[0, rows(cc), lanes(p)] = devs[cc, p] * lax.rsqrt(variances[cc, p] + EPS) * nw_ref[:, lanes(p)]


def _retention(rq, rk, rv, ret_norm_w):
    B, S, _ = rq.shape
    C = min(RET_CHUNK, S)
    H = RET_HEADS
    log_g = np.log(1.0 - 2.0 ** (-5.0 - np.arange(H, dtype=np.float64)))
    i = np.arange(C, dtype=np.float64)
    diff = i[:, None] - i[None, :]
    as_f32 = lambda a: jnp.asarray(a, F32)
    dmat = as_f32(np.where(diff >= 0, np.exp(log_g[:, None, None] * np.maximum(diff, 0.0)), 0.0))
    per_lane = lambda a: np.repeat(a, RET_DK, axis=-1)
    qdec = as_f32(per_lane(np.exp(log_g[None, :] * (i[:, None] + 1.0))))
    kdec = as_f32(per_lane(np.exp(log_g[None, :] * (C - 1.0 - i[:, None]))))
    cdec = as_f32(per_lane(np.exp(log_g * C)[None, :]).reshape(H // 2, LANES))
    span = C * min(RET_CHUNKS_PER_STEP, S // C)
    tok = lambda w: pl.BlockSpec((1, span, w), lambda b, i: (b, i, 0))
    return pl.pallas_call(
        _retention_kernel,
        grid=(B, S // span),
        in_specs=[tok(RET_QW), tok(RET_QW), tok(RET_VW),
                  _full_spec((H, C, C)), _full_spec((C, RET_QW)), _full_spec((C, RET_QW)),
                  _full_spec((H // 2, LANES)), _full_spec((1, RET_VW))],
        out_specs=tok(RET_VW),
        out_shape=jax.ShapeDtypeStruct((B, S, RET_VW), F32),
        scratch_shapes=[pltpu.VMEM((H // 2, LANES, LANES), F32)],
        compiler_params=_params(("arbitrary", "arbitrary")),
        name="retention",
    )(rq, rk, rv, dmat, qdec, kdec, cdec, ret_norm_w.reshape(1, RET_VW))


def _mix_kernel(h_ref, mod_ref, nw_ref, onsa_ref, oret_ref, wg_ref, wn_ref, wr_ref, wo_ref, o_ref):
    x = h_ref[0]
    sh = mod_ref[0, 3:4, :]
    sc = mod_ref[0, 4:5, :]
    gt = mod_ref[0, 5:6, :]
    u = _norm_mod(x, nw_ref[...], sc, sh).astype(BF16)
    gates = _dot_nt(u, wg_ref[...])
    y_nsa = _dot(onsa_ref[0], wn_ref[...])
    oret = (oret_ref[0] * _silu(gates[:, 0:RET_VW])).astype(BF16)
    y_ret = _dot(oret, wr_ref[...])
    D = x.shape[1]
    ga = jax.nn.sigmoid(gates[:, RET_VW:RET_VW + D])
    gb = jax.nn.sigmoid(gates[:, RET_VW + D:RET_VW + 2 * D])
    mixed = (ga * y_nsa + gb * y_ret).astype(BF16)
    o_ref[0] = x + gt * _dot(mixed, wo_ref[...])


def _mix(h, mod, norm_w, o_nsa, o_ret, w_in, w_nsa_up, w_ret_up, w_out, *, tm):
    B, S, D = h.shape
    o = np.cumsum([0, NSA_QW] + [NSA_KVW] * 6 + [3 * NSA_HEADS, RET_QW, RET_QW, RET_VW, RET_VW, D, D])
    wt = w_in.T
    wg = wt[o[11]:o[14]].astype(BF16)
    wn, wr, wo = w_nsa_up.astype(BF16), w_ret_up.astype(BF16), w_out.astype(BF16)
    tok = lambda w: pl.BlockSpec((1, tm, w), lambda b, i: (b, i, 0))
    return pl.pallas_call(
        _mix_kernel,
        grid=(B, S // tm),
        in_specs=[tok(D), pl.BlockSpec((1, N_MOD, D), lambda b, i: (b, 0, 0)), _full_spec((1, D)),
                  tok(NSA_QW), tok(RET_VW),
                  _resident_spec(wg.shape),
                  _resident_spec(wn.shape), _resident_spec(wr.shape), _resident_spec(wo.shape)],
        out_specs=tok(D),
        out_shape=jax.ShapeDtypeStruct((B, S, D), F32),
        compiler_params=_params(("arbitrary", "arbitrary")),
        name="mix",
    )(h, mod, norm_w.reshape(1, D), o_nsa, o_ret, wg, wn, wr, wo)


def kernel(x, c, ada_w, ada_b, norm1_w, ffn1_w_in, ffn1_w_out, norm2_w, w_in, cmp_k_pe, cmp_k_w1, cmp_k_w2,
           cmp_v_pe, cmp_v_w1, cmp_v_w2, ret_norm_w, w_nsa_up, w_ret_up, w_out, norm3_w, ffn2_w_in, ffn2_w_out,
           final_norm_w):
    B, S, D = x.shape
    depth = ada_w.shape[0]
    assert depth >= 1
    tm = min(TOKEN_TILE, S)
    tm_wide = min(WIDE_TOKEN_TILE, S)
    h = x
    for l in range(depth):
        last = l == depth - 1
        mod = _modulation(c, ada_w[l], ada_b[l]).reshape(B, N_MOD, D)
        h = _ffn(h, mod, norm1_w[l], final_norm_w, ffn1_w_in[l], ffn1_w_out[l], mod_base=0, final_norm=False, tm=tm_wide)
        q, cv, kaug, vst, kw, vwt, gates, rq, rk, rv = _projection(h, mod, norm2_w[l], w_in[l], tm=tm)
        kvc, kvct = _compress(cv, (cmp_k_pe[l], cmp_v_pe[l]), (cmp_k_w1[l], cmp_v_w1[l]), (cmp_k_w2[l], cmp_v_w2[l]))
        qaug, ocmp = _select(q, kvc, kvct, gates)
        o_nsa = _flash(qaug, kaug, vst, kw, vwt, gates, ocmp)
        o_ret = _retention(rq, rk, rv, ret_norm_w[l])
        h = _mix(h, mod, norm2_w[l], o_nsa, o_ret, w_in[l], w_nsa_up[l], w_ret_up[l], w_out[l], tm=tm_wide)
        h = _ffn(h, mod, norm3_w[l], final_norm_w, ffn2_w_in[l], ffn2_w_out[l], mod_base=6, final_norm=last, tm=tm_wide)
    return h
```

```python
import functools

import numpy as np
import jax
import jax.numpy as jnp
from jax import lax
from jax.experimental import pallas as pl
from jax.experimental.pallas import tpu as pltpu

F32 = jnp.float32
BF16 = jnp.bfloat16

NSA_HEADS = 8
NSA_GROUPS = 2
NSA_HPG = NSA_HEADS // NSA_GROUPS
NSA_DH = 64
CMP_STRIDE = 16
CMP_LEN = 2 * CMP_STRIDE
CMP_HIDDEN = 128
SEL_BLOCK = 64
SEL_TOPK = 16
WINDOW = 512
FORCE_BONUS = 1e4
RET_HEADS = 8
RET_DK = 64
RET_DV = 64
ROPE_BASE = 10000.0
EPS = 1e-6
NEG = -1e30
N_MOD = 9

NSA_QW = NSA_HEADS * NSA_DH
NSA_KVW = NSA_GROUPS * NSA_DH
RET_QW = RET_HEADS * RET_DK
RET_VW = RET_HEADS * RET_DV

LANES = 128
HALF = LANES // 2
BF16_ROWS = 16
VMEM_LIMIT = 56 * 1024 * 1024
LOG2E = 1.4426950408889634
MOD_COL_BLOCKS = 8
FFN_CHUNK = 256
TOKEN_TILE = 1024
WIDE_TOKEN_TILE = 1024
RET_CHUNK = 256
RET_CHUNKS_PER_STEP = 4
ATT_Q = 256
VAL_ROWS = NSA_DH + BF16_ROWS
CMP_PARTS = 4
RANK_SPAN = 16
FLASH_SUBTILES = 2
LOOP_TILES = 2
RUN_AHEAD = 6
LOOP_AHEAD = 2


def _dot(a, b):
    return jnp.dot(a, b, preferred_element_type=F32)


def _dot_nt(a, b):
    return lax.dot_general(a, b, (((1,), (1,)), ((), ())), preferred_element_type=F32)


def _dot_tn(a, b):
    return lax.dot_general(a, b, (((0,), (0,)), ((), ())), preferred_element_type=F32)


def _silu(a):
    return a * jax.nn.sigmoid(a)


def _norm_mod(x, nw, sc, sh):
    ms = jnp.mean(x * x, axis=-1, keepdims=True)
    gain = nw * (1.0 + sc)
    return (x * lax.rsqrt(ms + EPS)) * gain + sh


def _full_spec(shape):
    zeros = (0,) * len(shape)
    return pl.BlockSpec(shape, lambda *_: zeros)


def _resident_spec(shape):
    zeros = (0,) * len(shape)
    return pl.BlockSpec(shape, lambda *_: zeros, pipeline_mode=pl.Buffered(1))


def _params(sem):
    return pltpu.CompilerParams(dimension_semantics=sem, vmem_limit_bytes=VMEM_LIMIT)


def _mod_kernel(c_ref, w_ref, b_ref, o_ref):
    c = c_ref[...]
    o_ref[...] = _dot(_silu(c).astype(BF16), w_ref[...].astype(BF16)) + b_ref[...]


def _modulation(c, ada_w, ada_b):
    B, D = c.shape
    N = ada_w.shape[1]
    tn = N // MOD_COL_BLOCKS
    return pl.pallas_call(
        _mod_kernel,
        grid=(N // tn,),
        in_specs=[pl.BlockSpec((B, D), lambda j: (0, 0)),
                  pl.BlockSpec((D, tn), lambda j: (0, j)),
                  pl.BlockSpec((1, tn), lambda j: (0, j))],
        out_specs=pl.BlockSpec((B, tn), lambda j: (0, j)),
        out_shape=jax.ShapeDtypeStruct((B, N), F32),
        compiler_params=_params(("arbitrary",)),
        name="modulation",
    )(c, ada_w, ada_b.reshape(1, N))


def _ffn_kernel(x_ref, mod_ref, nw_ref, fw_ref, wi_ref, wo_ref, o_ref, u_scr, h_scr,
                *, mod_base, final_norm):
    x = x_ref[0]
    sh = mod_ref[0, mod_base:mod_base + 1, :]
    sc = mod_ref[0, mod_base + 1:mod_base + 2, :]
    gt = mod_ref[0, mod_base + 2:mod_base + 3, :]
    u_scr[...] = _norm_mod(x, nw_ref[...], sc, sh).astype(BF16)
    dff = wo_ref.shape[0]
    for c0 in range(0, dff, FFN_CHUNK):
        u = u_scr[...]
        a = _dot(u, wi_ref[:, c0:c0 + FFN_CHUNK])
        b = _dot(u, wi_ref[:, dff + c0:dff + c0 + FFN_CHUNK])
        h_scr[:, c0:c0 + FFN_CHUNK] = (_silu(a) * b).astype(BF16)
    y = x + 0.5 * gt * _dot(h_scr[...], wo_ref[...])
    if final_norm:
        ms = jnp.mean(y * y, axis=-1, keepdims=True)
        y = y * lax.rsqrt(ms + EPS) * fw_ref[...]
    o_ref[0] = y


def _ffn(x, mod, norm_w, final_w, w_in, w_out, *, mod_base, final_norm, tm):
    B, S, D = x.shape
    assert w_out.shape[0] % FFN_CHUNK == 0
    wi = w_in.astype(BF16)
    wo = w_out.astype(BF16)
    kern = functools.partial(_ffn_kernel, mod_base=mod_base, final_norm=final_norm)
    return pl.pallas_call(
        kern,
        grid=(B, S // tm),
        in_specs=[pl.BlockSpec((1, tm, D), lambda b, i: (b, i, 0)),
                  pl.BlockSpec((1, N_MOD, D), lambda b, i: (b, 0, 0)),
                  _full_spec((1, D)), _full_spec((1, D)),
                  _resident_spec(wi.shape), _resident_spec(wo.shape)],
        out_specs=pl.BlockSpec((1, tm, D), lambda b, i: (b, i, 0)),
        out_shape=jax.ShapeDtypeStruct((B, S, D), F32),
        scratch_shapes=[pltpu.VMEM((tm, D), BF16), pltpu.VMEM((tm, w_out.shape[0]), BF16)],
        compiler_params=_params(("arbitrary", "arbitrary")),
        name="ffn_final" if final_norm else "ffn",
    )(x, mod, norm_w.reshape(1, D), final_w.reshape(1, D), wi, wo)


def _proj_kernel(h_ref, mod_ref, nw_ref, wq_ref, wkv_ref, wg_ref, wr_ref, cos_ref, sin_ref,
                 q_ref, cv_ref, kaug_ref, vst_ref, kw_ref, vwt_ref, g_ref, rq_ref, rk_ref, rv_ref,
                 u_scr, cv_scr):
    tm = h_ref.shape[1]
    x = h_ref[0]
    sh = mod_ref[0, 3:4, :]
    sc = mod_ref[0, 4:5, :]
    u_scr[...] = _norm_mod(x, nw_ref[...], sc, sh).astype(BF16)
    u = u_scr[...]

    q_ref[0] = _dot_nt(u, wq_ref[...]).astype(BF16)

    kv = _dot_nt(u, wkv_ref[...])
    for t in range(2):
        cv_scr[t] = kv[:, t * LANES:(t + 1) * LANES]
        for tok in range(CMP_STRIDE):
            rows = cv_scr[t, pl.ds(tok, tm // CMP_STRIDE, stride=CMP_STRIDE), :]
            cv_ref[t, 0, :, tok * LANES:(tok + 1) * LANES] = rows.astype(BF16)
    ks = kv[:, 2 * LANES:3 * LANES]
    lane = lax.broadcasted_iota(jnp.int32, (tm, LANES), 1)
    row = lax.broadcasted_iota(jnp.int32, (tm, LANES), 0)
    low = lane < HALF
    blk = (pl.program_id(1) * tm + row) // SEL_BLOCK
    onehot = jnp.where(lane - HALF == blk, 1.0, 0.0)
    kaug_ref[0, 0] = jnp.where(low, ks, onehot).astype(BF16)
    kaug_ref[0, 1] = jnp.where(low, pltpu.roll(ks, HALF, 1), onehot).astype(BF16)
    kw_ref[0] = kv[:, 4 * LANES:5 * LANES].astype(BF16)
    kt = vst_ref.shape[4]
    ones = jnp.ones((VAL_ROWS - NSA_DH, tm), F32)
    for ref, col in ((vst_ref, 3), (vwt_ref, 5)):
        vt = kv[:, col * LANES:(col + 1) * LANES].T
        for g in range(NSA_GROUPS):
            aug = jnp.concatenate([vt[g * HALF:(g + 1) * HALF], ones], axis=0).astype(BF16)
            for c in range(tm // kt):
                ref[0, g, c] = aug[:, c * kt:(c + 1) * kt]

    g_ref[0] = jax.nn.sigmoid(_dot_nt(u, wg_ref[...]))

    r = _dot_nt(u, wr_ref[...])
    cos = cos_ref[...]
    sin = sin_ref[...]
    first = (lane & (RET_DK // 2)) == 0
    npair = RET_QW // LANES
    for j in range(2 * npair):
        xb = r[:, j * LANES:(j + 1) * LANES]
        partner = jnp.where(first, pltpu.roll(xb, LANES - RET_DK // 2, 1), pltpu.roll(xb, RET_DK // 2, 1))
        y = (xb * cos + partner * sin).astype(BF16)
        if j < npair:
            rq_ref[0, :, j * LANES:(j + 1) * LANES] = y
        else:
            rk_ref[0, :, (j - npair) * LANES:(j - npair + 1) * LANES] = y
    rv_ref[0] = r[:, 2 * RET_QW:2 * RET_QW + RET_VW].astype(BF16)


def _projection(h, mod, norm_w, w_in, *, tm):
    B, S, D = h.shape
    o = np.cumsum([0, NSA_QW] + [NSA_KVW] * 6 + [3 * NSA_HEADS, RET_QW, RET_QW, RET_VW])
    scale_q = NSA_DH ** -0.5 * LOG2E
    scale_k = RET_DK ** -0.5
    wt = w_in.T
    wq = (wt[o[0]:o[1]] * scale_q).astype(BF16)
    wkv = wt[o[1]:o[7]].astype(BF16)
    wg = jnp.pad(wt[o[7]:o[8]], ((0, LANES - 3 * NSA_HEADS), (0, 0))).astype(BF16)
    wr = jnp.concatenate([wt[o[8]:o[9]], wt[o[9]:o[10]] * scale_k, wt[o[10]:o[11]]], axis=0).astype(BF16)
    half = RET_DK // 2
    lane = np.arange(LANES)
    inv = ROPE_BASE ** (-np.arange(half, dtype=np.float64) / half)
    ang = np.arange(S, dtype=np.float64)[:, None] * inv[lane % half][None, :]
    cos_t = jnp.asarray(np.cos(ang), F32)
    sin_t = jnp.asarray(np.where((lane % RET_DK) < half, -np.sin(ang), np.sin(ang)), F32)
    tok = lambda w: pl.BlockSpec((1, tm, w), lambda b, i: (b, i, 0))
    kt = min(ATT_Q, S)
    vt_spec = pl.BlockSpec((1, NSA_GROUPS, tm // kt, VAL_ROWS, kt), lambda b, i: (b, 0, i, 0, 0))
    vt_shape = jax.ShapeDtypeStruct((B, NSA_GROUPS, S // kt, VAL_ROWS, kt), BF16)
    outs = pl.pallas_call(
        _proj_kernel,
        grid=(B, S // tm),
        in_specs=[tok(D), pl.BlockSpec((1, N_MOD, D), lambda b, i: (b, 0, 0)), _full_spec((1, D)),
                  _full_spec(wq.shape), _full_spec(wkv.shape), _full_spec(wg.shape), _full_spec(wr.shape),
                  pl.BlockSpec((tm, LANES), lambda b, i: (i, 0)), pl.BlockSpec((tm, LANES), lambda b, i: (i, 0))],
        out_specs=[tok(NSA_QW),
                   pl.BlockSpec((2, 1, tm // CMP_STRIDE, CMP_STRIDE * LANES), lambda b, i: (0, b, i, 0)),
                   pl.BlockSpec((1, NSA_GROUPS, tm, LANES), lambda b, i: (b, 0, i, 0)),
                   vt_spec, tok(LANES), vt_spec, tok(LANES),
                   tok(RET_QW), tok(RET_QW), tok(RET_VW)],
        out_shape=[jax.ShapeDtypeStruct((B, S, NSA_QW), BF16),
                   jax.ShapeDtypeStruct((2, B, S // CMP_STRIDE, CMP_STRIDE * LANES), BF16),
                   jax.ShapeDtypeStruct((B, NSA_GROUPS, S, LANES), BF16),
                   vt_shape,
                   jax.ShapeDtypeStruct((B, S, LANES), BF16),
                   vt_shape,
                   jax.ShapeDtypeStruct((B, S, LANES), F32),
                   jax.ShapeDtypeStruct((B, S, RET_QW), BF16),
                   jax.ShapeDtypeStruct((B, S, RET_QW), BF16),
                   jax.ShapeDtypeStruct((B, S, RET_VW), BF16)],
        scratch_shapes=[pltpu.VMEM((tm, D), BF16), pltpu.VMEM((2, tm, LANES), F32)],
        compiler_params=_params(("arbitrary", "arbitrary")),
        name="projection",
    )(h, mod, norm_w.reshape(1, D), wq, wkv, wg, wr, cos_t, sin_t)
    return outs


def _compress_kernel(ch_ref, w1c_ref, pe_ref, w1_ref, w2_ref, o_ref, ot_ref):
    ch = ch_ref[0, 0]
    nc = ch.shape[0]
    ab = _dot(ch, w1c_ref[0])
    pt = _dot(pe_ref[0], w1_ref[0])[0:1, :]
    pt2 = jnp.concatenate([pt, pt], axis=1)
    nh = NSA_GROUPS * CMP_HIDDEN
    hid = ab[:, :nh] + pltpu.roll(ab[:, nh:], nc - 1, 0) + pt2
    out = _dot(_silu(hid).astype(BF16), w2_ref[0])
    row = lax.broadcasted_iota(jnp.int32, out.shape, 0)
    out = jnp.where(row < nc - 1, out, 0.0)
    o_ref[0, 0] = out.astype(BF16)
    ot_ref[0, 0, 0:LANES, :] = out.T.astype(BF16)
    ot_ref[0, 0, LANES:, :] = jnp.ones((ot_ref.shape[2] - LANES, nc), BF16)


def _compress(cv, pe, w1, w2):
    _, B, nc, width = cv.shape
    chunks = cv
    eye = jnp.eye(NSA_GROUPS, dtype=F32)
    nh = NSA_GROUPS * CMP_HIDDEN

    def expand(w):
        w = w.reshape(CMP_STRIDE, NSA_DH, CMP_HIDDEN)
        return jnp.einsum("ldj,gh->lgdhj", w, eye).reshape(width, nh)

    half = CMP_STRIDE * NSA_DH
    w1c = jnp.stack([jnp.concatenate([expand(w[:half]), expand(w[half:])], axis=1) for w in w1]).astype(BF16)
    w2bd = jnp.stack([jnp.einsum("jd,gh->gjhd", w, eye).reshape(nh, NSA_KVW) for w in w2]).astype(BF16)
    pe8 = jnp.stack([jnp.broadcast_to(p.reshape(1, CMP_LEN * NSA_DH), (8, CMP_LEN * NSA_DH)) for p in pe]).astype(BF16)
    w1s = jnp.stack(w1).astype(BF16)
    sel = lambda shape: pl.BlockSpec((1,) + shape, lambda t, b: (t,) + (0,) * len(shape))
    return pl.pallas_call(
        _compress_kernel,
        grid=(2, B),
        in_specs=[pl.BlockSpec((1, 1, nc, width), lambda t, b: (t, b, 0, 0)),
                  sel(w1c.shape[1:]), sel(pe8.shape[1:]), sel(w1s.shape[1:]), sel(w2bd.shape[1:])],
        out_specs=[pl.BlockSpec((1, 1, nc, LANES), lambda t, b: (t, b, 0, 0)),
                   pl.BlockSpec((1, 1, LANES + BF16_ROWS, nc), lambda t, b: (t, b, 0, 0))],
        out_shape=[jax.ShapeDtypeStruct((2, B, nc, LANES), BF16),
                   jax.ShapeDtypeStruct((2, B, LANES + BF16_ROWS, nc), BF16)],
        compiler_params=_params(("arbitrary", "arbitrary")),
        name="compress",
    )(chunks, w1c, pe8, w1s, w2bd)


def _select_kernel(q_ref, kc_ref, vct_ref, g_ref, ovl_ref, qaug_ref, ocmp_ref, imp_scr, rank_scr, cap_scr,
                   *, n_sel, n_tiles):
    Q = q_ref.shape[1]
    NC = kc_ref.shape[2]
    NB = ovl_ref.shape[0]
    qi = pl.program_id(1)
    q0 = qi * Q
    gt = g_ref[0].T
    any_allowed = q0 + lax.broadcasted_iota(jnp.int32, (1, Q), 1) >= CMP_LEN - 1
    j_blk = lax.broadcasted_iota(jnp.int32, (NB, Q), 0)
    t_blk = q0 + lax.broadcasted_iota(jnp.int32, (NB, Q), 1)
    cur = t_blk // SEL_BLOCK
    forced = (j_blk == 0) | (j_blk == cur) | (j_blk == cur - 1)
    valid = j_blk * SEL_BLOCK <= t_blk
    j_sub = lax.broadcasted_iota(jnp.int32, (8, Q), 0)
    per_tile = Q // SEL_BLOCK

    def compressed_branch(rows):
        kcf = kc_ref[0, 0, 0:rows, :].astype(F32)
        kcr = pltpu.roll(kcf, HALF, 1)
        lowk = lax.broadcasted_iota(jnp.int32, (rows, LANES), 1) < HALF
        k_even = [jnp.where(lowk, kcf, 0.0).astype(BF16), jnp.where(lowk, kcr, 0.0).astype(BF16)]
        k_odd = [jnp.where(lowk, 0.0, kcr).astype(BF16), jnp.where(lowk, 0.0, kcf).astype(BF16)]
        vct = vct_ref[0, 0, :, 0:rows]
        ovl = ovl_ref[:, 0:rows]
        n_row = lax.broadcasted_iota(jnp.int32, (rows, Q), 0)
        t_col = q0 + lax.broadcasted_iota(jnp.int32, (rows, Q), 1)
        cap_scr[0:rows, :] = jnp.where(n_row * CMP_STRIDE + (CMP_LEN - 1) <= t_col, -NEG, NEG)

        scores = []
        for pb in range(NSA_HEADS // 2):
            g, pp = divmod(pb, NSA_HPG // 2)
            qt = q_ref[0, :, pb * LANES:(pb + 1) * LANES].astype(F32).T.astype(BF16)
            for e, kk in enumerate((k_even[g], k_odd[g])):
                scores.append(_dot(kk, qt))
                col = (2 * pp + e) * Q
                qaug_ref[0, g, 0, 0:HALF, col:col + Q] = qt[e * HALF:(e + 1) * HALF]

        for g in range(NSA_GROUPS):
            imp = jnp.zeros((NB, Q), F32)
            for hh in range(NSA_HPG):
                h = g * NSA_HPG + hh
                s = jnp.minimum(scores[h], cap_scr[0:rows, :])
                mx = jnp.max(s, axis=0, keepdims=True)
                ex = jnp.exp2(s - mx).astype(BF16)
                oa = _dot(vct, ex)
                inv = jnp.where(any_allowed, 1.0 / oa[LANES:LANES + 1], 0.0)
                ocmp_ref[0, 0, h * NSA_DH:(h + 1) * NSA_DH, :] = (gt[3 * h:3 * h + 1] * inv) * oa[g * HALF:(g + 1) * HALF]
                imp = imp + _dot(ovl, ex) * inv

            imp = jnp.where(forced, imp + FORCE_BONUS, imp)
            imp_scr[g] = jnp.where(valid, imp, -FORCE_BONUS)

    parts = CMP_PARTS if n_tiles % CMP_PARTS == 0 and NC % (16 * CMP_PARTS) == 0 else 1
    for c in range(parts):
        @pl.when((qi >= c * (n_tiles // parts)) & (qi < (c + 1) * (n_tiles // parts)))
        def _():
            compressed_branch((c + 1) * (NC // parts))
    rank_scr[...] = jnp.zeros_like(rank_scr)

    for c0 in range(0, NB, RANK_SPAN):
        @pl.when(c0 // per_tile <= qi)
        def _():
            slabs = range(NB // 8)
            for g in range(NSA_GROUPS):
                vals = [imp_scr[g, 8 * sb:8 * sb + 8, :] for sb in slabs]
                cnts = [rank_scr[g, 8 * sb:8 * sb + 8, :] for sb in slabs]
                for jp in range(c0, c0 + RANK_SPAN):
                    r = jnp.broadcast_to(imp_scr[g, jp:jp + 1, :], (8, Q))
                    for sb in slabs:
                        v = vals[sb]
                        if 8 * sb > jp:
                            ahead = r >= v
                        elif 8 * sb + 7 <= jp:
                            ahead = r > v
                        else:
                            ahead = (r > v) | ((r == v) & (j_sub + 8 * sb > jp))
                        cnts[sb] = jnp.where(ahead, cnts[sb] + 1, cnts[sb])
                for sb in slabs:
                    rank_scr[g, 8 * sb:8 * sb + 8, :] = cnts[sb]

    for g in range(NSA_GROUPS):
        sel = (rank_scr[g] < n_sel) & valid
        bias = jnp.where(sel, 0.0, NEG).astype(BF16)
        for hh in range(NSA_HPG):
            qaug_ref[0, g, 0, HALF:HALF + NB, hh * Q:(hh + 1) * Q] = bias


def _select(q, kc, vct, gates):
    B, S, _ = q.shape
    Q = min(ATT_Q, S)
    NC = kc.shape[2]
    nc, ns = NC - 1, S // SEL_BLOCK
    NB = LANES - HALF
    assert ns <= NB and Q % SEL_BLOCK == 0
    c_start = np.arange(nc) * CMP_STRIDE
    s_start = np.arange(ns) * SEL_BLOCK
    overlap = ((c_start[:, None] < s_start[None, :] + SEL_BLOCK) &
               (c_start[:, None] + CMP_LEN > s_start[None, :])).astype(np.float32)
    ovl = np.zeros((NB, NC), np.float32)
    ovl[:ns, :nc] = overlap.T
    kern = functools.partial(_select_kernel, n_sel=min(SEL_TOPK, ns), n_tiles=S // Q)
    return pl.pallas_call(
        kern,
        grid=(B, S // Q),
        in_specs=[pl.BlockSpec((1, Q, NSA_QW), lambda b, i: (b, i, 0)),
                  pl.BlockSpec((1, 1, NC, LANES), lambda b, i: (0, b, 0, 0)),
                  pl.BlockSpec((1, 1) + vct.shape[2:], lambda b, i: (1, b, 0, 0)),
                  pl.BlockSpec((1, Q, LANES), lambda b, i: (b, i, 0)),
                  _full_spec((NB, NC))],
        out_specs=[pl.BlockSpec((1, NSA_GROUPS, 1, LANES, NSA_HPG * Q), lambda b, i: (b, 0, i, 0, 0)),
                   pl.BlockSpec((1, 1, NSA_QW, Q), lambda b, i: (b, i, 0, 0))],
        out_shape=[jax.ShapeDtypeStruct((B, NSA_GROUPS, S // Q, LANES, NSA_HPG * Q), BF16),
                   jax.ShapeDtypeStruct((B, S // Q, NSA_QW, Q), F32)],
        scratch_shapes=[pltpu.VMEM((NSA_GROUPS, NB, Q), F32), pltpu.VMEM((NSA_GROUPS, NB, Q), jnp.int32),
                        pltpu.VMEM((NC, Q), F32)],
        compiler_params=_params(("arbitrary", "arbitrary")),
        name="select",
    )(q, kc, vct, gates, jnp.asarray(ovl, BF16))


def _flash_kernel(qaug_ref, kaug_ref, vst_ref, kw_ref, vwt_ref, g_ref, ocmp_ref, o_ref,
                  m_scr, acc_scr, qw_scr, cap_scr, s_scr):
    QS = qaug_ref.shape[2]
    R = qaug_ref.shape[4]
    Q = R // NSA_HPG
    KT = kaug_ref.shape[2] // vst_ref.shape[2]
    WT = WINDOW // KT
    SLC, WIN = 0, 1
    CAUSAL, EDGE = 0, 1
    t0 = pl.program_id(1) * QS
    k_row = lax.broadcasted_iota(jnp.int32, (KT, Q), 0)
    i_col = lax.broadcasted_iota(jnp.int32, (KT, Q), 1)
    big = -NEG
    cap_scr[CAUSAL] = jnp.where(k_row <= i_col, big, NEG)
    cap_scr[EDGE] = jnp.where(k_row > i_col, big, NEG)
    m_scr[SLC] = jnp.full(m_scr.shape[1:], NEG, F32)
    acc_scr[SLC] = jnp.zeros(acc_scr.shape[1:], F32)
    zeros = jnp.zeros((HALF, R), BF16)
    for qs in range(QS):
        qw_scr[qs, 0] = jnp.concatenate([qaug_ref[0, 0, qs, 0:HALF, :], zeros], axis=0)
        qw_scr[qs, 1] = jnp.concatenate([zeros, qaug_ref[0, 1, qs, 0:HALF, :]], axis=0)

    def scores(ch):
        br, qs, g, hh, kj, _, _ = ch
        off = pl.multiple_of(kj * KT, KT)
        if br == SLC:
            return _dot(kaug_ref[0, g, pl.ds(off, KT), :], qaug_ref[0, g, qs, :, hh * Q:(hh + 1) * Q])
        return _dot(kw_ref[0, pl.ds(off, KT), :], qw_scr[qs, g, :, hh * Q:(hh + 1) * Q])

    def absorb(ch, s):
        br, qs, g, hh, kj, cap, fresh = ch
        cols = slice(hh * Q, (hh + 1) * Q)
        if cap is not None:
            s = jnp.minimum(s, cap_scr[cap])
        vt = vst_ref[0, g, kj] if br == SLC else vwt_ref[0, g, kj]
        if fresh:
            m_new = jnp.max(s, axis=0, keepdims=True)
            acc_scr[br, qs, g, :, cols] = _dot(vt, jnp.exp2(s - m_new).astype(BF16))
        else:
            m_prev = m_scr[br, qs, g, :, cols]
            m_new = jnp.maximum(m_prev, jnp.max(s, axis=0, keepdims=True))
            p = jnp.exp2(s - m_new)
            alpha = jnp.exp2(m_prev - m_new)
            acc_scr[br, qs, g, :, cols] = alpha * acc_scr[br, qs, g, :, cols] + _dot(vt, p.astype(BF16))
        m_scr[br, qs, g, :, cols] = m_new

    NSH = QS * NSA_HEADS
    RING = 2 * NSH

    def run(chains, ready=()):
        order = list(ready) + list(chains)
        assert len(ready) + RUN_AHEAD < RING
        for t in range(-RUN_AHEAD, len(order)):
            n = t + RUN_AHEAD
            if len(ready) <= n < len(order):
                s_scr[(n % RING) // NSH, n % NSH] = scores(order[n])
            if t >= 0:
                absorb(order[t], s_scr[(t % RING) // NSH, t % NSH])

    def tile(br, qs, kj, cap, fresh=False):
        return [(br, qs, g, hh, kj, cap, fresh) for g in range(NSA_GROUPS) for hh in range(NSA_HPG)]

    def shared(kj, first_cap=None):
        return [ch for qs in range(QS) for ch in tile(SLC, qs, kj, first_cap if qs == 0 else None)]

    def issue(kj, slot, n):
        s_scr[slot, n] = scores(shared(kj)[n])

    for n in range(NSH):
        issue(0, 0, n)

    def pipelined(first, count):
        produce = [(first + k + 1, (k + 1) % 2, n) for k in range(count) for n in range(NSH)]
        absorbs = [(first + k, k % 2, n) for k in range(count) for n in range(NSH)]
        for t in range(-LOOP_AHEAD, len(absorbs)):
            if t + LOOP_AHEAD < len(produce):
                issue(*produce[t + LOOP_AHEAD])
            if t >= 0:
                kj, slot, n = absorbs[t]
                absorb(shared(kj)[n], s_scr[slot, n])

    def slc_body(j, carry):
        pipelined(LOOP_TILES * j, LOOP_TILES)
        return carry

    lax.fori_loop(0, t0 // LOOP_TILES, slc_body, 0)
    for left in range(2, LOOP_TILES, 2):
        @pl.when(t0 % LOOP_TILES == left)
        def _():
            pipelined(t0 - left, left)
    ready = shared(t0, CAUSAL)

    always = []
    later = {}
    for qs in range(QS):
        for kk in range(1, qs):
            always += tile(SLC, qs, t0 + kk, None)
        if qs > 0:
            always += tile(SLC, qs, t0 + qs, CAUSAL)
        always += tile(WIN, qs, t0 + qs, CAUSAL, fresh=True)
        for back in range(1, WT + 1):
            chains = tile(WIN, qs, t0 + qs - back, EDGE if back == WT else None)
            if back <= qs:
                always += chains
            else:
                later.setdefault(-(-(back - qs) // QS) * QS, []).extend(chains)
    if len(later) == 1:
        (need, chains), = later.items()

        @pl.when(t0 >= need)
        def _():
            run(always + chains, ready)

        @pl.when(t0 < need)
        def _():
            run(always, ready)
    else:
        run(always, ready)
        for need, chains in sorted(later.items()):
            @pl.when(t0 >= need)
            def _():
                run(chains)

    for qs in range(QS):
        gt = g_ref[0, qs * Q:(qs + 1) * Q, :].T
        for pb in range(NSA_HEADS // 2):
            halves = []
            for e in range(2):
                h = 2 * pb + e
                g, hh = divmod(h, NSA_HPG)
                cols = slice(hh * Q, (hh + 1) * Q)
                o = ocmp_ref[0, qs, h * NSA_DH:(h + 1) * NSA_DH, :]
                for br in (SLC, WIN):
                    scale = gt[3 * h + 1 + br:3 * h + 2 + br] / acc_scr[br, qs, g, NSA_DH:NSA_DH + 1, cols]
                    o = o + scale * acc_scr[br, qs, g, 0:NSA_DH, cols]
                halves.append(o)
            o_ref[0, qs * Q:(qs + 1) * Q, pb * LANES:(pb + 1) * LANES] = jnp.concatenate(halves, axis=0).T.astype(BF16)


def _flash(qaug, kaug, vst, kw, vwt, gates, ocmp):
    B, _, NQ, _, R = qaug.shape
    Q = R // NSA_HPG
    S = NQ * Q
    QS = min(FLASH_SUBTILES, NQ)
    assert WINDOW % Q == 0 and vst.shape[4] == Q and NQ % QS == 0 and QS % 2 == 0
    per_b = lambda shape: pl.BlockSpec((1,) + shape, lambda b, i: (b,) + (0,) * len(shape))
    return pl.pallas_call(
        _flash_kernel,
        grid=(B, NQ // QS),
        in_specs=[pl.BlockSpec((1, NSA_GROUPS, QS, LANES, R), lambda b, i: (b, 0, i, 0, 0)),
                  per_b((NSA_GROUPS, S, LANES)), per_b(vst.shape[1:]), per_b((S, LANES)), per_b(vwt.shape[1:]),
                  pl.BlockSpec((1, QS * Q, LANES), lambda b, i: (b, i, 0)),
                  pl.BlockSpec((1, QS, NSA_QW, Q), lambda b, i: (b, i, 0, 0))],
        out_specs=pl.BlockSpec((1, QS * Q, NSA_QW), lambda b, i: (b, i, 0)),
        out_shape=jax.ShapeDtypeStruct((B, S, NSA_QW), BF16),
        scratch_shapes=[pltpu.VMEM((2, QS, NSA_GROUPS, 1, R), F32),
                        pltpu.VMEM((2, QS, NSA_GROUPS, VAL_ROWS, R), F32),
                        pltpu.VMEM((QS, NSA_GROUPS, LANES, R), BF16), pltpu.VMEM((2, Q, Q), F32),
                        pltpu.VMEM((2, QS * NSA_HEADS, Q, Q), F32)],
        compiler_params=_params(("arbitrary", "arbitrary")),
        name="flash",
    )(qaug, kaug, vst, kw, vwt, gates, ocmp)


def _retention_kernel(rq_ref, rk_ref, rv_ref, dmat_ref, qdec_ref, kdec_ref, cdec_ref, nw_ref, o_ref, st_scr):
    C = dmat_ref.shape[1]

    @pl.when(pl.program_id(1) == 0)
    def _():
        st_scr[...] = jnp.zeros_like(st_scr)

    low = lax.broadcasted_iota(jnp.int32, (C, LANES), 1) < HALF
    same_head = ((lax.broadcasted_iota(jnp.int32, (LANES, LANES), 0) < HALF) ==
                 (lax.broadcasted_iota(jnp.int32, (LANES, LANES), 1) < HALF))
    head_mean = jnp.where(same_head, 1.0 / RET_DV, 0.0).astype(BF16)
    n_chunks = rq_ref.shape[1] // C
    pairs = [(cc, p) for cc in range(n_chunks) for p in range(RET_HEADS // 2)]
    rows = lambda cc: slice(cc * C, (cc + 1) * C)
    lanes = lambda p: slice(p * LANES, (p + 1) * LANES)
    state = [st_scr[p] for p in range(RET_HEADS // 2)]
    scores, cross = {}, {}
    for cc, p in pairs:
        q2, k2, v2 = rq_ref[0, rows(cc), lanes(p)], rk_ref[0, rows(cc), lanes(p)], rv_ref[0, rows(cc), lanes(p)]
        q2f = q2.astype(F32)
        qe = jnp.where(low, q2f, 0.0).astype(BF16)
        qo = jnp.where(low, 0.0, q2f).astype(BF16)
        scores[cc, p] = (_dot_nt(qe, k2), _dot_nt(qo, k2))
        cross[cc, p] = _dot(q2, state[p].astype(BF16))
        kd = (k2.astype(F32) * kdec_ref[:, lanes(p)]).astype(BF16)
        state[p] = state[p] * cdec_ref[p:p + 1, :] + jnp.where(same_head, _dot_tn(kd, v2), 0.0)
    for p in range(RET_HEADS // 2):
        st_scr[p] = state[p]
    outs = {}
    for cc, p in pairs:
        v2 = rv_ref[0, rows(cc), lanes(p)]
        ie = (scores[cc, p][0] * dmat_ref[2 * p]).astype(BF16)
        io = (scores[cc, p][1] * dmat_ref[2 * p + 1]).astype(BF16)
        intra = jnp.where(low, _dot(ie, v2), _dot(io, v2))
        outs[cc, p] = intra + cross[cc, p] * qdec_ref[:, lanes(p)]
    means = {key: _dot(o.astype(BF16), head_mean) for key, o in outs.items()}
    devs = {key: outs[key] - means[key] for key in outs}
    variances = {key: _dot((d * d).astype(BF16), head_mean) for key, d in devs.items()}
    for cc, p in pairs:
        o_ref[0, rows(cc), lanes(p)] = devs[cc, p] * lax.rsqrt(variances[cc, p] + EPS) * nw_ref[:, lanes(p)]


def _retention(rq, rk, rv, ret_norm_w):
    B, S, _ = rq.shape
    C = min(RET_CHUNK, S)
    H = RET_HEADS
    log_g = np.log(1.0 - 2.0 ** (-5.0 - np.arange(H, dtype=np.float64)))
    i = np.arange(C, dtype=np.float64)
    diff = i[:, None] - i[None, :]
    as_f32 = lambda a: jnp.asarray(a, F32)
    dmat = as_f32(np.where(diff >= 0, np.exp(log_g[:, None, None] * np.maximum(diff, 0.0)), 0.0))
    per_lane = lambda a: np.repeat(a, RET_DK, axis=-1)
    qdec = as_f32(per_lane(np.exp(log_g[None, :] * (i[:, None] + 1.0))))
    kdec = as_f32(per_lane(np.exp(log_g[None, :] * (C - 1.0 - i[:, None]))))
    cdec = as_f32(per_lane(np.exp(log_g * C)[None, :]).reshape(H // 2, LANES))
    span = C * min(RET_CHUNKS_PER_STEP, S // C)
    tok = lambda w: pl.BlockSpec((1, span, w), lambda b, i: (b, i, 0))
    return pl.pallas_call(
        _retention_kernel,
        grid=(B, S // span),
        in_specs=[tok(RET_QW), tok(RET_QW), tok(RET_VW),
                  _full_spec((H, C, C)), _full_spec((C, RET_QW)), _full_spec((C, RET_QW)),
                  _full_spec((H // 2, LANES)), _full_spec((1, RET_VW))],
        out_specs=tok(RET_VW),
        out_shape=jax.ShapeDtypeStruct((B, S, RET_VW), F32),
        scratch_shapes=[pltpu.VMEM((H // 2, LANES, LANES), F32)],
        compiler_params=_params(("arbitrary", "arbitrary")),
        name="retention",
    )(rq, rk, rv, dmat, qdec, kdec, cdec, ret_norm_w.reshape(1, RET_VW))


def _mix_kernel(h_ref, mod_ref, nw_ref, onsa_ref, oret_ref, wg_ref, wn_ref, wr_ref, wo_ref, o_ref):
    x = h_ref[0]
    sh = mod_ref[0, 3:4, :]
    sc = mod_ref[0, 4:5, :]
    gt = mod_ref[0, 5:6, :]
    u = _norm_mod(x, nw_ref[...], sc, sh).astype(BF16)
    gates = _dot_nt(u, wg_ref[...])
    y_nsa = _dot(onsa_ref[0], wn_ref[...])
    oret = (oret_ref[0] * _silu(gates[:, 0:RET_VW])).astype(BF16)
    y_ret = _dot(oret, wr_ref[...])
    D = x.shape[1]
    ga = jax.nn.sigmoid(gates[:, RET_VW:RET_VW + D])
    gb = jax.nn.sigmoid(gates[:, RET_VW + D:RET_VW + 2 * D])
    mixed = (ga * y_nsa + gb * y_ret).astype(BF16)
    o_ref[0] = x + gt * _dot(mixed, wo_ref[...])


def _mix(h, mod, norm_w, o_nsa, o_ret, w_in, w_nsa_up, w_ret_up, w_out, *, tm):
    B, S, D = h.shape
    o = np.cumsum([0, NSA_QW] + [NSA_KVW] * 6 + [3 * NSA_HEADS, RET_QW, RET_QW, RET_VW, RET_VW, D, D])
    wt = w_in.T
    wg = wt[o[11]:o[14]].astype(BF16)
    wn, wr, wo = w_nsa_up.astype(BF16), w_ret_up.astype(BF16), w_out.astype(BF16)
    tok = lambda w: pl.BlockSpec((1, tm, w), lambda b, i: (b, i, 0))
    return pl.pallas_call(
        _mix_kernel,
        grid=(B, S // tm),
        in_specs=[tok(D), pl.BlockSpec((1, N_MOD, D), lambda b, i: (b, 0, 0)), _full_spec((1, D)),
                  tok(NSA_QW), tok(RET_VW),
                  _resident_spec(wg.shape),
                  _resident_spec(wn.shape), _resident_spec(wr.shape), _resident_spec(wo.shape)],
        out_specs=tok(D),
        out_shape=jax.ShapeDtypeStruct((B, S, D), F32),
        compiler_params=_params(("arbitrary", "arbitrary")),
        name="mix",
    )(h, mod, norm_w.reshape(1, D), o_nsa, o_ret, wg, wn, wr, wo)


def kernel(x, c, ada_w, ada_b, norm1_w, ffn1_w_in, ffn1_w_out, norm2_w, w_in, cmp_k_pe, cmp_k_w1, cmp_k_w2,
           cmp_v_pe, cmp_v_w1, cmp_v_w2, ret_norm_w, w_nsa_up, w_ret_up, w_out, norm3_w, ffn2_w_in, ffn2_w_out,
           final_norm_w):
    B, S, D = x.shape
    depth = ada_w.shape[0]
    assert depth >= 1
    tm = min(TOKEN_TILE, S)
    tm_wide = min(WIDE_TOKEN_TILE, S)
    h = x
    for l in range(depth):
        last = l == depth - 1
        mod = _modulation(c, ada_w[l], ada_b[l]).reshape(B, N_MOD, D)
        h = _ffn(h, mod, norm1_w[l], final_norm_w, ffn1_w_in[l], ffn1_w_out[l], mod_base=0, final_norm=False, tm=tm_wide)
        q, cv, kaug, vst, kw, vwt, gates, rq, rk, rv = _projection(h, mod, norm2_w[l], w_in[l], tm=tm)
        kvc, kvct = _compress(cv, (cmp_k_pe[l], cmp_v_pe[l]), (cmp_k_w1[l], cmp_v_w1[l]), (cmp_k_w2[l], cmp_v_w2[l]))
        qaug, ocmp = _select(q, kvc, kvct, gates)
        o_nsa = _flash(qaug, kaug, vst, kw, vwt, gates, ocmp)
        o_ret = _retention(rq, rk, rv, ret_norm_w[l])
        h = _mix(h, mod, norm2_w[l], o_nsa, o_ret, w_in[l], w_nsa_up[l], w_ret_up[l], w_out[l], tm=tm_wide)
        h = _ffn(h, mod, norm3_w[l], final_norm_w, ffn2_w_in[l], ffn2_w_out[l], mod_base=6, final_norm=last, tm=tm_wide)
    return h
```

```python
import functools

import numpy as np
import jax
import jax.numpy as jnp
from jax import lax
from jax.experimental import pallas as pl
from jax.experimental.pallas import tpu as pltpu

F32 = jnp.float32
BF16 = jnp.bfloat16

NSA_HEADS = 8
NSA_GROUPS = 2
NSA_HPG = NSA_HEADS // NSA_GROUPS
NSA_DH = 64
CMP_STRIDE = 16
CMP_LEN = 2 * CMP_STRIDE
CMP_HIDDEN = 128
SEL_BLOCK = 64
SEL_TOPK = 16
WINDOW = 512
FORCE_BONUS = 1e4
RET_HEADS = 8
RET_DK = 64
RET_DV = 64
ROPE_BASE = 10000.0
EPS = 1e-6
NEG = -1e30
N_MOD = 9

NSA_QW = NSA_HEADS * NSA_DH
NSA_KVW = NSA_GROUPS * NSA_DH
RET_QW = RET_HEADS * RET_DK
RET_VW = RET_HEADS * RET_DV

LANES = 128
HALF = LANES // 2
BF16_ROWS = 16
VMEM_LIMIT = 56 * 1024 * 1024
LOG2E = 1.4426950408889634
MOD_COL_BLOCKS = 8
FFN_CHUNK = 256
TOKEN_TILE = 1024
WIDE_TOKEN_TILE = 1024
RET_CHUNK = 256
RET_CHUNKS_PER_STEP = 4
ATT_Q = 256
VAL_ROWS = NSA_DH + BF16_ROWS
CMP_PARTS = 4
RANK_SPAN = 16
FLASH_SUBTILES = 2
LOOP_TILES = 2
RUN_AHEAD = 6
LOOP_AHEAD = 2


def _dot(a, b):
    return jnp.dot(a, b, preferred_element_type=F32)


def _dot_nt(a, b):
    return lax.dot_general(a, b, (((1,), (1,)), ((), ())), preferred_element_type=F32)


def _dot_tn(a, b):
    return lax.dot_general(a, b, (((0,), (0,)), ((), ())), preferred_element_type=F32)


def _silu(a):
    return a * jax.nn.sigmoid(a)


def _norm_mod(x, nw, sc, sh):
    ms = jnp.mean(x * x, axis=-1, keepdims=True)
    gain = nw * (1.0 + sc)
    return (x * lax.rsqrt(ms + EPS)) * gain + sh


def _full_spec(shape):
    zeros = (0,) * len(shape)
    return pl.BlockSpec(shape, lambda *_: zeros)


def _resident_spec(shape):
    zeros = (0,) * len(shape)
    return pl.BlockSpec(shape, lambda *_: zeros, pipeline_mode=pl.Buffered(1))


def _params(sem):
    return pltpu.CompilerParams(dimension_semantics=sem, vmem_limit_bytes=VMEM_LIMIT)


def _mod_kernel(c_ref, w_ref, b_ref, o_ref):
    c = c_ref[...]
    o_ref[...] = _dot(_silu(c).astype(BF16), w_ref[...].astype(BF16)) + b_ref[...]


def _modulation(c, ada_w, ada_b):
    B, D = c.shape
    N = ada_w.shape[1]
    tn = N // MOD_COL_BLOCKS
    return pl.pallas_call(
        _mod_kernel,
        grid=(N // tn,),
        in_specs=[pl.BlockSpec((B, D), lambda j: (0, 0)),
                  pl.BlockSpec((D, tn), lambda j: (0, j)),
                  pl.BlockSpec((1, tn), lambda j: (0, j))],
        out_specs=pl.BlockSpec((B, tn), lambda j: (0, j)),
        out_shape=jax.ShapeDtypeStruct((B, N), F32),
        compiler_params=_params(("arbitrary",)),
        name="modulation",
    )(c, ada_w, ada_b.reshape(1, N))


def _ffn_kernel(x_ref, mod_ref, nw_ref, fw_ref, wi_ref, wo_ref, o_ref, u_scr, h_scr,
                *, mod_base, final_norm):
    x = x_ref[0]
    sh = mod_ref[0, mod_base:mod_base + 1, :]
    sc = mod_ref[0, mod_base + 1:mod_base + 2, :]
    gt = mod_ref[0, mod_base + 2:mod_base + 3, :]
    u_scr[...] = _norm_mod(x, nw_ref[...], sc, sh).astype(BF16)
    dff = wo_ref.shape[0]
    for c0 in range(0, dff, FFN_CHUNK):
        u = u_scr[...]
        a = _dot(u, wi_ref[:, c0:c0 + FFN_CHUNK])
        b = _dot(u, wi_ref[:, dff + c0:dff + c0 + FFN_CHUNK])
        h_scr[:, c0:c0 + FFN_CHUNK] = (_silu(a) * b).astype(BF16)
    y = x + 0.5 * gt * _dot(h_scr[...], wo_ref[...])
    if final_norm:
        ms = jnp.mean(y * y, axis=-1, keepdims=True)
        y = y * lax.rsqrt(ms + EPS) * fw_ref[...]
    o_ref[0] = y


def _ffn(x, mod, norm_w, final_w, w_in, w_out, *, mod_base, final_norm, tm):
    B, S, D = x.shape
    assert w_out.shape[0] % FFN_CHUNK == 0
    wi = w_in.astype(BF16)
    wo = w_out.astype(BF16)
    kern = functools.partial(_ffn_kernel, mod_base=mod_base, final_norm=final_norm)
    return pl.pallas_call(
        kern,
        grid=(B, S // tm),
        in_specs=[pl.BlockSpec((1, tm, D), lambda b, i: (b, i, 0)),
                  pl.BlockSpec((1, N_MOD, D), lambda b, i: (b, 0, 0)),
                  _full_spec((1, D)), _full_spec((1, D)),
                  _resident_spec(wi.shape), _resident_spec(wo.shape)],
        out_specs=pl.BlockSpec((1, tm, D), lambda b, i: (b, i, 0)),
        out_shape=jax.ShapeDtypeStruct((B, S, D), F32),
        scratch_shapes=[pltpu.VMEM((tm, D), BF16), pltpu.VMEM((tm, w_out.shape[0]), BF16)],
        compiler_params=_params(("arbitrary", "arbitrary")),
        name="ffn_final" if final_norm else "ffn",
    )(x, mod, norm_w.reshape(1, D), final_w.reshape(1, D), wi, wo)


def _proj_kernel(h_ref, mod_ref, nw_ref, wq_ref, wkv_ref, wg_ref, wr_ref, cos_ref, sin_ref,
                 q_ref, cv_ref, kaug_ref, vst_ref, kw_ref, vwt_ref, g_ref, rq_ref, rk_ref, rv_ref,
                 u_scr, cv_scr):
    tm = h_ref.shape[1]
    x = h_ref[0]
    sh = mod_ref[0, 3:4, :]
    sc = mod_ref[0, 4:5, :]
    u_scr[...] = _norm_mod(x, nw_ref[...], sc, sh).astype(BF16)
    u = u_scr[...]

    q_ref[0] = _dot_nt(u, wq_ref[...]).astype(BF16)

    kv = _dot_nt(u, wkv_ref[...])
    for t in range(2):
        cv_scr[t] = kv[:, t * LANES:(t + 1) * LANES]
        for tok in range(CMP_STRIDE):
            rows = cv_scr[t, pl.ds(tok, tm // CMP_STRIDE, stride=CMP_STRIDE), :]
            cv_ref[t, 0, :, tok * LANES:(tok + 1) * LANES] = rows.astype(BF16)
    ks = kv[:, 2 * LANES:3 * LANES]
    lane = lax.broadcasted_iota(jnp.int32, (tm, LANES), 1)
    row = lax.broadcasted_iota(jnp.int32, (tm, LANES), 0)
    low = lane < HALF
    blk = (pl.program_id(1) * tm + row) // SEL_BLOCK
    onehot = jnp.where(lane - HALF == blk, 1.0, 0.0)
    kaug_ref[0, 0] = jnp.where(low, ks, onehot).astype(BF16)
    kaug_ref[0, 1] = jnp.where(low, pltpu.roll(ks, HALF, 1), onehot).astype(BF16)
    kw_ref[0] = kv[:, 4 * LANES:5 * LANES].astype(BF16)
    kt = vst_ref.shape[4]
    ones = jnp.ones((VAL_ROWS - NSA_DH, tm), F32)
    for ref, col in ((vst_ref, 3), (vwt_ref, 5)):
        vt = kv[:, col * LANES:(col + 1) * LANES].T
        for g in range(NSA_GROUPS):
            aug = jnp.concatenate([vt[g * HALF:(g + 1) * HALF], ones], axis=0).astype(BF16)
            for c in range(tm // kt):
                ref[0, g, c] = aug[:, c * kt:(c + 1) * kt]

    g_ref[0] = jax.nn.sigmoid(_dot_nt(u, wg_ref[...]))

    r = _dot_nt(u, wr_ref[...])
    cos = cos_ref[...]
    sin = sin_ref[...]
    first = (lane & (RET_DK // 2)) == 0
    npair = RET_QW // LANES
    for j in range(2 * npair):
        xb = r[:, j * LANES:(j + 1) * LANES]
        partner = jnp.where(first, pltpu.roll(xb, LANES - RET_DK // 2, 1), pltpu.roll(xb, RET_DK // 2, 1))
        y = (xb * cos + partner * sin).astype(BF16)
        if j < npair:
            rq_ref[0, :, j * LANES:(j + 1) * LANES] = y
        else:
            rk_ref[0, :, (j - npair) * LANES:(j - npair + 1) * LANES] = y
    rv_ref[0] = r[:, 2 * RET_QW:2 * RET_QW + RET_VW].astype(BF16)


def _projection(h, mod, norm_w, w_in, *, tm):
    B, S, D = h.shape
    o = np.cumsum([0, NSA_QW] + [NSA_KVW] * 6 + [3 * NSA_HEADS, RET_QW, RET_QW, RET_VW])
    scale_q = NSA_DH ** -0.5 * LOG2E
    scale_k = RET_DK ** -0.5
    wt = w_in.T
    wq = (wt[o[0]:o[1]] * scale_q).astype(BF16)
    wkv = wt[o[1]:o[7]].astype(BF16)
    wg = jnp.pad(wt[o[7]:o[8]], ((0, LANES - 3 * NSA_HEADS), (0, 0))).astype(BF16)
    wr = jnp.concatenate([wt[o[8]:o[9]], wt[o[9]:o[10]] * scale_k, wt[o[10]:o[11]]], axis=0).astype(BF16)
    half = RET_DK // 2
    lane = np.arange(LANES)
    inv = ROPE_BASE ** (-np.arange(half, dtype=np.float64) / half)
    ang = np.arange(S, dtype=np.float64)[:, None] * inv[lane % half][None, :]
    cos_t = jnp.asarray(np.cos(ang), F32)
    sin_t = jnp.asarray(np.where((lane % RET_DK) < half, -np.sin(ang), np.sin(ang)), F32)
    tok = lambda w: pl.BlockSpec((1, tm, w), lambda b, i: (b, i, 0))
    kt = min(ATT_Q, S)
    vt_spec = pl.BlockSpec((1, NSA_GROUPS, tm // kt, VAL_ROWS, kt), lambda b, i: (b, 0, i, 0, 0))
    vt_shape = jax.ShapeDtypeStruct((B, NSA_GROUPS, S // kt, VAL_ROWS, kt), BF16)
    outs = pl.pallas_call(
        _proj_kernel,
        grid=(B, S // tm),
        in_specs=[tok(D), pl.BlockSpec((1, N_MOD, D), lambda b, i: (b, 0, 0)), _full_spec((1, D)),
                  _full_spec(wq.shape), _full_spec(wkv.shape), _full_spec(wg.shape), _full_spec(wr.shape),
                  pl.BlockSpec((tm, LANES), lambda b, i: (i, 0)), pl.BlockSpec((tm, LANES), lambda b, i: (i, 0))],
        out_specs=[tok(NSA_QW),
                   pl.BlockSpec((2, 1, tm // CMP_STRIDE, CMP_STRIDE * LANES), lambda b, i: (0, b, i, 0)),
                   pl.BlockSpec((1, NSA_GROUPS, tm, LANES), lambda b, i: (b, 0, i, 0)),
                   vt_spec, tok(LANES), vt_spec, tok(LANES),
                   tok(RET_QW), tok(RET_QW), tok(RET_VW)],
        out_shape=[jax.ShapeDtypeStruct((B, S, NSA_QW), BF16),
                   jax.ShapeDtypeStruct((2, B, S // CMP_STRIDE, CMP_STRIDE * LANES), BF16),
                   jax.ShapeDtypeStruct((B, NSA_GROUPS, S, LANES), BF16),
                   vt_shape,
                   jax.ShapeDtypeStruct((B, S, LANES), BF16),
                   vt_shape,
                   jax.ShapeDtypeStruct((B, S, LANES), F32),
                   jax.ShapeDtypeStruct((B, S, RET_QW), BF16),
                   jax.ShapeDtypeStruct((B, S, RET_QW), BF16),
                   jax.ShapeDtypeStruct((B, S, RET_VW), BF16)],
        scratch_shapes=[pltpu.VMEM((tm, D), BF16), pltpu.VMEM((2, tm, LANES), F32)],
        compiler_params=_params(("arbitrary", "arbitrary")),
        name="projection",
    )(h, mod, norm_w.reshape(1, D), wq, wkv, wg, wr, cos_t, sin_t)
    return outs


def _compress_kernel(ch_ref, w1c_ref, pe_ref, w1_ref, w2_ref, o_ref, ot_ref):
    nc = ch_ref.shape[2]
    nh = NSA_GROUPS * CMP_HIDDEN
    kinds = range(ch_ref.shape[0])
    ab = [_dot(ch_ref[t, 0], w1c_ref[t]) for t in kinds]
    pts = [_dot(pe_ref[t], w1_ref[t])[0:1, :] for t in kinds]
    hids = [ab[t][:, :nh] + pltpu.roll(ab[t][:, nh:], nc - 1, 0) + jnp.concatenate([pts[t], pts[t]], axis=1)
            for t in kinds]
    outs = [_dot(_silu(hids[t]).astype(BF16), w2_ref[t]) for t in kinds]
    row = lax.broadcasted_iota(jnp.int32, (nc, LANES), 0)
    for t in kinds:
        out = jnp.where(row < nc - 1, outs[t], 0.0)
        o_ref[t, 0] = out.astype(BF16)
        ot_ref[t, 0, 0:LANES, :] = out.T.astype(BF16)
        ot_ref[t, 0, LANES:, :] = jnp.ones((ot_ref.shape[2] - LANES, nc), BF16)


def _compress(cv, pe, w1, w2):
    _, B, nc, width = cv.shape
    chunks = cv
    eye = jnp.eye(NSA_GROUPS, dtype=F32)
    nh = NSA_GROUPS * CMP_HIDDEN

    def expand(w):
        w = w.reshape(CMP_STRIDE, NSA_DH, CMP_HIDDEN)
        return jnp.einsum("ldj,gh->lgdhj", w, eye).reshape(width, nh)

    half = CMP_STRIDE * NSA_DH
    w1c = jnp.stack([jnp.concatenate([expand(w[:half]), expand(w[half:])], axis=1) for w in w1]).astype(BF16)
    w2bd = jnp.stack([jnp.einsum("jd,gh->gjhd", w, eye).reshape(nh, NSA_KVW) for w in w2]).astype(BF16)
    pe8 = jnp.stack([jnp.broadcast_to(p.reshape(1, CMP_LEN * NSA_DH), (8, CMP_LEN * NSA_DH)) for p in pe]).astype(BF16)
    w1s = jnp.stack(w1).astype(BF16)
    return pl.pallas_call(
        _compress_kernel,
        grid=(B,),
        in_specs=[pl.BlockSpec((2, 1, nc, width), lambda b: (0, b, 0, 0)),
                  _resident_spec(w1c.shape), _resident_spec(pe8.shape), _resident_spec(w1s.shape),
                  _resident_spec(w2bd.shape)],
        out_specs=[pl.BlockSpec((2, 1, nc, LANES), lambda b: (0, b, 0, 0)),
                   pl.BlockSpec((2, 1, LANES + BF16_ROWS, nc), lambda b: (0, b, 0, 0))],
        out_shape=[jax.ShapeDtypeStruct((2, B, nc, LANES), BF16),
                   jax.ShapeDtypeStruct((2, B, LANES + BF16_ROWS, nc), BF16)],
        compiler_params=_params(("arbitrary",)),
        name="compress",
    )(chunks, w1c, pe8, w1s, w2bd)


def _select_kernel(q_ref, kc_ref, vct_ref, g_ref, ovl_ref, qaug_ref, ocmp_ref, imp_scr, rank_scr, cap_scr,
                   *, n_sel, n_tiles):
    Q = q_ref.shape[1]
    NC = kc_ref.shape[2]
    NB = ovl_ref.shape[0]
    qi = pl.program_id(1)
    q0 = qi * Q
    gt = g_ref[0].T
    any_allowed = q0 + lax.broadcasted_iota(jnp.int32, (1, Q), 1) >= CMP_LEN - 1
    j_blk = lax.broadcasted_iota(jnp.int32, (NB, Q), 0)
    t_blk = q0 + lax.broadcasted_iota(jnp.int32, (NB, Q), 1)
    cur = t_blk // SEL_BLOCK
    forced = (j_blk == 0) | (j_blk == cur) | (j_blk == cur - 1)
    valid = j_blk * SEL_BLOCK <= t_blk
    j_sub = lax.broadcasted_iota(jnp.int32, (8, Q), 0)
    per_tile = Q // SEL_BLOCK

    def compressed_branch(rows):
        kcf = kc_ref[0, 0, 0:rows, :].astype(F32)
        kcr = pltpu.roll(kcf, HALF, 1)
        lowk = lax.broadcasted_iota(jnp.int32, (rows, LANES), 1) < HALF
        k_even = [jnp.where(lowk, kcf, 0.0).astype(BF16), jnp.where(lowk, kcr, 0.0).astype(BF16)]
        k_odd = [jnp.where(lowk, 0.0, kcr).astype(BF16), jnp.where(lowk, 0.0, kcf).astype(BF16)]
        vct = vct_ref[0, 0, :, 0:rows]
        ovl = ovl_ref[:, 0:rows]
        n_row = lax.broadcasted_iota(jnp.int32, (rows, Q), 0)
        t_col = q0 + lax.broadcasted_iota(jnp.int32, (rows, Q), 1)
        cap_scr[0:rows, :] = jnp.where(n_row * CMP_STRIDE + (CMP_LEN - 1) <= t_col, -NEG, NEG)

        scores = []
        for pb in range(NSA_HEADS // 2):
            g, pp = divmod(pb, NSA_HPG // 2)
            qt = q_ref[0, :, pb * LANES:(pb + 1) * LANES].astype(F32).T.astype(BF16)
            for e, kk in enumerate((k_even[g], k_odd[g])):
                scores.append(_dot(kk, qt))
                col = (2 * pp + e) * Q
                qaug_ref[0, g, 0, 0:HALF, col:col + Q] = qt[e * HALF:(e + 1) * HALF]

        for g in range(NSA_GROUPS):
            imp = jnp.zeros((NB, Q), F32)
            for hh in range(NSA_HPG):
                h = g * NSA_HPG + hh
                s = jnp.minimum(scores[h], cap_scr[0:rows, :])
                mx = jnp.max(s, axis=0, keepdims=True)
                ex = jnp.exp2(s - mx).astype(BF16)
                oa = _dot(vct, ex)
                inv = jnp.where(any_allowed, 1.0 / oa[LANES:LANES + 1], 0.0)
                ocmp_ref[0, 0, h * NSA_DH:(h + 1) * NSA_DH, :] = (gt[3 * h:3 * h + 1] * inv) * oa[g * HALF:(g + 1) * HALF]
                imp = imp + _dot(ovl, ex) * inv

            imp = jnp.where(forced, imp + FORCE_BONUS, imp)
            imp_scr[g] = jnp.where(valid, imp, -FORCE_BONUS)

    parts = CMP_PARTS if n_tiles % CMP_PARTS == 0 and NC % (16 * CMP_PARTS) == 0 else 1
    for c in range(parts):
        @pl.when((qi >= c * (n_tiles // parts)) & (qi < (c + 1) * (n_tiles // parts)))
        def _():
            compressed_branch((c + 1) * (NC // parts))
    rank_scr[...] = jnp.zeros_like(rank_scr)

    for c0 in range(0, NB, RANK_SPAN):
        @pl.when(c0 // per_tile <= qi)
        def _():
            slabs = range(NB // 8)
            for g in range(NSA_GROUPS):
                vals = [imp_scr[g, 8 * sb:8 * sb + 8, :] for sb in slabs]
                cnts = [rank_scr[g, 8 * sb:8 * sb + 8, :] for sb in slabs]
                for jp in range(c0, c0 + RANK_SPAN):
                    r = jnp.broadcast_to(imp_scr[g, jp:jp + 1, :], (8, Q))
                    for sb in slabs:
                        v = vals[sb]
                        if 8 * sb > jp:
                            ahead = r >= v
                        elif 8 * sb + 7 <= jp:
                            ahead = r > v
                        else:
                            ahead = (r > v) | ((r == v) & (j_sub + 8 * sb > jp))
                        cnts[sb] = jnp.where(ahead, cnts[sb] + 1, cnts[sb])
                for sb in slabs:
                    rank_scr[g, 8 * sb:8 * sb + 8, :] = cnts[sb]

    for g in range(NSA_GROUPS):
        sel = (rank_scr[g] < n_sel) & valid
        bias = jnp.where(sel, 0.0, NEG).astype(BF16)
        for hh in range(NSA_HPG):
            qaug_ref[0, g, 0, HALF:HALF + NB, hh * Q:(hh + 1) * Q] = bias


def _select(q, kc, vct, gates):
    B, S, _ = q.shape
    Q = min(ATT_Q, S)
    NC = kc.shape[2]
    nc, ns = NC - 1, S // SEL_BLOCK
    NB = LANES - HALF
    assert ns <= NB and Q % SEL_BLOCK == 0
    c_start = np.arange(nc) * CMP_STRIDE
    s_start = np.arange(ns) * SEL_BLOCK
    overlap = ((c_start[:, None] < s_start[None, :] + SEL_BLOCK) &
               (c_start[:, None] + CMP_LEN > s_start[None, :])).astype(np.float32)
    ovl = np.zeros((NB, NC), np.float32)
    ovl[:ns, :nc] = overlap.T
    kern = functools.partial(_select_kernel, n_sel=min(SEL_TOPK, ns), n_tiles=S // Q)
    return pl.pallas_call(
        kern,
        grid=(B, S // Q),
        in_specs=[pl.BlockSpec((1, Q, NSA_QW), lambda b, i: (b, i, 0)),
                  pl.BlockSpec((1, 1, NC, LANES), lambda b, i: (0, b, 0, 0)),
                  pl.BlockSpec((1, 1) + vct.shape[2:], lambda b, i: (1, b, 0, 0)),
                  pl.BlockSpec((1, Q, LANES), lambda b, i: (b, i, 0)),
                  _full_spec((NB, NC))],
        out_specs=[pl.BlockSpec((1, NSA_GROUPS, 1, LANES, NSA_HPG * Q), lambda b, i: (b, 0, i, 0, 0)),
                   pl.BlockSpec((1, 1, NSA_QW, Q), lambda b, i: (b, i, 0, 0))],
        out_shape=[jax.ShapeDtypeStruct((B, NSA_GROUPS, S // Q, LANES, NSA_HPG * Q), BF16),
                   jax.ShapeDtypeStruct((B, S // Q, NSA_QW, Q), F32)],
        scratch_shapes=[pltpu.VMEM((NSA_GROUPS, NB, Q), F32), pltpu.VMEM((NSA_GROUPS, NB, Q), jnp.int32),
                        pltpu.VMEM((NC, Q), F32)],
        compiler_params=_params(("arbitrary", "arbitrary")),
        name="select",
    )(q, kc, vct, gates, jnp.asarray(ovl, BF16))


def _flash_kernel(qaug_ref, kaug_ref, vst_ref, kw_ref, vwt_ref, g_ref, ocmp_ref, o_ref,
                  m_scr, acc_scr, qw_scr, cap_scr, s_scr):
    QS = qaug_ref.shape[2]
    R = qaug_ref.shape[4]
    Q = R // NSA_HPG
    KT = kaug_ref.shape[2] // vst_ref.shape[2]
    WT = WINDOW // KT
    SLC, WIN = 0, 1
    CAUSAL, EDGE = 0, 1
    t0 = pl.program_id(1) * QS
    k_row = lax.broadcasted_iota(jnp.int32, (KT, Q), 0)
    i_col = lax.broadcasted_iota(jnp.int32, (KT, Q), 1)
    big = -NEG
    cap_scr[CAUSAL] = jnp.where(k_row <= i_col, big, NEG)
    cap_scr[EDGE] = jnp.where(k_row > i_col, big, NEG)
    m_scr[SLC] = jnp.full(m_scr.shape[1:], NEG, F32)
    acc_scr[SLC] = jnp.zeros(acc_scr.shape[1:], F32)
    zeros = jnp.zeros((HALF, R), BF16)
    for qs in range(QS):
        qw_scr[qs, 0] = jnp.concatenate([qaug_ref[0, 0, qs, 0:HALF, :], zeros], axis=0)
        qw_scr[qs, 1] = jnp.concatenate([zeros, qaug_ref[0, 1, qs, 0:HALF, :]], axis=0)

    def scores(ch):
        br, qs, g, hh, kj, _, _ = ch
        off = pl.multiple_of(kj * KT, KT)
        if br == SLC:
            return _dot(kaug_ref[0, g, pl.ds(off, KT), :], qaug_ref[0, g, qs, :, hh * Q:(hh + 1) * Q])
        return _dot(kw_ref[0, pl.ds(off, KT), :], qw_scr[qs, g, :, hh * Q:(hh + 1) * Q])

    def absorb(ch, s):
        br, qs, g, hh, kj, cap, fresh = ch
        cols = slice(hh * Q, (hh + 1) * Q)
        if cap is not None:
            s = jnp.minimum(s, cap_scr[cap])
        vt = vst_ref[0, g, kj] if br == SLC else vwt_ref[0, g, kj]
        if fresh:
            m_new = jnp.max(s, axis=0, keepdims=True)
            acc_scr[br, qs, g, :, cols] = _dot(vt, jnp.exp2(s - m_new).astype(BF16))
        else:
            m_prev = m_scr[br, qs, g, :, cols]
            m_new = jnp.maximum(m_prev, jnp.max(s, axis=0, keepdims=True))
            p = jnp.exp2(s - m_new)
            alpha = jnp.exp2(m_prev - m_new)
            acc_scr[br, qs, g, :, cols] = alpha * acc_scr[br, qs, g, :, cols] + _dot(vt, p.astype(BF16))
        m_scr[br, qs, g, :, cols] = m_new

    NSH = QS * NSA_HEADS
    RING = 2 * NSH

    def run(chains, ready=()):
        order = list(ready) + list(chains)
        assert len(ready) + RUN_AHEAD < RING
        for t in range(-RUN_AHEAD, len(order)):
            n = t + RUN_AHEAD
            if len(ready) <= n < len(order):
                s_scr[(n % RING) // NSH, n % NSH] = scores(order[n])
            if t >= 0:
                absorb(order[t], s_scr[(t % RING) // NSH, t % NSH])

    def tile(br, qs, kj, cap, fresh=False):
        return [(br, qs, g, hh, kj, cap, fresh) for g in range(NSA_GROUPS) for hh in range(NSA_HPG)]

    def shared(kj, first_cap=None):
        return [ch for qs in range(QS) for ch in tile(SLC, qs, kj, first_cap if qs == 0 else None)]

    def issue(kj, slot, n):
        s_scr[slot, n] = scores(shared(kj)[n])

    for n in range(NSH):
        issue(0, 0, n)

    def pipelined(first, count):
        produce = [(first + k + 1, (k + 1) % 2, n) for k in range(count) for n in range(NSH)]
        absorbs = [(first + k, k % 2, n) for k in range(count) for n in range(NSH)]
        for t in range(-LOOP_AHEAD, len(absorbs)):
            if t + LOOP_AHEAD < len(produce):
                issue(*produce[t + LOOP_AHEAD])
            if t >= 0:
                kj, slot, n = absorbs[t]
                absorb(shared(kj)[n], s_scr[slot, n])

    def slc_body(j, carry):
        pipelined(LOOP_TILES * j, LOOP_TILES)
        return carry

    lax.fori_loop(0, t0 // LOOP_TILES, slc_body, 0)
    for left in range(2, LOOP_TILES, 2):
        @pl.when(t0 % LOOP_TILES == left)
        def _():
            pipelined(t0 - left, left)
    ready = shared(t0, CAUSAL)

    always = []
    later = {}
    for qs in range(QS):
        for kk in range(1, qs):
            always += tile(SLC, qs, t0 + kk, None)
        if qs > 0:
            always += tile(SLC, qs, t0 + qs, CAUSAL)
        always += tile(WIN, qs, t0 + qs, CAUSAL, fresh=True)
        for back in range(1, WT + 1):
            chains = tile(WIN, qs, t0 + qs - back, EDGE if back == WT else None)
            if back <= qs:
                always += chains
            else:
                later.setdefault(-(-(back - qs) // QS) * QS, []).extend(chains)
    if len(later) == 1:
        (need, chains), = later.items()

        @pl.when(t0 >= need)
        def _():
            run(always + chains, ready)

        @pl.when(t0 < need)
        def _():
            run(always, ready)
    else:
        run(always, ready)
        for need, chains in sorted(later.items()):
            @pl.when(t0 >= need)
            def _():
                run(chains)

    for qs in range(QS):
        gt = g_ref[0, qs * Q:(qs + 1) * Q, :].T
        for pb in range(NSA_HEADS // 2):
            halves = []
            for e in range(2):
                h = 2 * pb + e
                g, hh = divmod(h, NSA_HPG)
                cols = slice(hh * Q, (hh + 1) * Q)
                o = ocmp_ref[0, qs, h * NSA_DH:(h + 1) * NSA_DH, :]
                for br in (SLC, WIN):
                    scale = gt[3 * h + 1 + br:3 * h + 2 + br] / acc_scr[br, qs, g, NSA_DH:NSA_DH + 1, cols]
                    o = o + scale * acc_scr[br, qs, g, 0:NSA_DH, cols]
                halves.append(o)
            o_ref[0, qs * Q:(qs + 1) * Q, pb * LANES:(pb + 1) * LANES] = jnp.concatenate(halves, axis=0).T.astype(BF16)


def _flash(qaug, kaug, vst, kw, vwt, gates, ocmp):
    B, _, NQ, _, R = qaug.shape
    Q = R // NSA_HPG
    S = NQ * Q
    QS = min(FLASH_SUBTILES, NQ)
    assert WINDOW % Q == 0 and vst.shape[4] == Q and NQ % QS == 0 and QS % 2 == 0
    per_b = lambda shape: pl.BlockSpec((1,) + shape, lambda b, i: (b,) + (0,) * len(shape))
    return pl.pallas_call(
        _flash_kernel,
        grid=(B, NQ // QS),
        in_specs=[pl.BlockSpec((1, NSA_GROUPS, QS, LANES, R), lambda b, i: (b, 0, i, 0, 0)),
                  per_b((NSA_GROUPS, S, LANES)), per_b(vst.shape[1:]), per_b((S, LANES)), per_b(vwt.shape[1:]),
                  pl.BlockSpec((1, QS * Q, LANES), lambda b, i: (b, i, 0)),
                  pl.BlockSpec((1, QS, NSA_QW, Q), lambda b, i: (b, i, 0, 0))],
        out_specs=pl.BlockSpec((1, QS * Q, NSA_QW), lambda b, i: (b, i, 0)),
        out_shape=jax.ShapeDtypeStruct((B, S, NSA_QW), BF16),
        scratch_shapes=[pltpu.VMEM((2, QS, NSA_GROUPS, 1, R), F32),
                        pltpu.VMEM((2, QS, NSA_GROUPS, VAL_ROWS, R), F32),
                        pltpu.VMEM((QS, NSA_GROUPS, LANES, R), BF16), pltpu.VMEM((2, Q, Q), F32),
                        pltpu.VMEM((2, QS * NSA_HEADS, Q, Q), F32)],
        compiler_params=_params(("arbitrary", "arbitrary")),
        name="flash",
    )(qaug, kaug, vst, kw, vwt, gates, ocmp)


def _retention_kernel(rq_ref, rk_ref, rv_ref, dmat_ref, qdec_ref, kdec_ref, cdec_ref, nw_ref, o_ref, st_scr):
    C = dmat_ref.shape[1]

    @pl.when(pl.program_id(1) == 0)
    def _():
        st_scr[...] = jnp.zeros_like(st_scr)

    low = lax.broadcasted_iota(jnp.int32, (C, LANES), 1) < HALF
    same_head = ((lax.broadcasted_iota(jnp.int32, (LANES, LANES), 0) < HALF) ==
                 (lax.broadcasted_iota(jnp.int32, (LANES, LANES), 1) < HALF))
    head_mean = jnp.where(same_head, 1.0 / RET_DV, 0.0).astype(BF16)
    n_chunks = rq_ref.shape[1] // C
    pairs = [(cc, p) for cc in range(n_chunks) for p in range(RET_HEADS // 2)]
    rows = lambda cc: slice(cc * C, (cc + 1) * C)
    lanes = lambda p: slice(p * LANES, (p + 1) * LANES)
    state = [st_scr[p] for p in range(RET_HEADS // 2)]
    scores, cross = {}, {}
    for cc, p in pairs:
        q2, k2, v2 = rq_ref[0, rows(cc), lanes(p)], rk_ref[0, rows(cc), lanes(p)], rv_ref[0, rows(cc), lanes(p)]
        q2f = q2.astype(F32)
        qe = jnp.where(low, q2f, 0.0).astype(BF16)
        qo = jnp.where(low, 0.0, q2f).astype(BF16)
        scores[cc, p] = (_dot_nt(qe, k2), _dot_nt(qo, k2))
        cross[cc, p] = _dot(q2, state[p].astype(BF16))
        kd = (k2.astype(F32) * kdec_ref[:, lanes(p)]).astype(BF16)
        state[p] = state[p] * cdec_ref[p:p + 1, :] + jnp.where(same_head, _dot_tn(kd, v2), 0.0)
    for p in range(RET_HEADS // 2):
        st_scr[p] = state[p]
    outs = {}
    for cc, p in pairs:
        v2 = rv_ref[0, rows(cc), lanes(p)]
        ie = (scores[cc, p][0] * dmat_ref[2 * p]).astype(BF16)
        io = (scores[cc, p][1] * dmat_ref[2 * p + 1]).astype(BF16)
        intra = jnp.where(low, _dot(ie, v2), _dot(io, v2))
        outs[cc, p] = intra + cross[cc, p] * qdec_ref[:, lanes(p)]
    means = {key: _dot(o.astype(BF16), head_mean) for key, o in outs.items()}
    devs = {key: outs[key] - means[key] for key in outs}
    variances = {key: _dot((d * d).astype(BF16), head_mean) for key, d in devs.items()}
    for cc, p in pairs:
        o_ref[0, rows(cc), lanes(p)] = devs[cc, p] * lax.rsqrt(variances[cc, p] + EPS) * nw_ref[:, lanes(p)]


def _retention(rq, rk, rv, ret_norm_w):
    B, S, _ = rq.shape
    C = min(RET_CHUNK, S)
    H = RET_HEADS
    log_g = np.log(1.0 - 2.0 ** (-5.0 - np.arange(H, dtype=np.float64)))
    i = np.arange(C, dtype=np.float64)
    diff = i[:, None] - i[None, :]
    as_f32 = lambda a: jnp.asarray(a, F32)
    dmat = as_f32(np.where(diff >= 0, np.exp(log_g[:, None, None] * np.maximum(diff, 0.0)), 0.0))
    per_lane = lambda a: np.repeat(a, RET_DK, axis=-1)
    qdec = as_f32(per_lane(np.exp(log_g[None, :] * (i[:, None] + 1.0))))
    kdec = as_f32(per_lane(np.exp(log_g[None, :] * (C - 1.0 - i[:, None]))))
    cdec = as_f32(per_lane(np.exp(log_g * C)[None, :]).reshape(H // 2, LANES))
    span = C * min(RET_CHUNKS_PER_STEP, S // C)
    tok = lambda w: pl.BlockSpec((1, span, w), lambda b, i: (b, i, 0))
    return pl.pallas_call(
        _retention_kernel,
        grid=(B, S // span),
        in_specs=[tok(RET_QW), tok(RET_QW), tok(RET_VW),
                  _full_spec((H, C, C)), _full_spec((C, RET_QW)), _full_spec((C, RET_QW)),
                  _full_spec((H // 2, LANES)), _full_spec((1, RET_VW))],
        out_specs=tok(RET_VW),
        out_shape=jax.ShapeDtypeStruct((B, S, RET_VW), F32),
        scratch_shapes=[pltpu.VMEM((H // 2, LANES, LANES), F32)],
        compiler_params=_params(("arbitrary", "arbitrary")),
        name="retention",
    )(rq, rk, rv, dmat, qdec, kdec, cdec, ret_norm_w.reshape(1, RET_VW))


def _mix_kernel(h_ref, mod_ref, nw_ref, onsa_ref, oret_ref, wg_ref, wn_ref, wr_ref, wo_ref, o_ref):
    x = h_ref[0]
    sh = mod_ref[0, 3:4, :]
    sc = mod_ref[0, 4:5, :]
    gt = mod_ref[0, 5:6, :]
    u = _norm_mod(x, nw_ref[...], sc, sh).astype(BF16)
    gates = _dot_nt(u, wg_ref[...])
    y_nsa = _dot(onsa_ref[0], wn_ref[...])
    oret = (oret_ref[0] * _silu(gates[:, 0:RET_VW])).astype(BF16)
    y_ret = _dot(oret, wr_ref[...])
    D = x.shape[1]
    ga = jax.nn.sigmoid(gates[:, RET_VW:RET_VW + D])
    gb = jax.nn.sigmoid(gates[:, RET_VW + D:RET_VW + 2 * D])
    mixed = (ga * y_nsa + gb * y_ret).astype(BF16)
    o_ref[0] = x + gt * _dot(mixed, wo_ref[...])


def _mix(h, mod, norm_w, o_nsa, o_ret, w_in, w_nsa_up, w_ret_up, w_out, *, tm):
    B, S, D = h.shape
    o = np.cumsum([0, NSA_QW] + [NSA_KVW] * 6 + [3 * NSA_HEADS, RET_QW, RET_QW, RET_VW, RET_VW, D, D])
    wt = w_in.T
    wg = wt[o[11]:o[14]].astype(BF16)
    wn, wr, wo = w_nsa_up.astype(BF16), w_ret_up.astype(BF16), w_out.astype(BF16)
    tok = lambda w: pl.BlockSpec((1, tm, w), lambda b, i: (b, i, 0))
    return pl.pallas_call(
        _mix_kernel,
        grid=(B, S // tm),
        in_specs=[tok(D), pl.BlockSpec((1, N_MOD, D), lambda b, i: (b, 0, 0)), _full_spec((1, D)),
                  tok(NSA_QW), tok(RET_VW),
                  _resident_spec(wg.shape),
                  _resident_spec(wn.shape), _resident_spec(wr.shape), _resident_spec(wo.shape)],
        out_specs=tok(D),
        out_shape=jax.ShapeDtypeStruct((B, S, D), F32),
        compiler_params=_params(("arbitrary", "arbitrary")),
        name="mix",
    )(h, mod, norm_w.reshape(1, D), o_nsa, o_ret, wg, wn, wr, wo)


def kernel(x, c, ada_w, ada_b, norm1_w, ffn1_w_in, ffn1_w_out, norm2_w, w_in, cmp_k_pe, cmp_k_w1, cmp_k_w2,
           cmp_v_pe, cmp_v_w1, cmp_v_w2, ret_norm_w, w_nsa_up, w_ret_up, w_out, norm3_w, ffn2_w_in, ffn2_w_out,
           final_norm_w):
    B, S, D = x.shape
    depth = ada_w.shape[0]
    assert depth >= 1
    tm = min(TOKEN_TILE, S)
    tm_wide = min(WIDE_TOKEN_TILE, S)
    h = x
    for l in range(depth):
        last = l == depth - 1
        mod = _modulation(c, ada_w[l], ada_b[l]).reshape(B, N_MOD, D)
        h = _ffn(h, mod, norm1_w[l], final_norm_w, ffn1_w_in[l], ffn1_w_out[l], mod_base=0, final_norm=False, tm=tm_wide)
        q, cv, kaug, vst, kw, vwt, gates, rq, rk, rv = _projection(h, mod, norm2_w[l], w_in[l], tm=tm)
        kvc, kvct = _compress(cv, (cmp_k_pe[l], cmp_v_pe[l]), (cmp_k_w1[l], cmp_v_w1[l]), (cmp_k_w2[l], cmp_v_w2[l]))
        qaug, ocmp = _select(q, kvc, kvct, gates)
        o_nsa = _flash(qaug, kaug, vst, kw, vwt, gates, ocmp)
        o_ret = _retention(rq, rk, rv, ret_norm_w[l])
        h = _mix(h, mod, norm2_w[l], o_nsa, o_ret, w_in[l], w_nsa_up[l], w_ret_up[l], w_out[l], tm=tm_wide)
        h = _ffn(h, mod, norm3_w[l], final_norm_w, ffn2_w_in[l], ffn2_w_out[l], mod_base=6, final_norm=last, tm=tm_wide)
    return h
```
